```python
import jax
import jax.numpy as jnp
from jax import lax
import numpy as np

D_MODEL = 2048
BATCH = 8
SEQ = 2048
DEPTH = 1

GRID_W = 64
N_META = 16
NA_HEADS = 16
NA_HEAD_DIM = 64
NA_WIN_H_MAX = 8
NA_WIN_W = 16
NA_QBLOCK_W = NA_WIN_W
NA_KBLOCK_W = 2 * NA_WIN_W
MLA_HEADS = 16
MLA_Q_RANK = 512
MLA_KV_RANK = 512
MLA_NOPE_DIM = 128
MLA_ROPE_DIM = 64
MLA_V_DIM = 128
MLA_QBLOCK = 128
ROPE_THETA = 10000.0
PEER_HEADS = 8
PEER_NKEYS = 128
PEER_EXPERTS = PEER_NKEYS * PEER_NKEYS
PEER_DK = 256
PEER_TOPK = 16
PEER_TOKEN_CHUNK = 16
NORM_EPS = 1e-6
NEG_INF = -1e30

NA_WIDTH = NA_HEADS * NA_HEAD_DIM
MLA_Q_WIDTH = MLA_HEADS * (MLA_NOPE_DIM + MLA_ROPE_DIM)
MLA_KV_WIDTH = MLA_HEADS * (MLA_NOPE_DIM + MLA_V_DIM)
MLA_OUT_WIDTH = MLA_HEADS * MLA_V_DIM
IN_SIZES = (NA_WIDTH, NA_WIDTH, NA_WIDTH, MLA_Q_RANK, MLA_KV_RANK, MLA_ROPE_DIM, D_MODEL, D_MODEL)
IN_WIDTH = sum(IN_SIZES)

kernel_name = 'hybrid_na_mla_peer_block'


def rmsnorm(x, g):
    xf = x.astype(jnp.float32)
    y = xf * lax.rsqrt(jnp.mean(xf * xf, axis=-1, keepdims=True) + NORM_EPS)
    return (y * g.astype(jnp.float32)).astype(x.dtype)


def rope(x, cos, sin):
    half = x.shape[-1] // 2
    x1 = x[..., :half].astype(jnp.float32)
    x2 = x[..., half:].astype(jnp.float32)
    return jnp.concatenate([x1 * cos - x2 * sin, x2 * cos + x1 * sin], axis=-1).astype(x.dtype)


def _na_static(rows):
    wh = min(NA_WIN_H_MAX, rows)
    r = np.arange(rows)
    row_start = np.clip(r - wh // 2, 0, rows - wh)
    row_off = row_start[:, None] + np.arange(wh)[None] - r[:, None] + (NA_WIN_H_MAX - 1)
    ncb = GRID_W // NA_QBLOCK_W
    qcol = np.arange(ncb)[:, None] * NA_QBLOCK_W + np.arange(NA_QBLOCK_W)[None]
    kcol0 = np.clip(np.arange(ncb) * NA_QBLOCK_W - NA_WIN_W // 2, 0, GRID_W - NA_KBLOCK_W)
    kcol = kcol0[:, None] + np.arange(NA_KBLOCK_W)[None]
    cstart = np.clip(qcol - NA_WIN_W // 2, 0, GRID_W - NA_WIN_W)
    kc = kcol[:, None, :]
    col_mask = (kc >= cstart[:, :, None]) & (kc < cstart[:, :, None] + NA_WIN_W)
    col_off = np.clip(kc - qcol[:, :, None], -(NA_WIN_W - 1), NA_WIN_W - 1) + NA_WIN_W - 1
    return wh, row_start.astype(np.int32), row_off, kcol, col_mask, col_off


def neighbourhood_attention(q, k, v, rel_bias, meta_bias):
    b, t, h, dh = q.shape
    n = t - N_META
    rows = n // GRID_W
    wh, row_start, row_off, kcol, col_mask, col_off = _na_static(rows)
    ncb = GRID_W // NA_QBLOCK_W
    scale = dh ** -0.5
    qm, km, vm = q[:, :N_META], k[:, :N_META], v[:, :N_META]
    qg = q[:, N_META:].reshape(b, rows, GRID_W, h, dh)
    kg = k[:, N_META:].reshape(b, rows, GRID_W, h, dh)
    vg = v[:, N_META:].reshape(b, rows, GRID_W, h, dh)
    mb = meta_bias.astype(jnp.float32)
    sm = jnp.einsum('bqhd,bkhd->bhqk', qm, km).astype(jnp.float32) * scale + mb[None, :, None, :]
    om = jnp.einsum('bhqk,bkhd->bqhd', jax.nn.softmax(sm, axis=-1).astype(v.dtype), vm)
    bias = rel_bias.astype(jnp.float32)[:, row_off[:, :, None, None, None], col_off[None, None]]
    bias = bias.transpose(1, 0, 3, 4, 2, 5)
    bias = jnp.where(jnp.asarray(col_mask)[None, None, :, :, None, :], bias, NEG_INF)
    q_rows = qg.reshape(b, rows, ncb, NA_QBLOCK_W, h, dh).transpose(1, 0, 2, 3, 4, 5)
    n_loc = wh * NA_KBLOCK_W

    def row_fn(args):
        q_r, rs, bias_r = args
        k_blk = lax.dynamic_slice_in_dim(kg, rs, wh, axis=1)[:, :, kcol]
        v_blk = lax.dynamic_slice_in_dim(vg, rs, wh, axis=1)[:, :, kcol]
        s_loc = jnp.einsum('bjqhd,bwjkhd->bhjqwk', q_r, k_blk).astype(jnp.float32) * scale + bias_r[None]
        s_meta = jnp.einsum('bjqhd,bmhd->bhjqm', q_r, km).astype(jnp.float32) * scale + mb[None, :, None, None, :]
        s = jnp.concatenate([s_loc.reshape(b, h, ncb, NA_QBLOCK_W, n_loc), s_meta], axis=-1)
        p = jax.nn.softmax(s, axis=-1).astype(v.dtype)
        p_loc = p[..., :n_loc].reshape(b, h, ncb, NA_QBLOCK_W, wh, NA_KBLOCK_W)
        return (jnp.einsum('bhjqwk,bwjkhd->bjqhd', p_loc, v_blk)
                + jnp.einsum('bhjqm,bmhd->bjqhd', p[..., n_loc:], vm))

    o_rows = lax.map(row_fn, (q_rows, jnp.asarray(row_start), bias))
    og = o_rows.transpose(1, 0, 2, 3, 4, 5).reshape(b, n, h, dh)
    return jnp.concatenate([om, og], axis=1)


def latent_attention(c_q, c_kv, k_rope_raw, q_norm_g, w_uq, kv_norm_g, w_ukv, cos, sin):
    b, t, _ = c_q.shape
    q = (rmsnorm(c_q, q_norm_g) @ w_uq).reshape(b, t, MLA_HEADS, MLA_NOPE_DIM + MLA_ROPE_DIM)
    q_nope = q[..., :MLA_NOPE_DIM]
    q_rope = rope(q[..., MLA_NOPE_DIM:], cos[:, None, :], sin[:, None, :])
    kv = (rmsnorm(c_kv, kv_norm_g) @ w_ukv).reshape(b, t, MLA_HEADS, MLA_NOPE_DIM + MLA_V_DIM)
    k_nope, val = kv[..., :MLA_NOPE_DIM], kv[..., MLA_NOPE_DIM:]
    k_rope = rope(k_rope_raw, cos, sin)
    scale = (MLA_NOPE_DIM + MLA_ROPE_DIM) ** -0.5

    def attend(qn, qr):
        s = (jnp.einsum('bqhd,bkhd->bhqk', qn, k_nope)
             + jnp.einsum('bqhr,bkr->bhqk', qr, k_rope)).astype(jnp.float32) * scale
        p = jax.nn.softmax(s, axis=-1).astype(val.dtype)
        return jnp.einsum('bhqk,bkhd->bqhd', p, val)

    o_meta = attend(q_nope[:, :N_META], q_rope[:, :N_META])
    n = t - N_META
    nb = n // MLA_QBLOCK

    def blocks(a):
        return a[:, N_META:].reshape((b, nb, MLA_QBLOCK) + a.shape[2:]).swapaxes(0, 1)

    o_real = lax.map(lambda a: attend(a[0], a[1]), (blocks(q_nope), blocks(q_rope)))
    o_real = o_real.swapaxes(0, 1).reshape(b, n, MLA_HEADS, MLA_V_DIM)
    return jnp.concatenate([o_meta, o_real], axis=1).reshape(b, t, MLA_OUT_WIDTH)


def peer(hn, w_q, sub_keys, u, v):
    b, t, d = hn.shape
    q = (hn @ w_q).reshape(b, t, PEER_HEADS, 2, PEER_DK // 2)
    s = jnp.einsum('bthsc,hsnc->bthsn', q, sub_keys).astype(jnp.float32)
    s_top, i_top = lax.top_k(s, PEER_TOPK)
    cand = s_top[..., 0, :, None] + s_top[..., 1, None, :]
    c_top, c_idx = lax.top_k(cand.reshape(b, t, PEER_HEADS, PEER_TOPK * PEER_TOPK), PEER_TOPK)
    i1 = jnp.take_along_axis(i_top[..., 0, :], c_idx // PEER_TOPK, axis=-1)
    i2 = jnp.take_along_axis(i_top[..., 1, :], c_idx % PEER_TOPK, axis=-1)
    expert = i1 * PEER_NKEYS + i2
    g = jax.nn.softmax(c_top, axis=-1).astype(hn.dtype)
    nc = (b * t) // PEER_TOKEN_CHUNK
    hf = hn.reshape(nc, PEER_TOKEN_CHUNK, d)
    ef = expert.reshape(nc, PEER_TOKEN_CHUNK, PEER_HEADS * PEER_TOPK)
    gf = g.reshape(nc, PEER_TOKEN_CHUNK, PEER_HEADS * PEER_TOPK)

    def chunk_fn(a):
        h_c, e_c, g_c = a
        act = jax.nn.gelu(jnp.einsum('nd,nkd->nk', h_c, u[e_c]))
        return jnp.einsum('nk,nkd->nd', g_c * act, v[e_c])

    return lax.map(chunk_fn, (hf, ef, gf)).reshape(b, t, d)


def setup_inputs(seed: int = 0) -> dict:
    key = jax.random.key(seed)
    ks = jax.random.split(key, 19)

    def nrm(k, shape, scale):
        return jax.random.normal(k, shape, jnp.float32) * scale

    def gain(k, shape):
        return 1.0 + 0.02 * jax.random.normal(k, shape, jnp.float32)

    return {
        'x': nrm(ks[0], (BATCH, SEQ, D_MODEL), 1.0),
        'meta_tokens': nrm(ks[1], (N_META, D_MODEL), 1.0),
        'norm1_g': gain(ks[2], (DEPTH, D_MODEL)),
        'w_in': nrm(ks[3], (DEPTH, D_MODEL, IN_WIDTH), D_MODEL ** -0.5),
        'na_rel_bias': nrm(ks[4], (DEPTH, NA_HEADS, 2 * NA_WIN_H_MAX - 1, 2 * NA_WIN_W - 1), 0.1),
        'na_meta_bias': nrm(ks[5], (DEPTH, NA_HEADS, N_META), 0.1),
        'mla_q_norm_g': gain(ks[6], (DEPTH, MLA_Q_RANK)),
        'mla_w_uq': nrm(ks[7], (DEPTH, MLA_Q_RANK, MLA_Q_WIDTH), MLA_Q_RANK ** -0.5),
        'mla_kv_norm_g': gain(ks[8], (DEPTH, MLA_KV_RANK)),
        'mla_w_ukv': nrm(ks[9], (DEPTH, MLA_KV_RANK, MLA_KV_WIDTH), MLA_KV_RANK ** -0.5),
        'w_na_branch': nrm(ks[10], (DEPTH, NA_WIDTH, D_MODEL), NA_WIDTH ** -0.5),
        'w_mla_branch': nrm(ks[11], (DEPTH, MLA_OUT_WIDTH, D_MODEL), MLA_OUT_WIDTH ** -0.5),
        'w_out': nrm(ks[12], (DEPTH, D_MODEL, D_MODEL), D_MODEL ** -0.5),
        'norm2_g': gain(ks[13], (DEPTH, D_MODEL)),
        'peer_w_q': nrm(ks[14], (DEPTH, D_MODEL, PEER_HEADS * PEER_DK), D_MODEL ** -0.5),
        'peer_sub_keys': nrm(ks[15], (DEPTH, PEER_HEADS, 2, PEER_NKEYS, PEER_DK // 2), (PEER_DK // 2) ** -0.5),
        'peer_u': nrm(ks[16], (DEPTH, PEER_EXPERTS, D_MODEL), D_MODEL ** -0.5),
        'peer_v': nrm(ks[17], (DEPTH, PEER_EXPERTS, D_MODEL), PEER_HEADS ** -0.5),
        'final_norm_g': gain(ks[18], (D_MODEL,)),
    }


def reference(x, meta_tokens, norm1_g, w_in, na_rel_bias, na_meta_bias, mla_q_norm_g, mla_w_uq,
              mla_kv_norm_g, mla_w_ukv, w_na_branch, w_mla_branch, w_out, norm2_g, peer_w_q,
              peer_sub_keys, peer_u, peer_v, final_norm_g):
    b, s, d = x.shape
    t = s + N_META
    h = jnp.concatenate([jnp.broadcast_to(meta_tokens.astype(x.dtype)[None], (b, N_META, d)), x], axis=1)
    pos = jnp.arange(t, dtype=jnp.float32)
    inv_freq = ROPE_THETA ** (-jnp.arange(0, MLA_ROPE_DIM, 2, dtype=jnp.float32) / MLA_ROPE_DIM)
    ang = pos[:, None] * inv_freq[None, :]
    cos, sin = jnp.cos(ang), jnp.sin(ang)
    offsets = [int(o) for o in np.cumsum(IN_SIZES)[:-1]]
    for l in range(DEPTH):
        hn = rmsnorm(h, norm1_g[l])
        na_q, na_k, na_v, c_q, c_kv, k_rope_raw, gate_a, gate_b = jnp.split(hn @ w_in[l], offsets, axis=-1)
        hd = (b, t, NA_HEADS, NA_HEAD_DIM)
        o_na = neighbourhood_attention(na_q.reshape(hd), na_k.reshape(hd), na_v.reshape(hd),
                                       na_rel_bias[l], na_meta_bias[l]).reshape(b, t, NA_WIDTH)
        o_mla = latent_attention(c_q, c_kv, k_rope_raw, mla_q_norm_g[l], mla_w_uq[l],
                                 mla_kv_norm_g[l], mla_w_ukv[l], cos, sin)
        merged = (jax.nn.sigmoid(gate_a) * (o_na @ w_na_branch[l])
                  + jax.nn.sigmoid(gate_b) * (o_mla @ w_mla_branch[l]))
        h = h + merged @ w_out[l]
        h = h + peer(rmsnorm(h, norm2_g[l]), peer_w_q[l], peer_sub_keys[l], peer_u[l], peer_v[l])
    return rmsnorm(h[:, N_META:], final_norm_g)
```

```python
import functools

import jax
import jax.numpy as jnp
import numpy as np
from jax import lax
from jax.experimental import pallas as pl
from jax.experimental.pallas import tpu as pltpu

GRID_W = 64
N_META = 16
NA_HEADS = 16
NA_HEAD_DIM = 64
NA_WIN_H = 8
NA_WIN_W = 16
NA_WIDTH = NA_HEADS * NA_HEAD_DIM
MLA_HEADS = 16
MLA_RANK = 512
MLA_NOPE = 128
MLA_ROPE = 64
MLA_V = 128
MLA_QK_PAD = 256
ROPE_THETA = 10000.0
PEER_HEADS = 8
PEER_NKEYS = 128
PEER_DK = 256
PEER_TOPK = 16
NORM_EPS = 1e-6
NEG_INF = -1e30

V7X_VMEM_BYTES = 64 * 1024 * 1024
V7X_VMEM_HEADROOM = 6 * 1024 * 1024
V7X_LANES = 128

NA_QROWS = 4
NA_KROWS = NA_QROWS + NA_WIN_H

F32 = jnp.float32
BF16 = jnp.bfloat16
_F32_MIN = float(np.finfo(np.float32).min)


def _vmem_limit(estimate_bytes):
    want = int(estimate_bytes * 1.25) + 4 * 1024 * 1024
    return min(max(want, 16 * 1024 * 1024), V7X_VMEM_BYTES - V7X_VMEM_HEADROOM)


def _params(semantics, vmem_estimate):
    return pltpu.CompilerParams(dimension_semantics=semantics,
                                vmem_limit_bytes=_vmem_limit(vmem_estimate))


def _row_tile(rows, want):
    tile = min(rows, want)
    assert rows % tile == 0, (rows, tile)
    return tile


def _dot_nt(a, b):
    return lax.dot_general(a, b, (((1,), (1,)), ((), ())), preferred_element_type=F32)


def _rmsnorm(x, g):
    return x * lax.rsqrt(jnp.mean(x * x, axis=-1, keepdims=True) + NORM_EPS) * g


def _pad_meta_rows(a):
    return jnp.concatenate([a, jnp.zeros((V7X_LANES - N_META, a.shape[1]), a.dtype)], axis=0)


def _in_proj_kernel(x_ref, g_ref, w_ref, wkr_ref, ng_ref, o_ref, kr_ref, hn_ref, *, tn):
    j = pl.program_id(1)

    @pl.when(j == 0)
    def _():
        hn = _rmsnorm(x_ref[...], g_ref[...]).astype(BF16)
        hn_ref[...] = hn
        kr_ref[...] = jnp.dot(hn, wkr_ref[...], preferred_element_type=F32)

    acc = jnp.dot(hn_ref[...], w_ref[...], preferred_element_type=F32)
    col0 = j * tn
    plain_end = 3 * NA_WIDTH
    norm_end = plain_end + 2 * MLA_RANK

    @pl.when(col0 < plain_end)
    def _():
        scale = jnp.where(col0 < NA_WIDTH, NA_HEAD_DIM ** -0.5, 1.0).astype(F32)
        o_ref[...] = (acc * scale).astype(BF16)

    @pl.when(jnp.logical_and(col0 >= plain_end, col0 < norm_end))
    def _():
        for c in range(tn // MLA_RANK):
            sl = slice(c * MLA_RANK, (c + 1) * MLA_RANK)
            o_ref[:, sl] = _rmsnorm(acc[:, sl], ng_ref[:, sl]).astype(BF16)

    @pl.when(col0 >= norm_end)
    def _():
        o_ref[...] = jax.nn.sigmoid(acc).astype(BF16)


def _in_proj(x2d, g1, w_main, w_kr, norm_g, *, tm):
    r, d = x2d.shape
    n = w_main.shape[1]
    tn = 1024 if d % 1024 == 0 else 512
    norm_blk0 = (3 * NA_WIDTH) // tn
    norm_nblk = (2 * MLA_RANK) // tn
    est = 2 * (tm * d * 4 + d * tn * 2 + tm * tn * 2 + tm * 128 * 4 + d * 128 * 2) + tm * d * 2 + tm * tn * 4
    return pl.pallas_call(
        functools.partial(_in_proj_kernel, tn=tn),
        grid=(r // tm, n // tn),
        in_specs=[
            pl.BlockSpec((tm, d), lambda i, j: (i, 0)),
            pl.BlockSpec((1, d), lambda i, j: (0, 0)),
            pl.BlockSpec((d, tn), lambda i, j: (0, j)),
            pl.BlockSpec((d, 128), lambda i, j: (0, 0)),
            pl.BlockSpec((1, tn), lambda i, j: (0, jnp.clip(j - norm_blk0, 0, norm_nblk - 1))),
        ],
        out_specs=[
            pl.BlockSpec((tm, tn), lambda i, j: (i, j)),
            pl.BlockSpec((tm, 128), lambda i, j: (i, 0)),
        ],
        out_shape=[jax.ShapeDtypeStruct((r, n), BF16), jax.ShapeDtypeStruct((r, 128), F32)],
        scratch_shapes=[pltpu.VMEM((tm, d), BF16)],
        compiler_params=_params(("parallel", "arbitrary"), est),
        name="in_proj",
    )(x2d, g1, w_main, w_kr, norm_g)


def _rope_half(t, cos, sin):
    return t * cos + pltpu.roll(t, 64, axis=1) * sin


def _mla_up_kernel(cq_ref, ckv_ref, kr_ref, cos_ref, sin_ref, wq_ref, wkv_ref, q_ref, k_ref, v_ref):
    cos = cos_ref[...]
    sin = sin_ref[...]
    scale = (MLA_NOPE + MLA_ROPE) ** -0.5
    yq = jnp.dot(cq_ref[...], wq_ref[0], preferred_element_type=F32)
    q_ref[0, :, 0:MLA_NOPE] = (yq[:, 0:MLA_NOPE] * scale).astype(BF16)
    q_ref[0, :, MLA_NOPE:] = (_rope_half(yq[:, MLA_NOPE:], cos, sin) * scale).astype(BF16)
    ykv = jnp.dot(ckv_ref[...], wkv_ref[0], preferred_element_type=F32)
    k_ref[0, :, 0:MLA_NOPE] = ykv[:, 0:MLA_NOPE].astype(BF16)
    k_ref[0, :, MLA_NOPE:] = _rope_half(kr_ref[...], cos, sin).astype(BF16)
    v_ref[0] = ykv[:, MLA_NOPE:].astype(BF16)


def _mla_up(proj, kr, cos_t, sin_t, wq_h, wkv_h, *, tm):
    r = proj.shape[0]
    cq_blk = (3 * NA_WIDTH) // MLA_RANK
    n_pos_blk = cos_t.shape[0] // tm
    est = 2 * (2 * tm * MLA_RANK * 2 + 3 * tm * 128 * 4 + 2 * MLA_RANK * 256 * 2
               + 2 * tm * 256 * 2 + tm * 128 * 2) + 4 * tm * 256 * 4
    return pl.pallas_call(
        _mla_up_kernel,
        grid=(r // tm, MLA_HEADS),
        in_specs=[
            pl.BlockSpec((tm, MLA_RANK), lambda i, h: (i, cq_blk)),
            pl.BlockSpec((tm, MLA_RANK), lambda i, h: (i, cq_blk + 1)),
            pl.BlockSpec((tm, 128), lambda i, h: (i, 0)),
            pl.BlockSpec((tm, 128), lambda i, h: (i % n_pos_blk, 0)),
            pl.BlockSpec((tm, 128), lambda i, h: (i % n_pos_blk, 0)),
            pl.BlockSpec((1, MLA_RANK, 256), lambda i, h: (h, 0, 0)),
            pl.BlockSpec((1, MLA_RANK, 256), lambda i, h: (h, 0, 0)),
        ],
        out_specs=[
            pl.BlockSpec((1, tm, MLA_QK_PAD), lambda i, h: (h, i, 0)),
            pl.BlockSpec((1, tm, MLA_QK_PAD), lambda i, h: (h, i, 0)),
            pl.BlockSpec((1, tm, MLA_V), lambda i, h: (h, i, 0)),
        ],
        out_shape=[
            jax.ShapeDtypeStruct((MLA_HEADS, r, MLA_QK_PAD), BF16),
            jax.ShapeDtypeStruct((MLA_HEADS, r, MLA_QK_PAD), BF16),
            jax.ShapeDtypeStruct((MLA_HEADS, r, MLA_V), BF16),
        ],
        compiler_params=_params(("parallel", "arbitrary"), est),
        name="mla_up",
    )(proj, proj, kr, cos_t, sin_t, wq_h, wkv_h)


def _mla_attn_kernel(q_ref, k_ref, v_ref, qm_ref, km_ref, vm_ref, o_ref, om_ref, *, tq):
    k = k_ref[0]
    v = v_ref[0]
    km = _pad_meta_rows(km_ref[0])
    vm = _pad_meta_rows(vm_ref[0])
    meta_lane = lax.broadcasted_iota(jnp.int32, (1, V7X_LANES), 1) < N_META

    def attend(q):
        s = _dot_nt(q, k)
        sm = jnp.where(meta_lane, _dot_nt(q, km), NEG_INF)
        m = jnp.maximum(jnp.max(s, axis=-1, keepdims=True), jnp.max(sm, axis=-1, keepdims=True))
        p = jnp.exp(s - m)
        pm = jnp.exp(sm - m)
        l = jnp.sum(p, axis=-1, keepdims=True) + jnp.sum(pm, axis=-1, keepdims=True)
        o = (jnp.dot(p.astype(BF16), v, preferred_element_type=F32)
             + jnp.dot(pm.astype(BF16), vm, preferred_element_type=F32))
        return o / l

    s_len = q_ref.shape[1]
    for c in range(s_len // tq):
        rows = slice(c * tq, (c + 1) * tq)
        o_ref[rows, :] = attend(q_ref[0, rows, :]).astype(BF16)
    om_ref[...] = attend(qm_ref[0]).astype(BF16)


def _mla_attn(q, k, v, qm, km, vm, *, batch):
    r = q.shape[1]
    s_len = r // batch
    tq = min(512, s_len)
    est = 2 * (2 * s_len * 256 * 2 + 2 * s_len * 128 * 2) + 3 * tq * s_len * 4 + tq * s_len * 2
    return pl.pallas_call(
        functools.partial(_mla_attn_kernel, tq=tq),
        grid=(batch, MLA_HEADS),
        in_specs=[
            pl.BlockSpec((1, s_len, MLA_QK_PAD), lambda b, h: (h, b, 0)),
            pl.BlockSpec((1, s_len, MLA_QK_PAD), lambda b, h: (h, b, 0)),
            pl.BlockSpec((1, s_len, MLA_V), lambda b, h: (h, b, 0)),
            pl.BlockSpec((1, N_META, MLA_QK_PAD), lambda b, h: (h, b, 0)),
            pl.BlockSpec((1, N_META, MLA_QK_PAD), lambda b, h: (h, b, 0)),
            pl.BlockSpec((1, N_META, MLA_V), lambda b, h: (h, b, 0)),
        ],
        out_specs=[
            pl.BlockSpec((s_len, MLA_V), lambda b, h: (b, h)),
            pl.BlockSpec((N_META, MLA_V), lambda b, h: (b, h)),
        ],
        out_shape=[
            jax.ShapeDtypeStruct((r, MLA_HEADS * MLA_V), BF16),
            jax.ShapeDtypeStruct((batch * N_META, MLA_HEADS * MLA_V), BF16),
        ],
        compiler_params=_params(("parallel", "parallel"), est),
        name="mla_attn",
    )(q, k, v, qm, km, vm)


def _na_bias_tables(rel_bias, rows):
    nqb = rows // NA_QROWS
    wh = NA_WIN_H
    qc = np.arange(GRID_W)
    kc = np.arange(GRID_W)
    cstart = np.clip(qc - NA_WIN_W // 2, 0, GRID_W - NA_WIN_W)
    col_ok = (kc[None, :] >= cstart[:, None]) & (kc[None, :] < cstart[:, None] + NA_WIN_W)
    col_off = np.clip(kc[None, :] - qc[:, None], -(NA_WIN_W - 1), NA_WIN_W - 1) + NA_WIN_W - 1
    n_co = 2 * NA_WIN_W - 1
    onehot = np.zeros((n_co, GRID_W * GRID_W), np.float32)
    onehot[col_off.reshape(-1), np.arange(GRID_W * GRID_W)] = 1.0
    h = rel_bias.shape[0]
    tiles = jnp.dot(rel_bias.astype(F32).reshape(h * (2 * wh - 1), n_co), jnp.asarray(onehot),
                    precision=lax.Precision.HIGHEST).reshape(h, 2 * wh - 1, GRID_W, GRID_W)
    ro_idx = np.zeros((3, NA_QROWS, NA_KROWS), np.int32)
    ok = np.zeros((3, NA_QROWS, GRID_W, NA_KROWS, GRID_W), bool)
    for v, qb in enumerate((0, 1, nqb - 1)):
        kstart = int(np.clip(NA_QROWS * qb - wh // 2, 0, rows - NA_KROWS))
        for i in range(NA_QROWS):
            r = NA_QROWS * qb + i
            rs = int(np.clip(r - wh // 2, 0, rows - wh))
            for j in range(NA_KROWS):
                kr = kstart + j
                in_row = rs <= kr < rs + wh
                ro_idx[v, i, j] = int(np.clip(kr - r + wh - 1, 0, 2 * wh - 2))
                ok[v, i, :, j, :] = col_ok if in_row else False
    t = jnp.take(tiles, jnp.asarray(ro_idx.reshape(-1)), axis=1)
    t = t.reshape(h, 3, NA_QROWS, NA_KROWS, GRID_W, GRID_W).transpose(0, 1, 2, 4, 3, 5)
    t = jnp.where(jnp.asarray(ok)[None], t, NEG_INF)
    return t.reshape(h, 3, NA_QROWS * GRID_W, NA_KROWS * GRID_W)


def _na_attn_kernel(q_ref, k_ref, v_ref, qm_ref, km_ref, vm_ref, bias_ref, mb_ref, o_ref, om_ref, *, rows):
    nqb = rows // NA_QROWS
    qblk = NA_QROWS * GRID_W
    kblk = NA_KROWS * GRID_W
    km = _pad_meta_rows(km_ref[...])
    vm = _pad_meta_rows(vm_ref[...])
    lane = lax.broadcasted_iota(jnp.int32, (1, 2 * NA_HEAD_DIM), 1)
    head_lanes = [lane < NA_HEAD_DIM, lane >= NA_HEAD_DIM]

    def softmax_pv(s, sm, v_loc):
        m = jnp.maximum(jnp.max(s, axis=-1, keepdims=True), jnp.max(sm, axis=-1, keepdims=True))
        p = jnp.exp(s - m)
        pm = jnp.exp(sm - m)
        l = jnp.sum(p, axis=-1, keepdims=True) + jnp.sum(pm, axis=-1, keepdims=True)
        o = (jnp.dot(p.astype(BF16), v_loc, preferred_element_type=F32)
             + jnp.dot(pm.astype(BF16), vm, preferred_element_type=F32))
        return o / l

    def block(qb, carry):
        q0 = pl.multiple_of(qb * qblk, qblk)
        k0 = pl.multiple_of(jnp.clip(NA_QROWS * qb - NA_WIN_H // 2, 0, rows - NA_KROWS) * GRID_W, GRID_W)
        variant = jnp.where(qb == 0, 0, jnp.where(qb == nqb - 1, 2, 1))
        q = q_ref[pl.ds(q0, qblk), :]
        k_loc = k_ref[pl.ds(k0, kblk), :]
        v_loc = v_ref[pl.ds(k0, kblk), :]
        outs = []
        for hh in range(2):
            qh = jnp.where(head_lanes[hh], q, jnp.zeros_like(q))
            s = _dot_nt(qh, k_loc) + bias_ref[0, hh, variant]
            sm = _dot_nt(qh, km) + mb_ref[0, hh]
            outs.append(softmax_pv(s, sm, v_loc))
        o_ref[pl.ds(q0, qblk), :] = jnp.where(head_lanes[0], outs[0], outs[1]).astype(BF16)
        return carry

    lax.fori_loop(0, nqb, block, 0)

    qm = qm_ref[...]
    outs = []
    for hh in range(2):
        qh = jnp.where(head_lanes[hh], qm, jnp.zeros_like(qm))
        sm = _dot_nt(qh, km) + mb_ref[0, hh]
        m = jnp.max(sm, axis=-1, keepdims=True)
        pm = jnp.exp(sm - m)
        l = jnp.sum(pm, axis=-1, keepdims=True)
        outs.append(jnp.dot(pm.astype(BF16), vm, preferred_element_type=F32) / l)
    om_ref[...] = jnp.where(head_lanes[0], outs[0], outs[1]).astype(BF16)


def _na_attn(proj, proj_m, bias_tab, meta_bias, *, batch):
    r = proj.shape[0]
    s_len = r // batch
    rows = s_len // GRID_W
    assert rows % NA_QROWS == 0 and rows >= NA_KROWS, rows
    npair = NA_HEADS // 2
    qblk = NA_QROWS * GRID_W
    kblk = NA_KROWS * GRID_W
    bias5 = bias_tab.reshape(npair, 2, 3, qblk, kblk)
    mb4 = jnp.concatenate([meta_bias.astype(F32), jnp.full((NA_HEADS, V7X_LANES - N_META), NEG_INF, F32)],
                          axis=1).reshape(npair, 2, 1, V7X_LANES)
    est = 2 * (3 * s_len * 128 * 2 + 6 * qblk * kblk * 4 + s_len * 128 * 2) + 6 * qblk * kblk * 4
    return pl.pallas_call(
        functools.partial(_na_attn_kernel, rows=rows),
        grid=(npair, batch),
        in_specs=[
            pl.BlockSpec((s_len, 128), lambda p, b: (b, p)),
            pl.BlockSpec((s_len, 128), lambda p, b: (b, npair + p)),
            pl.BlockSpec((s_len, 128), lambda p, b: (b, 2 * npair + p)),
            pl.BlockSpec((N_META, 128), lambda p, b: (b, p)),
            pl.BlockSpec((N_META, 128), lambda p, b: (b, npair + p)),
            pl.BlockSpec((N_META, 128), lambda p, b: (b, 2 * npair + p)),
            pl.BlockSpec((1, 2, 3, qblk, kblk), lambda p, b: (p, 0, 0, 0, 0)),
            pl.BlockSpec((1, 2, 1, V7X_LANES), lambda p, b: (p, 0, 0, 0)),
        ],
        out_specs=[
            pl.BlockSpec((s_len, 128), lambda p, b: (b, p)),
            pl.BlockSpec((N_META, 128), lambda p, b: (b, p)),
        ],
        out_shape=[
            jax.ShapeDtypeStruct((r, NA_WIDTH), BF16),
            jax.ShapeDtypeStruct((batch * N_META, NA_WIDTH), BF16),
        ],
        compiler_params=_params(("parallel", "parallel"), est),
        name="na_attn",
    )(proj, proj, proj, proj_m, proj_m, proj_m, bias5, mb4)


def _merge_kernel(ona_ref, omla_ref, wna_ref, wmla_ref, ga_ref, gb_ref, o_ref):
    a = jnp.dot(ona_ref[...], wna_ref[...], preferred_element_type=F32)
    b = jnp.dot(omla_ref[...], wmla_ref[...], preferred_element_type=F32)
    o_ref[...] = (ga_ref[...].astype(F32) * a + gb_ref[...].astype(F32) * b).astype(BF16)


def _merge(o_na, o_mla, w_na, w_mla, proj, *, tm):
    r = o_na.shape[0]
    d = w_na.shape[1]
    tn = 1024 if d % 1024 == 0 else 512
    ga_blk = (3 * NA_WIDTH + 2 * MLA_RANK) // tn
    gb_blk = ga_blk + d // tn
    est = 2 * (tm * (NA_WIDTH + MLA_HEADS * MLA_V) * 2 + (NA_WIDTH + MLA_HEADS * MLA_V) * tn * 2
               + 3 * tm * tn * 2) + 3 * tm * tn * 4
    return pl.pallas_call(
        _merge_kernel,
        grid=(r // tm, d // tn),
        in_specs=[
            pl.BlockSpec((tm, NA_WIDTH), lambda i, j: (i, 0)),
            pl.BlockSpec((tm, MLA_HEADS * MLA_V), lambda i, j: (i, 0)),
            pl.BlockSpec((NA_WIDTH, tn), lambda i, j: (0, j)),
            pl.BlockSpec((MLA_HEADS * MLA_V, tn), lambda i, j: (0, j)),
            pl.BlockSpec((tm, tn), lambda i, j: (i, ga_blk + j)),
            pl.BlockSpec((tm, tn), lambda i, j: (i, gb_blk + j)),
        ],
        out_specs=pl.BlockSpec((tm, tn), lambda i, j: (i, j)),
        out_shape=jax.ShapeDtypeStruct((r, d), BF16),
        compiler_params=_params(("parallel", "arbitrary"), est),
        name="merge",
    )(o_na, o_mla, w_na, w_mla, proj, proj)


def _out_proj_kernel(m_ref, w_ref, x_ref, o_ref):
    o_ref[...] = x_ref[...] + jnp.dot(m_ref[...], w_ref[...], preferred_element_type=F32)


def _out_proj(merged, w_out, x2d, *, tm):
    r, d = x2d.shape
    tn = 1024 if d % 1024 == 0 else 512
    est = 2 * (tm * d * 2 + d * tn * 2 + 2 * tm * tn * 4) + tm * tn * 4
    return pl.pallas_call(
        _out_proj_kernel,
        grid=(r // tm, d // tn),
        in_specs=[
            pl.BlockSpec((tm, d), lambda i, j: (i, 0)),
            pl.BlockSpec((d, tn), lambda i, j: (0, j)),
            pl.BlockSpec((tm, tn), lambda i, j: (i, j)),
        ],
        out_specs=pl.BlockSpec((tm, tn), lambda i, j: (i, j)),
        out_shape=jax.ShapeDtypeStruct((r, d), F32),
        compiler_params=_params(("parallel", "arbitrary"), est),
        name="out_proj",
    )(merged, w_out, x2d)


def _peer_q_kernel(h_ref, g_ref, w_ref, q_ref, hn_ref):
    @pl.when(pl.program_id(1) == 0)
    def _():
        hn_ref[...] = _rmsnorm(h_ref[...], g_ref[...]).astype(BF16)

    q_ref[...] = jnp.dot(hn_ref[...], w_ref[...], preferred_element_type=F32).astype(BF16)


def _peer_q(h2, g2, w_q, *, tm):
    r, d = h2.shape
    n = w_q.shape[1]
    tn = 1024
    est = 2 * (tm * d * 4 + d * tn * 2 + tm * tn * 2 + tm * d * 2) + tm * tn * 4
    return pl.pallas_call(
        _peer_q_kernel,
        grid=(r // tm, n // tn),
        in_specs=[
            pl.BlockSpec((tm, d), lambda i, j: (i, 0)),
            pl.BlockSpec((1, d), lambda i, j: (0, 0)),
            pl.BlockSpec((d, tn), lambda i, j: (0, j)),
        ],
        out_specs=[
            pl.BlockSpec((tm, tn), lambda i, j: (i, j)),
            pl.BlockSpec((tm, d), lambda i, j: (i, 0)),
        ],
        out_shape=[jax.ShapeDtypeStruct((r, n), BF16), jax.ShapeDtypeStruct((r, d), BF16)],
        compiler_params=_params(("parallel", "arbitrary"), est),
        name="peer_q",
    )(h2, g2, w_q)


def _extract_topk(x, k):
    n, t = x.shape
    row = lax.broadcasted_iota(jnp.int32, (n, t), 0).astype(F32)
    krow = lax.broadcasted_iota(jnp.int32, (k, t), 0)

    def body(i, carry):
        x, rank, vals = carry
        m = jnp.max(x, axis=0, keepdims=True)
        first = jnp.min(jnp.where(x == m, row, float(n)), axis=0, keepdims=True)
        hit = row == first
        rank = jnp.where(hit, i.astype(F32), rank)
        x = jnp.where(hit, _F32_MIN, x)
        vals = jnp.where(krow == i, m, vals)
        return x, rank, vals

    _, rank, vals = lax.fori_loop(
        0, k, body, (x, jnp.full((n, t), float(k), F32), jnp.zeros((k, t), F32)))
    return vals, rank


def _peer_route_kernel(q_ref, keys_ref, x1_ref, e1_ref, x2_ref, e2_ref):
    k = PEER_TOPK
    q = q_ref[...]
    s1 = _dot_nt(keys_ref[0, 0], q[:, 0:PEER_DK // 2])
    s2 = _dot_nt(keys_ref[0, 1], q[:, PEER_DK // 2:])
    top1, rank1 = _extract_topk(s1, k)
    top2, rank2 = _extract_topk(s2, k)

    tm = q.shape[0]
    pieces, spans = [], []
    off = 0
    for i in range(k):
        cnt = k // (i + 1)
        rows = -(-cnt // 8) * 8
        jrow = lax.broadcasted_iota(jnp.int32, (rows, tm), 0)
        pieces.append(jnp.where(jrow < cnt, top1[i:i + 1, :] + top2[0:rows, :], _F32_MIN))
        spans.append((off, rows))
        off += rows
    cand = jnp.concatenate(pieces, axis=0)
    ctop, crank = _extract_topk(cand, k)
    chosen = crank < float(k)
    cmax = ctop[0:1, :]
    z = jnp.sum(jnp.where(chosen, jnp.exp(cand - cmax), 0.0), axis=0, keepdims=True)

    x1 = jnp.zeros(s1.shape, F32)
    for i, (off, rows) in enumerate(spans):
        n_sel = jnp.sum(jnp.where(chosen[off:off + rows, :], 1.0, 0.0), axis=0, keepdims=True)
        x1 = jnp.where(rank1 == float(i), n_sel, x1)
    x1_ref[0] = x1
    x2_ref[0] = -rank2
    e1_ref[0] = jnp.exp(s1 - top1[0:1, :]) / z
    e2_ref[0] = jnp.exp(s2 - top2[0:1, :])


def _peer_route(q2, sub_keys, *, tm):
    r = q2.shape[0]
    tab = jax.ShapeDtypeStruct((PEER_HEADS, PEER_NKEYS, r), F32)
    tab_spec = pl.BlockSpec((1, PEER_NKEYS, tm), lambda i, h: (h, 0, i))
    est = 2 * (tm * PEER_DK * 2 + 2 * 128 * 128 * 2 + 4 * PEER_NKEYS * tm * 4) + 16 * PEER_NKEYS * tm * 4
    return pl.pallas_call(
        _peer_route_kernel,
        grid=(r // tm, PEER_HEADS),
        in_specs=[
            pl.BlockSpec((tm, PEER_DK), lambda i, h: (i, h)),
            pl.BlockSpec((1, 2, PEER_NKEYS, PEER_DK // 2), lambda i, h: (h, 0, 0, 0)),
        ],
        out_specs=[tab_spec, tab_spec, tab_spec, tab_spec],
        out_shape=[tab, tab, tab, tab],
        compiler_params=_params(("parallel", "parallel"), est),
        name="peer_route",
    )(q2, sub_keys)


PEER_GATE_THRESHOLD = 1.0


def _peer_dense_kernel(hn_ref, u_ref, vt_ref, x1_ref, e1_ref, x2_ref, e2_ref, h_ref, g_ref, o_ref, acc_ref,
                       *, te, final_norm):
    j = pl.program_id(1)

    @pl.when(j == 0)
    def _():
        acc_ref[...] = jnp.zeros_like(acc_ref)

    act = jax.nn.gelu(_dot_nt(u_ref[...], hn_ref[...]))
    blocks = []
    for ai in range(te // PEER_NKEYS):
        a = j * (te // PEER_NKEYS) + ai
        gate = None
        for h in range(PEER_HEADS):
            sel = (x1_ref[h, pl.ds(a, 1), :] + x2_ref[h]) >= PEER_GATE_THRESHOLD
            term = jnp.where(sel, e1_ref[h, pl.ds(a, 1), :] * e2_ref[h], 0.0)
            gate = term if gate is None else gate + term
        blocks.append((gate * act[ai * PEER_NKEYS:(ai + 1) * PEER_NKEYS, :]).astype(BF16))
    p = jnp.concatenate(blocks, axis=0)
    acc_ref[...] += jnp.dot(vt_ref[...], p, preferred_element_type=F32)

    @pl.when(j == pl.num_programs(1) - 1)
    def _():
        out = h_ref[...] + acc_ref[...].T
        o_ref[...] = _rmsnorm(out, g_ref[...]) if final_norm else out


def _peer_dense(hn2, u, vt, tabs, h2, gf, *, tm, te, final_norm):
    r, d = hn2.shape
    e = u.shape[0]
    x1, e1, x2, e2 = tabs
    tab_spec = pl.BlockSpec((PEER_HEADS, PEER_NKEYS, tm), lambda i, j: (0, 0, i))
    est = (2 * (tm * d * 2 + 2 * te * d * 2 + 4 * PEER_HEADS * PEER_NKEYS * tm * 4 + tm * d * 4)
           + tm * d * 4 + d * tm * 4 + 6 * te * tm * 4)
    return pl.pallas_call(
        functools.partial(_peer_dense_kernel, te=te, final_norm=final_norm),
        grid=(r // tm, e // te),
        in_specs=[
            pl.BlockSpec((tm, d), lambda i, j: (i, 0)),
            pl.BlockSpec((te, d), lambda i, j: (j, 0)),
            pl.BlockSpec((d, te), lambda i, j: (0, j)),
            tab_spec, tab_spec, tab_spec, tab_spec,
            pl.BlockSpec((tm, d), lambda i, j: (i, 0), pipeline_mode=pl.Buffered(1)),
            pl.BlockSpec((1, d), lambda i, j: (0, 0)),
        ],
        out_specs=pl.BlockSpec((tm, d), lambda i, j: (i, 0)),
        out_shape=jax.ShapeDtypeStruct((r, d), F32),
        scratch_shapes=[pltpu.VMEM((d, tm), F32)],
        compiler_params=_params(("parallel", "arbitrary"), est),
        name="peer_dense",
    )(hn2, u, vt, x1, e1, x2, e2, h2, gf)


def _rope_tables(pos):
    inv_freq = ROPE_THETA ** (-jnp.arange(0, MLA_ROPE, 2, dtype=F32) / MLA_ROPE)
    ang = pos.astype(F32)[:, None] * inv_freq[None, :]
    cos, sin = jnp.cos(ang), jnp.sin(ang)
    zeros = jnp.zeros((pos.shape[0], V7X_LANES - MLA_ROPE), F32)
    return (jnp.concatenate([cos, cos, zeros], axis=1), jnp.concatenate([-sin, sin, zeros], axis=1))


def _swap_halves(w):
    half = w.shape[-1] // 2
    return jnp.concatenate([w[..., half:], w[..., :half]], axis=-1)


def _layer(x_real, x_meta, p, l, *, batch):
    d = x_real.shape[1]
    s_len = x_real.shape[0] // batch
    tm_real = _row_tile(x_real.shape[0], 1024)
    tm_meta = x_meta.shape[0]

    w_in = p["w_in"][l]
    kr0 = 3 * NA_WIDTH + 2 * MLA_RANK
    w_main = jnp.concatenate([w_in[:, :kr0], w_in[:, kr0 + MLA_ROPE:]], axis=1).astype(BF16)
    w_kr = w_in[:, kr0:kr0 + MLA_ROPE]
    w_kr = jnp.concatenate([w_kr, _swap_halves(w_kr)], axis=1).astype(BF16)
    norm_g = jnp.concatenate([p["mla_q_norm_g"][l], p["mla_kv_norm_g"][l]])[None].astype(F32)
    g1 = p["norm1_g"][l][None].astype(F32)
    wq = p["mla_w_uq"][l].reshape(MLA_RANK, MLA_HEADS, MLA_NOPE + MLA_ROPE)
    wq = jnp.concatenate([wq, _swap_halves(wq[..., MLA_NOPE:])], axis=-1)
    wq = wq.transpose(1, 0, 2).astype(BF16)
    wkv = p["mla_w_ukv"][l].reshape(MLA_RANK, MLA_HEADS, MLA_NOPE + MLA_V).transpose(1, 0, 2).astype(BF16)
    cos_r, sin_r = _rope_tables(N_META + jnp.arange(s_len))
    cos_m, sin_m = _rope_tables(jnp.tile(jnp.arange(N_META), batch))
    bias_tab = _na_bias_tables(p["na_rel_bias"][l], s_len // GRID_W)

    proj_r, kr_r = _in_proj(x_real, g1, w_main, w_kr, norm_g, tm=tm_real)
    proj_m, kr_m = _in_proj(x_meta, g1, w_main, w_kr, norm_g, tm=tm_meta)
    q_r, k_r, v_r = _mla_up(proj_r, kr_r, cos_r, sin_r, wq, wkv, tm=_row_tile(s_len, 1024))
    q_m, k_m, v_m = _mla_up(proj_m, kr_m, cos_m, sin_m, wq, wkv, tm=tm_meta)
    omla_r, omla_m = _mla_attn(q_r, k_r, v_r, q_m, k_m, v_m, batch=batch)
    ona_r, ona_m = _na_attn(proj_r, proj_m, bias_tab, p["na_meta_bias"][l], batch=batch)

    w_na = p["w_na_branch"][l].astype(BF16)
    w_mla = p["w_mla_branch"][l].astype(BF16)
    w_out = p["w_out"][l].astype(BF16)
    h_real = _out_proj(_merge(ona_r, omla_r, w_na, w_mla, proj_r, tm=tm_real), w_out, x_real, tm=tm_real)
    h_meta = _out_proj(_merge(ona_m, omla_m, w_na, w_mla, proj_m, tm=tm_meta), w_out, x_meta, tm=tm_meta)
    return h_real, h_meta


def _peer(h2, p, l, gf, *, tm, te, final_norm):
    g2 = p["norm2_g"][l][None].astype(F32)
    w_q = p["peer_w_q"][l].astype(BF16)
    keys = p["peer_sub_keys"][l].astype(BF16)
    u = p["peer_u"][l].astype(BF16)
    vt = p["peer_v"][l].astype(BF16).T
    q2, hn2 = _peer_q(h2, g2, w_q, tm=_row_tile(h2.shape[0], 1024))
    tabs = _peer_route(q2, keys, tm=min(256, h2.shape[0]))
    return _peer_dense(hn2, u, vt, tabs, h2, gf, tm=tm, te=te, final_norm=final_norm)


def kernel(x, meta_tokens, norm1_g, w_in, na_rel_bias, na_meta_bias, mla_q_norm_g, mla_w_uq, mla_kv_norm_g,
           mla_w_ukv, w_na_branch, w_mla_branch, w_out, norm2_g, peer_w_q, peer_sub_keys, peer_u, peer_v,
           final_norm_g):
    b, s, d = x.shape
    depth = w_in.shape[0]
    p = dict(norm1_g=norm1_g, w_in=w_in, na_rel_bias=na_rel_bias, na_meta_bias=na_meta_bias,
             mla_q_norm_g=mla_q_norm_g, mla_w_uq=mla_w_uq, mla_kv_norm_g=mla_kv_norm_g, mla_w_ukv=mla_w_ukv,
             w_na_branch=w_na_branch, w_mla_branch=w_mla_branch, w_out=w_out, norm2_g=norm2_g,
             peer_w_q=peer_w_q, peer_sub_keys=peer_sub_keys, peer_u=peer_u, peer_v=peer_v)
    x_real = x.reshape(b * s, d)
    x_meta = jnp.broadcast_to(meta_tokens.astype(x.dtype)[None], (b, N_META, d)).reshape(b * N_META, d)
    gf = final_norm_g[None].astype(F32)
    tm = _row_tile(b * s, 512)
    te = 512
    for l in range(depth):
        h_real, h_meta = _layer(x_real, x_meta, p, l, batch=b)
        x_real = _peer(h_real, p, l, gf, tm=tm, te=te, final_norm=l == depth - 1)
        x_meta = _peer(h_meta, p, l, gf, tm=h_meta.shape[0], te=te, final_norm=False)
    return x_real.reshape(b, s, d)
```

```python
import functools

import jax
import jax.numpy as jnp
import numpy as np
from jax import lax
from jax.experimental import pallas as pl
from jax.experimental.pallas import tpu as pltpu

GRID_W = 64
N_META = 16
NA_HEADS = 16
NA_HEAD_DIM = 64
NA_WIN_H = 8
NA_WIN_W = 16
NA_WIDTH = NA_HEADS * NA_HEAD_DIM
MLA_HEADS = 16
MLA_RANK = 512
MLA_NOPE = 128
MLA_ROPE = 64
MLA_V = 128
MLA_QK_PAD = 256
ROPE_THETA = 10000.0
PEER_HEADS = 8
PEER_NKEYS = 128
PEER_DK = 256
PEER_TOPK = 16
NORM_EPS = 1e-6
NEG_INF = -1e30

V7X_VMEM_BYTES = 64 * 1024 * 1024
V7X_VMEM_HEADROOM = 6 * 1024 * 1024
V7X_LANES = 128

NA_QROWS = 4
NA_KROWS = NA_QROWS + NA_WIN_H

F32 = jnp.float32
BF16 = jnp.bfloat16
_F32_MIN = float(np.finfo(np.float32).min)


def _vmem_limit(estimate_bytes):
    want = int(estimate_bytes * 1.25) + 4 * 1024 * 1024
    return min(max(want, 16 * 1024 * 1024), V7X_VMEM_BYTES - V7X_VMEM_HEADROOM)


def _params(semantics, vmem_estimate, flags=None):
    return pltpu.CompilerParams(dimension_semantics=semantics,
                                vmem_limit_bytes=_vmem_limit(vmem_estimate), flags=flags)


def _row_tile(rows, want):
    tile = min(rows, want)
    assert rows % tile == 0, (rows, tile)
    return tile


def _dot_nt(a, b):
    return lax.dot_general(a, b, (((1,), (1,)), ((), ())), preferred_element_type=F32)


def _rmsnorm(x, g):
    return x * lax.rsqrt(jnp.mean(x * x, axis=-1, keepdims=True) + NORM_EPS) * g


def _pad_meta_rows(a):
    return jnp.concatenate([a, jnp.zeros((V7X_LANES - N_META, a.shape[1]), a.dtype)], axis=0)


def _in_proj_kernel(x_ref, g_ref, w_ref, wkr_ref, ng_ref, o_ref, kr_ref, hn_ref, *, tn):
    j = pl.program_id(1)

    @pl.when(j == 0)
    def _():
        hn = _rmsnorm(x_ref[...], g_ref[...]).astype(BF16)
        hn_ref[...] = hn
        kr_ref[...] = jnp.dot(hn, wkr_ref[...], preferred_element_type=F32)

    acc = jnp.dot(hn_ref[...], w_ref[...], preferred_element_type=F32)
    col0 = j * tn
    plain_end = 3 * NA_WIDTH
    norm_end = plain_end + 2 * MLA_RANK

    @pl.when(col0 < plain_end)
    def _():
        scale = jnp.where(col0 < NA_WIDTH, NA_HEAD_DIM ** -0.5, 1.0).astype(F32)
        o_ref[...] = (acc * scale).astype(BF16)

    @pl.when(jnp.logical_and(col0 >= plain_end, col0 < norm_end))
    def _():
        for c in range(tn // MLA_RANK):
            sl = slice(c * MLA_RANK, (c + 1) * MLA_RANK)
            o_ref[:, sl] = _rmsnorm(acc[:, sl], ng_ref[:, sl]).astype(BF16)

    @pl.when(col0 >= norm_end)
    def _():
        o_ref[...] = jax.nn.sigmoid(acc).astype(BF16)


def _in_proj(x2d, g1, w_main, w_kr, norm_g, *, tm):
    r, d = x2d.shape
    n = w_main.shape[1]
    tn = 1024 if d % 1024 == 0 else 512
    norm_blk0 = (3 * NA_WIDTH) // tn
    norm_nblk = (2 * MLA_RANK) // tn
    est = 2 * (tm * d * 4 + d * tn * 2 + tm * tn * 2 + tm * 128 * 4 + d * 128 * 2) + tm * d * 2 + tm * tn * 4
    return pl.pallas_call(
        functools.partial(_in_proj_kernel, tn=tn),
        grid=(r // tm, n // tn),
        in_specs=[
            pl.BlockSpec((tm, d), lambda i, j: (i, 0)),
            pl.BlockSpec((1, d), lambda i, j: (0, 0)),
            pl.BlockSpec((d, tn), lambda i, j: (0, j)),
            pl.BlockSpec((d, 128), lambda i, j: (0, 0)),
            pl.BlockSpec((1, tn), lambda i, j: (0, jnp.clip(j - norm_blk0, 0, norm_nblk - 1))),
        ],
        out_specs=[
            pl.BlockSpec((tm, tn), lambda i, j: (i, j)),
            pl.BlockSpec((tm, 128), lambda i, j: (i, 0)),
        ],
        out_shape=[jax.ShapeDtypeStruct((r, n), BF16), jax.ShapeDtypeStruct((r, 128), F32)],
        scratch_shapes=[pltpu.VMEM((tm, d), BF16)],
        compiler_params=_params(("parallel", "arbitrary"), est),
        name="in_proj",
    )(x2d, g1, w_main, w_kr, norm_g)


def _rope_half(t, cos, sin):
    return t * cos + pltpu.roll(t, 64, axis=1) * sin


def _mla_up_kernel(cq_ref, ckv_ref, kr_ref, cos_ref, sin_ref, wq_ref, wkv_ref, q_ref, k_ref, v_ref):
    cos = cos_ref[...]
    sin = sin_ref[...]
    scale = (MLA_NOPE + MLA_ROPE) ** -0.5
    yq = jnp.dot(cq_ref[...], wq_ref[0], preferred_element_type=F32)
    q_ref[0, :, 0:MLA_NOPE] = (yq[:, 0:MLA_NOPE] * scale).astype(BF16)
    q_ref[0, :, MLA_NOPE:] = (_rope_half(yq[:, MLA_NOPE:], cos, sin) * scale).astype(BF16)
    ykv = jnp.dot(ckv_ref[...], wkv_ref[0], preferred_element_type=F32)
    k_ref[0, :, 0:MLA_NOPE] = ykv[:, 0:MLA_NOPE].astype(BF16)
    k_ref[0, :, MLA_NOPE:] = _rope_half(kr_ref[...], cos, sin).astype(BF16)
    v_ref[0] = ykv[:, MLA_NOPE:].astype(BF16)


def _mla_up(proj, kr, cos_t, sin_t, wq_h, wkv_h, *, tm):
    r = proj.shape[0]
    cq_blk = (3 * NA_WIDTH) // MLA_RANK
    n_pos_blk = cos_t.shape[0] // tm
    est = 2 * (2 * tm * MLA_RANK * 2 + 3 * tm * 128 * 4 + 2 * MLA_RANK * 256 * 2
               + 2 * tm * 256 * 2 + tm * 128 * 2) + 4 * tm * 256 * 4
    return pl.pallas_call(
        _mla_up_kernel,
        grid=(r // tm, MLA_HEADS),
        in_specs=[
            pl.BlockSpec((tm, MLA_RANK), lambda i, h: (i, cq_blk)),
            pl.BlockSpec((tm, MLA_RANK), lambda i, h: (i, cq_blk + 1)),
            pl.BlockSpec((tm, 128), lambda i, h: (i, 0)),
            pl.BlockSpec((tm, 128), lambda i, h: (i % n_pos_blk, 0)),
            pl.BlockSpec((tm, 128), lambda i, h: (i % n_pos_blk, 0)),
            pl.BlockSpec((1, MLA_RANK, 256), lambda i, h: (h, 0, 0)),
            pl.BlockSpec((1, MLA_RANK, 256), lambda i, h: (h, 0, 0)),
        ],
        out_specs=[
            pl.BlockSpec((1, tm, MLA_QK_PAD), lambda i, h: (h, i, 0)),
            pl.BlockSpec((1, tm, MLA_QK_PAD), lambda i, h: (h, i, 0)),
            pl.BlockSpec((1, tm, MLA_V), lambda i, h: (h, i, 0)),
        ],
        out_shape=[
            jax.ShapeDtypeStruct((MLA_HEADS, r, MLA_QK_PAD), BF16),
            jax.ShapeDtypeStruct((MLA_HEADS, r, MLA_QK_PAD), BF16),
            jax.ShapeDtypeStruct((MLA_HEADS, r, MLA_V), BF16),
        ],
        compiler_params=_params(("parallel", "arbitrary"), est),
        name="mla_up",
    )(proj, proj, kr, cos_t, sin_t, wq_h, wkv_h)


def _mla_attn_kernel(q_ref, k_ref, v_ref, qm_ref, km_ref, vm_ref, o_ref, om_ref, *, tq):
    k = k_ref[0]
    v = v_ref[0]
    km = _pad_meta_rows(km_ref[0])
    vm = _pad_meta_rows(vm_ref[0])
    meta_lane = lax.broadcasted_iota(jnp.int32, (1, V7X_LANES), 1) < N_META

    def attend(q):
        s = _dot_nt(q, k)
        sm = jnp.where(meta_lane, _dot_nt(q, km), NEG_INF)
        m = jnp.maximum(jnp.max(s, axis=-1, keepdims=True), jnp.max(sm, axis=-1, keepdims=True))
        p = jnp.exp(s - m)
        pm = jnp.exp(sm - m)
        l = jnp.sum(p, axis=-1, keepdims=True) + jnp.sum(pm, axis=-1, keepdims=True)
        o = (jnp.dot(p.astype(BF16), v, preferred_element_type=F32)
             + jnp.dot(pm.astype(BF16), vm, preferred_element_type=F32))
        return o / l

    s_len = q_ref.shape[1]
    for c in range(s_len // tq):
        rows = slice(c * tq, (c + 1) * tq)
        o_ref[rows, :] = attend(q_ref[0, rows, :]).astype(BF16)
    om_ref[...] = attend(qm_ref[0]).astype(BF16)


def _mla_attn(q, k, v, qm, km, vm, *, batch):
    r = q.shape[1]
    s_len = r // batch
    tq = min(512, s_len)
    est = 2 * (2 * s_len * 256 * 2 + 2 * s_len * 128 * 2) + 3 * tq * s_len * 4 + tq * s_len * 2
    return pl.pallas_call(
        functools.partial(_mla_attn_kernel, tq=tq),
        grid=(batch, MLA_HEADS),
        in_specs=[
            pl.BlockSpec((1, s_len, MLA_QK_PAD), lambda b, h: (h, b, 0)),
            pl.BlockSpec((1, s_len, MLA_QK_PAD), lambda b, h: (h, b, 0)),
            pl.BlockSpec((1, s_len, MLA_V), lambda b, h: (h, b, 0)),
            pl.BlockSpec((1, N_META, MLA_QK_PAD), lambda b, h: (h, b, 0)),
            pl.BlockSpec((1, N_META, MLA_QK_PAD), lambda b, h: (h, b, 0)),
            pl.BlockSpec((1, N_META, MLA_V), lambda b, h: (h, b, 0)),
        ],
        out_specs=[
            pl.BlockSpec((s_len, MLA_V), lambda b, h: (b, h)),
            pl.BlockSpec((N_META, MLA_V), lambda b, h: (b, h)),
        ],
        out_shape=[
            jax.ShapeDtypeStruct((r, MLA_HEADS * MLA_V), BF16),
            jax.ShapeDtypeStruct((batch * N_META, MLA_HEADS * MLA_V), BF16),
        ],
        compiler_params=_params(("parallel", "parallel"), est),
        name="mla_attn",
    )(q, k, v, qm, km, vm)


def _na_bias_tables(rel_bias, rows):
    nqb = rows // NA_QROWS
    wh = NA_WIN_H
    qc = np.arange(GRID_W)
    kc = np.arange(GRID_W)
    cstart = np.clip(qc - NA_WIN_W // 2, 0, GRID_W - NA_WIN_W)
    col_ok = (kc[None, :] >= cstart[:, None]) & (kc[None, :] < cstart[:, None] + NA_WIN_W)
    col_off = np.clip(kc[None, :] - qc[:, None], -(NA_WIN_W - 1), NA_WIN_W - 1) + NA_WIN_W - 1
    n_co = 2 * NA_WIN_W - 1
    onehot = np.zeros((n_co, GRID_W * GRID_W), np.float32)
    onehot[col_off.reshape(-1), np.arange(GRID_W * GRID_W)] = 1.0
    h = rel_bias.shape[0]
    tiles = jnp.dot(rel_bias.astype(F32).reshape(h * (2 * wh - 1), n_co), jnp.asarray(onehot),
                    precision=lax.Precision.HIGHEST).reshape(h, 2 * wh - 1, GRID_W, GRID_W)
    ro_idx = np.zeros((3, NA_QROWS, NA_KROWS), np.int32)
    ok = np.zeros((3, NA_QROWS, GRID_W, NA_KROWS, GRID_W), bool)
    for v, qb in enumerate((0, 1, nqb - 1)):
        kstart = int(np.clip(NA_QROWS * qb - wh // 2, 0, rows - NA_KROWS))
        for i in range(NA_QROWS):
            r = NA_QROWS * qb + i
            rs = int(np.clip(r - wh // 2, 0, rows - wh))
            for j in range(NA_KROWS):
                kr = kstart + j
                in_row = rs <= kr < rs + wh
                ro_idx[v, i, j] = int(np.clip(kr - r + wh - 1, 0, 2 * wh - 2))
                ok[v, i, :, j, :] = col_ok if in_row else False
    t = jnp.take(tiles, jnp.asarray(ro_idx.reshape(-1)), axis=1)
    t = t.reshape(h, 3, NA_QROWS, NA_KROWS, GRID_W, GRID_W).transpose(0, 1, 2, 4, 3, 5)
    t = jnp.where(jnp.asarray(ok)[None], t, NEG_INF)
    return t.reshape(h, 3, NA_QROWS * GRID_W, NA_KROWS * GRID_W)


def _na_attn_kernel(q_ref, k_ref, v_ref, qm_ref, km_ref, vm_ref, bias_ref, mb_ref, o_ref, om_ref, *, rows):
    nqb = rows // NA_QROWS
    qblk = NA_QROWS * GRID_W
    kblk = NA_KROWS * GRID_W
    km = _pad_meta_rows(km_ref[...])
    vm = _pad_meta_rows(vm_ref[...])
    lane = lax.broadcasted_iota(jnp.int32, (1, 2 * NA_HEAD_DIM), 1)
    head_lanes = [lane < NA_HEAD_DIM, lane >= NA_HEAD_DIM]

    def softmax_pv(s, sm, v_loc):
        m = jnp.maximum(jnp.max(s, axis=-1, keepdims=True), jnp.max(sm, axis=-1, keepdims=True))
        p = jnp.exp(s - m)
        pm = jnp.exp(sm - m)
        l = jnp.sum(p, axis=-1, keepdims=True) + jnp.sum(pm, axis=-1, keepdims=True)
        o = (jnp.dot(p.astype(BF16), v_loc, preferred_element_type=F32)
             + jnp.dot(pm.astype(BF16), vm, preferred_element_type=F32))
        return o / l

    def block(qb, carry):
        q0 = pl.multiple_of(qb * qblk, qblk)
        k0 = pl.multiple_of(jnp.clip(NA_QROWS * qb - NA_WIN_H // 2, 0, rows - NA_KROWS) * GRID_W, GRID_W)
        variant = jnp.where(qb == 0, 0, jnp.where(qb == nqb - 1, 2, 1))
        q = q_ref[pl.ds(q0, qblk), :]
        k_loc = k_ref[pl.ds(k0, kblk), :]
        v_loc = v_ref[pl.ds(k0, kblk), :]
        outs = []
        for hh in range(2):
            qh = jnp.where(head_lanes[hh], q, jnp.zeros_like(q))
            s = _dot_nt(qh, k_loc) + bias_ref[0, hh, variant]
            sm = _dot_nt(qh, km) + mb_ref[0, hh]
            outs.append(softmax_pv(s, sm, v_loc))
        o_ref[pl.ds(q0, qblk), :] = jnp.where(head_lanes[0], outs[0], outs[1]).astype(BF16)
        return carry

    lax.fori_loop(0, nqb, block, 0)

    qm = qm_ref[...]
    outs = []
    for hh in range(2):
        qh = jnp.where(head_lanes[hh], qm, jnp.zeros_like(qm))
        sm = _dot_nt(qh, km) + mb_ref[0, hh]
        m = jnp.max(sm, axis=-1, keepdims=True)
        pm = jnp.exp(sm - m)
        l = jnp.sum(pm, axis=-1, keepdims=True)
        outs.append(jnp.dot(pm.astype(BF16), vm, preferred_element_type=F32) / l)
    om_ref[...] = jnp.where(head_lanes[0], outs[0], outs[1]).astype(BF16)


def _na_attn(proj, proj_m, bias_tab, meta_bias, *, batch):
    r = proj.shape[0]
    s_len = r // batch
    rows = s_len // GRID_W
    assert rows % NA_QROWS == 0 and rows >= NA_KROWS, rows
    npair = NA_HEADS // 2
    qblk = NA_QROWS * GRID_W
    kblk = NA_KROWS * GRID_W
    bias5 = bias_tab.reshape(npair, 2, 3, qblk, kblk)
    mb4 = jnp.concatenate([meta_bias.astype(F32), jnp.full((NA_HEADS, V7X_LANES - N_META), NEG_INF, F32)],
                          axis=1).reshape(npair, 2, 1, V7X_LANES)
    est = 2 * (3 * s_len * 128 * 2 + 6 * qblk * kblk * 4 + s_len * 128 * 2) + 6 * qblk * kblk * 4
    return pl.pallas_call(
        functools.partial(_na_attn_kernel, rows=rows),
        grid=(npair, batch),
        in_specs=[
            pl.BlockSpec((s_len, 128), lambda p, b: (b, p)),
            pl.BlockSpec((s_len, 128), lambda p, b: (b, npair + p)),
            pl.BlockSpec((s_len, 128), lambda p, b: (b, 2 * npair + p)),
            pl.BlockSpec((N_META, 128), lambda p, b: (b, p)),
            pl.BlockSpec((N_META, 128), lambda p, b: (b, npair + p)),
            pl.BlockSpec((N_META, 128), lambda p, b: (b, 2 * npair + p)),
            pl.BlockSpec((1, 2, 3, qblk, kblk), lambda p, b: (p, 0, 0, 0, 0)),
            pl.BlockSpec((1, 2, 1, V7X_LANES), lambda p, b: (p, 0, 0, 0)),
        ],
        out_specs=[
            pl.BlockSpec((s_len, 128), lambda p, b: (b, p)),
            pl.BlockSpec((N_META, 128), lambda p, b: (b, p)),
        ],
        out_shape=[
            jax.ShapeDtypeStruct((r, NA_WIDTH), BF16),
            jax.ShapeDtypeStruct((batch * N_META, NA_WIDTH), BF16),
        ],
        compiler_params=_params(("parallel", "parallel"), est),
        name="na_attn",
    )(proj, proj, proj, proj_m, proj_m, proj_m, bias5, mb4)


def _merge_kernel(ona_ref, omla_ref, wna_ref, wmla_ref, ga_ref, gb_ref, o_ref):
    a = jnp.dot(ona_ref[...], wna_ref[...], preferred_element_type=F32)
    b = jnp.dot(omla_ref[...], wmla_ref[...], preferred_element_type=F32)
    o_ref[...] = (ga_ref[...].astype(F32) * a + gb_ref[...].astype(F32) * b).astype(BF16)


def _merge(o_na, o_mla, w_na, w_mla, proj, *, tm):
    r = o_na.shape[0]
    d = w_na.shape[1]
    tn = 1024 if d % 1024 == 0 else 512
    ga_blk = (3 * NA_WIDTH + 2 * MLA_RANK) // tn
    gb_blk = ga_blk + d // tn
    est = 2 * (tm * (NA_WIDTH + MLA_HEADS * MLA_V) * 2 + (NA_WIDTH + MLA_HEADS * MLA_V) * tn * 2
               + 3 * tm * tn * 2) + 3 * tm * tn * 4
    return pl.pallas_call(
        _merge_kernel,
        grid=(r // tm, d // tn),
        in_specs=[
            pl.BlockSpec((tm, NA_WIDTH), lambda i, j: (i, 0)),
            pl.BlockSpec((tm, MLA_HEADS * MLA_V), lambda i, j: (i, 0)),
            pl.BlockSpec((NA_WIDTH, tn), lambda i, j: (0, j)),
            pl.BlockSpec((MLA_HEADS * MLA_V, tn), lambda i, j: (0, j)),
            pl.BlockSpec((tm, tn), lambda i, j: (i, ga_blk + j)),
            pl.BlockSpec((tm, tn), lambda i, j: (i, gb_blk + j)),
        ],
        out_specs=pl.BlockSpec((tm, tn), lambda i, j: (i, j)),
        out_shape=jax.ShapeDtypeStruct((r, d), BF16),
        compiler_params=_params(("parallel", "arbitrary"), est),
        name="merge",
    )(o_na, o_mla, w_na, w_mla, proj, proj)


def _out_proj_kernel(m_ref, w_ref, x_ref, o_ref):
    o_ref[...] = x_ref[...] + jnp.dot(m_ref[...], w_ref[...], preferred_element_type=F32)


def _out_proj(merged, w_out, x2d, *, tm):
    r, d = x2d.shape
    tn = 1024 if d % 1024 == 0 else 512
    est = 2 * (tm * d * 2 + d * tn * 2 + 2 * tm * tn * 4) + tm * tn * 4
    return pl.pallas_call(
        _out_proj_kernel,
        grid=(r // tm, d // tn),
        in_specs=[
            pl.BlockSpec((tm, d), lambda i, j: (i, 0)),
            pl.BlockSpec((d, tn), lambda i, j: (0, j)),
            pl.BlockSpec((tm, tn), lambda i, j: (i, j)),
        ],
        out_specs=pl.BlockSpec((tm, tn), lambda i, j: (i, j)),
        out_shape=jax.ShapeDtypeStruct((r, d), F32),
        compiler_params=_params(("parallel", "arbitrary"), est),
        name="out_proj",
    )(merged, w_out, x2d)


def _peer_q_kernel(h_ref, g_ref, w_ref, q_ref, hnt_ref, hn_ref):
    @pl.when(pl.program_id(1) == 0)
    def _():
        hn = _rmsnorm(h_ref[...], g_ref[...])
        hn_ref[...] = hn.astype(BF16)
        hnt_ref[...] = hn.T.astype(BF16)

    q_ref[...] = jnp.dot(hn_ref[...], w_ref[...], preferred_element_type=F32).astype(BF16)


def _peer_q(h2, g2, w_q, *, tm):
    r, d = h2.shape
    n = w_q.shape[1]
    tn = 1024
    est = 2 * (tm * d * 4 + d * tn * 2 + tm * tn * 2 + tm * d * 2) + tm * d * 2 + tm * tn * 4 + 2 * tm * d * 4
    return pl.pallas_call(
        _peer_q_kernel,
        grid=(r // tm, n // tn),
        in_specs=[
            pl.BlockSpec((tm, d), lambda i, j: (i, 0)),
            pl.BlockSpec((1, d), lambda i, j: (0, 0)),
            pl.BlockSpec((d, tn), lambda i, j: (0, j)),
        ],
        out_specs=[
            pl.BlockSpec((tm, tn), lambda i, j: (i, j)),
            pl.BlockSpec((d, tm), lambda i, j: (0, i)),
        ],
        out_shape=[jax.ShapeDtypeStruct((r, n), BF16), jax.ShapeDtypeStruct((d, r), BF16)],
        scratch_shapes=[pltpu.VMEM((tm, d), BF16)],
        compiler_params=_params(("parallel", "arbitrary"), est),
        name="peer_q",
    )(h2, g2, w_q)


def _compare_exchange(a, b):
    if a is None:
        return b, None
    if b is None:
        return a, None
    return jnp.maximum(a, b), jnp.minimum(a, b)


def _bitonic_merge(xs):
    n = len(xs)
    step = n // 2
    while step >= 1:
        for i in range(n):
            if i & step == 0:
                xs[i], xs[i + step] = _compare_exchange(xs[i], xs[i + step])
        step //= 2
    return xs


def _bitonic_sort(xs):
    n = len(xs)
    xs = list(xs)
    size = 2
    while size <= n:
        step = size // 2
        while step >= 1:
            for i in range(n):
                if i & step == 0:
                    hi, lo = _compare_exchange(xs[i], xs[i + step])
                    xs[i], xs[i + step] = (hi, lo) if i & size == 0 else (lo, hi)
            step //= 2
        size *= 2
    return xs


def _merge_topk(xs, ys, k):
    xs = list(xs) + [None] * (k - len(xs))
    ys = list(ys) + [None] * (k - len(ys))
    return _bitonic_merge([_compare_exchange(xs[i], ys[k - 1 - i])[0] for i in range(k)])


def _pow2_ceil(n):
    return 1 << (n - 1).bit_length()


def _sorted_topk_slabs(s, k):
    slabs = _bitonic_sort([s[8 * v:8 * v + 8, :] for v in range(s.shape[0] // 8)])
    for shift in (4, 2, 1):
        slabs = _merge_topk(slabs, [pltpu.roll(x, shift, axis=0) for x in slabs], k)
    return slabs


def _extract_topk(x, k):
    n, t = x.shape
    row = lax.broadcasted_iota(jnp.int32, (n, t), 0).astype(F32)
    krow = lax.broadcasted_iota(jnp.int32, (k, t), 0)

    def body(i, carry):
        x, rank, vals = carry
        m = jnp.max(x, axis=0, keepdims=True)
        first = jnp.min(jnp.where(x == m, row, float(n)), axis=0, keepdims=True)
        hit = row == first
        rank = jnp.where(hit, i.astype(F32), rank)
        x = jnp.where(hit, _F32_MIN, x)
        vals = jnp.where(krow == i, m, vals)
        return x, rank, vals

    _, rank, vals = lax.fori_loop(
        0, k, body, (x, jnp.full((n, t), float(k), F32), jnp.zeros((k, t), F32)))
    return vals, rank


def _route_head_exact(s1, s2):
    k = PEER_TOPK
    top1, rank1 = _extract_topk(s1, k)
    top2, rank2 = _extract_topk(s2, k)

    tm = s1.shape[1]
    pieces, spans = [], []
    off = 0
    for i in range(k):
        cnt = k // (i + 1)
        rows = -(-cnt // 8) * 8
        jrow = lax.broadcasted_iota(jnp.int32, (rows, tm), 0)
        pieces.append(jnp.where(jrow < cnt, top1[i:i + 1, :] + top2[0:rows, :], _F32_MIN))
        spans.append((off, rows))
        off += rows
    cand = jnp.concatenate(pieces, axis=0)
    ctop, crank = _extract_topk(cand, k)
    chosen = crank < float(k)
    cmax = ctop[0:1, :]
    z = jnp.sum(jnp.where(chosen, jnp.exp(cand - cmax), 0.0), axis=0, keepdims=True)

    y1 = jnp.ones(s1.shape, F32)
    for i, (off, rows) in enumerate(spans):
        n_sel = jnp.sum(jnp.where(chosen[off:off + rows, :], 1.0, 0.0), axis=0, keepdims=True)
        y1 = jnp.where(rank1 == float(i), 1.0 - n_sel, y1)
    return y1, jnp.exp(s1 - top1[0:1, :]) / z, -rank2, jnp.exp(s2 - top2[0:1, :])


def _route_tables_by_value(scores):
    k = PEER_TOPK
    tm = scores[0][0].shape[1]

    sublane = lax.broadcasted_iota(jnp.int32, (8, tm), 0)
    tops = []
    for side in range(2):
        packed = None
        for h in range(PEER_HEADS):
            slabs = _sorted_topk_slabs(scores[h][side], k)
            packed = slabs if packed is None else [jnp.where(sublane == h, s, p) for s, p in zip(slabs, packed)]
        tops.append(packed)
    top1, top2 = tops

    cells = {(i, j): top1[i] + top2[j] for i in range(k) for j in range(k // (i + 1))}
    grid_rows = [[cells[i, j] for j in range(k // (i + 1))] for i in range(k)]
    lists = [r for r in grid_rows[1:] if len(r) > 1] + [[r[0] for r in grid_rows if len(r) == 1]]
    while len(lists) > 1:
        lists.sort(key=len)
        size = min(k, _pow2_ceil(len(lists[0]) + len(lists[1])))
        merged = [m for m in _merge_topk(lists[0], lists[1], size) if m is not None]
        lists = [merged] + lists[2:]
    ctop = _merge_topk(grid_rows[0], lists[0], k)
    thr, cmax = ctop[k - 1], ctop[0]

    taken = {c: v >= thr for c, v in cells.items()}
    n_taken = sum(jnp.where(t, 1.0, 0.0) for t in taken.values())
    z = sum(jnp.where(taken[c], jnp.exp(v - cmax), 0.0) for c, v in cells.items())
    inv_z = 1.0 / z
    y1_rows = [1.0 - sum(jnp.where(taken[i, j], 1.0, 0.0) for j in range(k // (i + 1))) for i in range(k)]

    exact = jnp.where(n_taken == float(k), 1.0, 0.0)
    for top in (top1, top2):
        for i in range(k - 1):
            exact = jnp.where(top[i] > top[i + 1], exact, 0.0)

    tables = []
    for h in range(PEER_HEADS):
        s1, s2 = scores[h]
        row = slice(h, h + 1)
        y1 = jnp.ones(s1.shape, F32)
        for i in range(k):
            y1 = jnp.where(s1 == top1[i][row], y1_rows[i][row], y1)
        y2 = jnp.full(s2.shape, -float(k), F32)
        for j in reversed(range(k)):
            y2 = jnp.where(s2 >= top2[j][row], -float(j), y2)
        tables.append((y1, jnp.exp(s1 - top1[0][row]) * inv_z[row], y2, jnp.exp(s2 - top2[0][row])))
        for s, top in ((s1, top1), (s2, top2)):
            members = jnp.sum(jnp.where(s >= top[k - 1][row], 1.0, 0.0), axis=0, keepdims=True)
            exact = jnp.where(members == float(k), exact, 0.0)
    return tables, exact


def _peer_route_kernel(q_ref, keys_ref, y1_ref, e1_ref, y2_ref, e2_ref):
    half = PEER_DK // 2
    scores = []
    for h in range(PEER_HEADS):
        q = q_ref[:, h * PEER_DK:(h + 1) * PEER_DK]
        scores.append((_dot_nt(keys_ref[h, 0], q[:, :half]),
                       _dot_nt(keys_ref[h, 1], q[:, half:])))

    def store(h, tables):
        for ref, t in zip((y1_ref, e1_ref, y2_ref, e2_ref), tables):
            ref[h] = t

    tables, exact = _route_tables_by_value(scores)
    for h in range(PEER_HEADS):
        store(h, tables[h])

    @pl.when(jnp.min(exact) < 0.5)
    def _():
        for h in range(PEER_HEADS):
            store(h, _route_head_exact(*scores[h]))


def _peer_route(q2, sub_keys, *, tm):
    r = q2.shape[0]
    width = PEER_HEADS * PEER_DK
    tab_spec = pl.BlockSpec((PEER_HEADS, PEER_NKEYS, tm), lambda i: (0, 0, i))
    tab = jax.ShapeDtypeStruct((PEER_HEADS, PEER_NKEYS, r), F32)
    est = (2 * (tm * width * 2 + PEER_HEADS * PEER_NKEYS * PEER_DK * 2 + PEER_HEADS * PEER_NKEYS * tm * 16)
           + 40 * PEER_HEADS * PEER_NKEYS * tm * 4)
    return pl.pallas_call(
        _peer_route_kernel,
        grid=(r // tm,),
        in_specs=[
            pl.BlockSpec((tm, width), lambda i: (i, 0)),
            pl.BlockSpec((PEER_HEADS, 2, PEER_NKEYS, PEER_DK // 2), lambda i: (0, 0, 0, 0)),
        ],
        out_specs=[tab_spec, tab_spec, tab_spec, tab_spec],
        out_shape=[tab, tab, tab, tab],
        compiler_params=_params(("parallel",), est),
        name="peer_route",
    )(q2, sub_keys)


def _peer_dense_kernel(hnt_ref, u_ref, vt_ref, y1_ref, e1_ref, y2_ref, e2_ref, h_ref, g_ref, o_ref,
                       acc_ref, act_ref, p_ref, *, te, final_norm):
    j = pl.program_id(1)
    d, tm = acc_ref.shape

    @pl.when(j == 0)
    def _():
        acc_ref[...] = jnp.zeros_like(acc_ref)

    act_ref[...] = jax.nn.gelu(jnp.dot(u_ref[...], hnt_ref[...], preferred_element_type=F32))
    a_blocks = te // PEER_NKEYS
    a0 = j * a_blocks
    for ai in range(a_blocks):
        rows = slice(ai * PEER_NKEYS, (ai + 1) * PEER_NKEYS)
        y1_rows = [y1_ref[h, pl.ds(a0 + ai, 1), :] for h in range(PEER_HEADS)]
        e1_rows = [e1_ref[h, pl.ds(a0 + ai, 1), :] for h in range(PEER_HEADS)]
        for c in range(tm // V7X_LANES):
            cols = slice(c * V7X_LANES, (c + 1) * V7X_LANES)
            gate = None
            for h in range(PEER_HEADS):
                sel = y2_ref[h, :, cols] >= y1_rows[h][:, cols]
                term = jnp.where(sel, e1_rows[h][:, cols] * e2_ref[h, :, cols], 0.0)
                gate = term if gate is None else gate + term
            p_ref[rows, cols] = (gate * act_ref[rows, cols]).astype(BF16)
    acc_ref[...] += jnp.dot(vt_ref[...], p_ref[...], preferred_element_type=F32)

    @pl.when(j == pl.num_programs(1) - 1)
    def _():
        out = h_ref[...] + acc_ref[...].T
        o_ref[...] = _rmsnorm(out, g_ref[...]) if final_norm else out


def _peer_dense(hn2t, u, vt, tabs, h2, gf, *, tm, te, final_norm):
    d, r = hn2t.shape
    e = u.shape[0]
    y1, e1, y2, e2 = tabs
    tab_spec = pl.BlockSpec((PEER_HEADS, PEER_NKEYS, tm), lambda i, j: (0, 0, i))
    est = (2 * (tm * d * 2 + 2 * te * d * 2 + PEER_HEADS * PEER_NKEYS * tm * 16 + tm * d * 4)
           + tm * d * 4 + d * tm * 4 + 6 * te * tm * 4)
    return pl.pallas_call(
        functools.partial(_peer_dense_kernel, te=te, final_norm=final_norm),
        grid=(r // tm, e // te),
        in_specs=[
            pl.BlockSpec((d, tm), lambda i, j: (0, i)),
            pl.BlockSpec((te, d), lambda i, j: (j, 0)),
            pl.BlockSpec((d, te), lambda i, j: (0, j)),
            tab_spec, tab_spec, tab_spec, tab_spec,
            pl.BlockSpec((tm, d), lambda i, j: (i, 0), pipeline_mode=pl.Buffered(1)),
            pl.BlockSpec((1, d), lambda i, j: (0, 0)),
        ],
        out_specs=pl.BlockSpec((tm, d), lambda i, j: (i, 0)),
        out_shape=jax.ShapeDtypeStruct((r, d), F32),
        scratch_shapes=[pltpu.VMEM((d, tm), F32), pltpu.VMEM((te, tm), F32), pltpu.VMEM((te, tm), BF16)],
        compiler_params=_params(("parallel", "arbitrary"), est),
        name="peer_dense",
    )(hn2t, u, vt, y1, e1, y2, e2, h2, gf)


def _rope_tables(pos):
    inv_freq = ROPE_THETA ** (-jnp.arange(0, MLA_ROPE, 2, dtype=F32) / MLA_ROPE)
    ang = pos.astype(F32)[:, None] * inv_freq[None, :]
    cos, sin = jnp.cos(ang), jnp.sin(ang)
    zeros = jnp.zeros((pos.shape[0], V7X_LANES - MLA_ROPE), F32)
    return (jnp.concatenate([cos, cos, zeros], axis=1), jnp.concatenate([-sin, sin, zeros], axis=1))


def _swap_halves(w):
    half = w.shape[-1] // 2
    return jnp.concatenate([w[..., half:], w[..., :half]], axis=-1)


def _layer(x_real, x_meta, p, l, *, batch):
    d = x_real.shape[1]
    s_len = x_real.shape[0] // batch
    tm_real = _row_tile(x_real.shape[0], 1024)
    tm_meta = x_meta.shape[0]

    w_in = p["w_in"][l]
    kr0 = 3 * NA_WIDTH + 2 * MLA_RANK
    w_main = jnp.concatenate([w_in[:, :kr0], w_in[:, kr0 + MLA_ROPE:]], axis=1).astype(BF16)
    w_kr = w_in[:, kr0:kr0 + MLA_ROPE]
    w_kr = jnp.concatenate([w_kr, _swap_halves(w_kr)], axis=1).astype(BF16)
    norm_g = jnp.concatenate([p["mla_q_norm_g"][l], p["mla_kv_norm_g"][l]])[None].astype(F32)
    g1 = p["norm1_g"][l][None].astype(F32)
    wq = p["mla_w_uq"][l].reshape(MLA_RANK, MLA_HEADS, MLA_NOPE + MLA_ROPE)
    wq = jnp.concatenate([wq, _swap_halves(wq[..., MLA_NOPE:])], axis=-1)
    wq = wq.transpose(1, 0, 2).astype(BF16)
    wkv = p["mla_w_ukv"][l].reshape(MLA_RANK, MLA_HEADS, MLA_NOPE + MLA_V).transpose(1, 0, 2).astype(BF16)
    cos_r, sin_r = _rope_tables(N_META + jnp.arange(s_len))
    cos_m, sin_m = _rope_tables(jnp.tile(jnp.arange(N_META), batch))
    bias_tab = _na_bias_tables(p["na_rel_bias"][l], s_len // GRID_W)

    proj_r, kr_r = _in_proj(x_real, g1, w_main, w_kr, norm_g, tm=tm_real)
    proj_m, kr_m = _in_proj(x_meta, g1, w_main, w_kr, norm_g, tm=tm_meta)
    q_r, k_r, v_r = _mla_up(proj_r, kr_r, cos_r, sin_r, wq, wkv, tm=_row_tile(s_len, 1024))
    q_m, k_m, v_m = _mla_up(proj_m, kr_m, cos_m, sin_m, wq, wkv, tm=tm_meta)
    omla_r, omla_m = _mla_attn(q_r, k_r, v_r, q_m, k_m, v_m, batch=batch)
    ona_r, ona_m = _na_attn(proj_r, proj_m, bias_tab, p["na_meta_bias"][l], batch=batch)

    w_na = p["w_na_branch"][l].astype(BF16)
    w_mla = p["w_mla_branch"][l].astype(BF16)
    w_out = p["w_out"][l].astype(BF16)
    h_real = _out_proj(_merge(ona_r, omla_r, w_na, w_mla, proj_r, tm=tm_real), w_out, x_real, tm=tm_real)
    h_meta = _out_proj(_merge(ona_m, omla_m, w_na, w_mla, proj_m, tm=tm_meta), w_out, x_meta, tm=tm_meta)
    return h_real, h_meta


def _peer(h2, p, l, gf, *, tm, te, final_norm):
    g2 = p["norm2_g"][l][None].astype(F32)
    w_q = p["peer_w_q"][l].astype(BF16)
    keys = p["peer_sub_keys"][l].astype(BF16)
    u = p["peer_u"][l].astype(BF16)
    vt = p["peer_v"][l].astype(BF16).T
    q2, hn2t = _peer_q(h2, g2, w_q, tm=_row_tile(h2.shape[0], 1024))
    tabs = _peer_route(q2, keys, tm=min(256, h2.shape[0]))
    return _peer_dense(hn2t, u, vt, tabs, h2, gf, tm=tm, te=te, final_norm=final_norm)


def kernel(x, meta_tokens, norm1_g, w_in, na_rel_bias, na_meta_bias, mla_q_norm_g, mla_w_uq, mla_kv_norm_g,
           mla_w_ukv, w_na_branch, w_mla_branch, w_out, norm2_g, peer_w_q, peer_sub_keys, peer_u, peer_v,
           final_norm_g):
    b, s, d = x.shape
    depth = w_in.shape[0]
    p = dict(norm1_g=norm1_g, w_in=w_in, na_rel_bias=na_rel_bias, na_meta_bias=na_meta_bias,
             mla_q_norm_g=mla_q_norm_g, mla_w_uq=mla_w_uq, mla_kv_norm_g=mla_kv_norm_g, mla_w_ukv=mla_w_ukv,
             w_na_branch=w_na_branch, w_mla_branch=w_mla_branch, w_out=w_out, norm2_g=norm2_g,
             peer_w_q=peer_w_q, peer_sub_keys=peer_sub_keys, peer_u=peer_u, peer_v=peer_v)
    x_real = x.reshape(b * s, d)
    x_meta = jnp.broadcast_to(meta_tokens.astype(x.dtype)[None], (b, N_META, d)).reshape(b * N_META, d)
    gf = final_norm_g[None].astype(F32)
    tm = _row_tile(b * s, 512)
    te = 512
    for l in range(depth):
        h_real, h_meta = _layer(x_real, x_meta, p, l, batch=b)
        x_real = _peer(h_real, p, l, gf, tm=tm, te=te, final_norm=l == depth - 1)
        x_meta = _peer(h_meta, p, l, gf, tm=h_meta.shape[0], te=te, final_norm=False)
    return x_real.reshape(b, s, d)
```

```python
import functools

import jax
import jax.numpy as jnp
import numpy as np
from jax import lax
from jax.experimental import pallas as pl
from jax.experimental.pallas import tpu as pltpu

GRID_W = 64
N_META = 16
NA_HEADS = 16
NA_HEAD_DIM = 64
NA_WIN_H = 8
NA_WIN_W = 16
NA_WIDTH = NA_HEADS * NA_HEAD_DIM
MLA_HEADS = 16
MLA_RANK = 512
MLA_NOPE = 128
MLA_ROPE = 64
MLA_V = 128
MLA_QK_PAD = 256
ROPE_THETA = 10000.0
PEER_HEADS = 8
PEER_NKEYS = 128
PEER_DK = 256
PEER_TOPK = 16
NORM_EPS = 1e-6
NEG_INF = -1e30

V7X_VMEM_BYTES = 64 * 1024 * 1024
V7X_VMEM_HEADROOM = 6 * 1024 * 1024
V7X_LANES = 128

NA_QROWS = 4
NA_KROWS = NA_QROWS + NA_WIN_H

F32 = jnp.float32
BF16 = jnp.bfloat16
_F32_MIN = float(np.finfo(np.float32).min)


def _vmem_limit(estimate_bytes):
    want = int(estimate_bytes * 1.25) + 4 * 1024 * 1024
    return min(max(want, 16 * 1024 * 1024), V7X_VMEM_BYTES - V7X_VMEM_HEADROOM)


def _params(semantics, vmem_estimate, flags=None):
    return pltpu.CompilerParams(dimension_semantics=semantics,
                                vmem_limit_bytes=_vmem_limit(vmem_estimate), flags=flags)


def _row_tile(rows, want):
    tile = min(rows, want)
    assert rows % tile == 0, (rows, tile)
    return tile


def _dot_nt(a, b):
    return lax.dot_general(a, b, (((1,), (1,)), ((), ())), preferred_element_type=F32)


def _rmsnorm(x, g):
    return x * lax.rsqrt(jnp.mean(x * x, axis=-1, keepdims=True) + NORM_EPS) * g


def _pad_meta_rows(a):
    return jnp.concatenate([a, jnp.zeros((V7X_LANES - N_META, a.shape[1]), a.dtype)], axis=0)


def _in_proj_kernel(x_ref, g_ref, w_ref, wkr_ref, ng_ref, o_ref, kr_ref, hn_ref, *, tn):
    j = pl.program_id(1)

    @pl.when(j == 0)
    def _():
        hn = _rmsnorm(x_ref[...], g_ref[...]).astype(BF16)
        hn_ref[...] = hn
        kr_ref[...] = jnp.dot(hn, wkr_ref[...], preferred_element_type=F32)

    acc = jnp.dot(hn_ref[...], w_ref[...], preferred_element_type=F32)
    col0 = j * tn
    plain_end = 3 * NA_WIDTH
    norm_end = plain_end + 2 * MLA_RANK

    @pl.when(col0 < plain_end)
    def _():
        scale = jnp.where(col0 < NA_WIDTH, NA_HEAD_DIM ** -0.5, 1.0).astype(F32)
        o_ref[...] = (acc * scale).astype(BF16)

    @pl.when(jnp.logical_and(col0 >= plain_end, col0 < norm_end))
    def _():
        for c in range(tn // MLA_RANK):
            sl = slice(c * MLA_RANK, (c + 1) * MLA_RANK)
            o_ref[:, sl] = _rmsnorm(acc[:, sl], ng_ref[:, sl]).astype(BF16)

    @pl.when(col0 >= norm_end)
    def _():
        o_ref[...] = jax.nn.sigmoid(acc).astype(BF16)


def _in_proj(x2d, g1, w_main, w_kr, norm_g, *, tm):
    r, d = x2d.shape
    n = w_main.shape[1]
    tn = 1024 if d % 1024 == 0 else 512
    norm_blk0 = (3 * NA_WIDTH) // tn
    norm_nblk = (2 * MLA_RANK) // tn
    est = 2 * (tm * d * 4 + d * tn * 2 + tm * tn * 2 + tm * 128 * 4 + d * 128 * 2) + tm * d * 2 + tm * tn * 4
    return pl.pallas_call(
        functools.partial(_in_proj_kernel, tn=tn),
        grid=(r // tm, n // tn),
        in_specs=[
            pl.BlockSpec((tm, d), lambda i, j: (i, 0)),
            pl.BlockSpec((1, d), lambda i, j: (0, 0)),
            pl.BlockSpec((d, tn), lambda i, j: (0, j)),
            pl.BlockSpec((d, 128), lambda i, j: (0, 0)),
            pl.BlockSpec((1, tn), lambda i, j: (0, jnp.clip(j - norm_blk0, 0, norm_nblk - 1))),
        ],
        out_specs=[
            pl.BlockSpec((tm, tn), lambda i, j: (i, j)),
            pl.BlockSpec((tm, 128), lambda i, j: (i, 0)),
        ],
        out_shape=[jax.ShapeDtypeStruct((r, n), BF16), jax.ShapeDtypeStruct((r, 128), F32)],
        scratch_shapes=[pltpu.VMEM((tm, d), BF16)],
        compiler_params=_params(("parallel", "arbitrary"), est),
        name="in_proj",
    )(x2d, g1, w_main, w_kr, norm_g)


def _rope_half(t, cos, sin):
    return t * cos + pltpu.roll(t, 64, axis=1) * sin


def _mla_up_kernel(cq_ref, ckv_ref, kr_ref, cos_ref, sin_ref, wq_ref, wkv_ref, q_ref, k_ref, v_ref):
    cos = cos_ref[...]
    sin = sin_ref[...]
    scale = (MLA_NOPE + MLA_ROPE) ** -0.5
    yq = jnp.dot(cq_ref[...], wq_ref[0], preferred_element_type=F32)
    q_ref[0, :, 0:MLA_NOPE] = (yq[:, 0:MLA_NOPE] * scale).astype(BF16)
    q_ref[0, :, MLA_NOPE:] = (_rope_half(yq[:, MLA_NOPE:], cos, sin) * scale).astype(BF16)
    ykv = jnp.dot(ckv_ref[...], wkv_ref[0], preferred_element_type=F32)
    k_ref[0, :, 0:MLA_NOPE] = ykv[:, 0:MLA_NOPE].astype(BF16)
    k_ref[0, :, MLA_NOPE:] = _rope_half(kr_ref[...], cos, sin).astype(BF16)
    v_ref[0] = ykv[:, MLA_NOPE:].astype(BF16)


def _mla_up(proj, kr, cos_t, sin_t, wq_h, wkv_h, *, tm):
    r = proj.shape[0]
    cq_blk = (3 * NA_WIDTH) // MLA_RANK
    n_pos_blk = cos_t.shape[0] // tm
    est = 2 * (2 * tm * MLA_RANK * 2 + 3 * tm * 128 * 4 + 2 * MLA_RANK * 256 * 2
               + 2 * tm * 256 * 2 + tm * 128 * 2) + 4 * tm * 256 * 4
    return pl.pallas_call(
        _mla_up_kernel,
        grid=(r // tm, MLA_HEADS),
        in_specs=[
            pl.BlockSpec((tm, MLA_RANK), lambda i, h: (i, cq_blk)),
            pl.BlockSpec((tm, MLA_RANK), lambda i, h: (i, cq_blk + 1)),
            pl.BlockSpec((tm, 128), lambda i, h: (i, 0)),
            pl.BlockSpec((tm, 128), lambda i, h: (i % n_pos_blk, 0)),
            pl.BlockSpec((tm, 128), lambda i, h: (i % n_pos_blk, 0)),
            pl.BlockSpec((1, MLA_RANK, 256), lambda i, h: (h, 0, 0)),
            pl.BlockSpec((1, MLA_RANK, 256), lambda i, h: (h, 0, 0)),
        ],
        out_specs=[
            pl.BlockSpec((1, tm, MLA_QK_PAD), lambda i, h: (h, i, 0)),
            pl.BlockSpec((1, tm, MLA_QK_PAD), lambda i, h: (h, i, 0)),
            pl.BlockSpec((1, tm, MLA_V), lambda i, h: (h, i, 0)),
        ],
        out_shape=[
            jax.ShapeDtypeStruct((MLA_HEADS, r, MLA_QK_PAD), BF16),
            jax.ShapeDtypeStruct((MLA_HEADS, r, MLA_QK_PAD), BF16),
            jax.ShapeDtypeStruct((MLA_HEADS, r, MLA_V), BF16),
        ],
        compiler_params=_params(("parallel", "arbitrary"), est),
        name="mla_up",
    )(proj, proj, kr, cos_t, sin_t, wq_h, wkv_h)


def _mla_attn_kernel(q_ref, k_ref, v_ref, qm_ref, km_ref, vm_ref, o_ref, om_ref, *, tq):
    k = k_ref[0]
    v = v_ref[0]
    km = _pad_meta_rows(km_ref[0])
    vm = _pad_meta_rows(vm_ref[0])
    meta_lane = lax.broadcasted_iota(jnp.int32, (1, V7X_LANES), 1) < N_META

    def attend(q):
        s = _dot_nt(q, k)
        sm = jnp.where(meta_lane, _dot_nt(q, km), NEG_INF)
        m = jnp.maximum(jnp.max(s, axis=-1, keepdims=True), jnp.max(sm, axis=-1, keepdims=True))
        p = jnp.exp(s - m)
        pm = jnp.exp(sm - m)
        l = jnp.sum(p, axis=-1, keepdims=True) + jnp.sum(pm, axis=-1, keepdims=True)
        o = (jnp.dot(p.astype(BF16), v, preferred_element_type=F32)
             + jnp.dot(pm.astype(BF16), vm, preferred_element_type=F32))
        return o / l

    s_len = q_ref.shape[1]
    for c in range(s_len // tq):
        rows = slice(c * tq, (c + 1) * tq)
        o_ref[rows, :] = attend(q_ref[0, rows, :]).astype(BF16)
    om_ref[...] = attend(qm_ref[0]).astype(BF16)


def _mla_attn(q, k, v, qm, km, vm, *, batch):
    r = q.shape[1]
    s_len = r // batch
    tq = min(512, s_len)
    est = 2 * (2 * s_len * 256 * 2 + 2 * s_len * 128 * 2) + 3 * tq * s_len * 4 + tq * s_len * 2
    return pl.pallas_call(
        functools.partial(_mla_attn_kernel, tq=tq),
        grid=(batch, MLA_HEADS),
        in_specs=[
            pl.BlockSpec((1, s_len, MLA_QK_PAD), lambda b, h: (h, b, 0)),
            pl.BlockSpec((1, s_len, MLA_QK_PAD), lambda b, h: (h, b, 0)),
            pl.BlockSpec((1, s_len, MLA_V), lambda b, h: (h, b, 0)),
            pl.BlockSpec((1, N_META, MLA_QK_PAD), lambda b, h: (h, b, 0)),
            pl.BlockSpec((1, N_META, MLA_QK_PAD), lambda b, h: (h, b, 0)),
            pl.BlockSpec((1, N_META, MLA_V), lambda b, h: (h, b, 0)),
        ],
        out_specs=[
            pl.BlockSpec((s_len, MLA_V), lambda b, h: (b, h)),
            pl.BlockSpec((N_META, MLA_V), lambda b, h: (b, h)),
        ],
        out_shape=[
            jax.ShapeDtypeStruct((r, MLA_HEADS * MLA_V), BF16),
            jax.ShapeDtypeStruct((batch * N_META, MLA_HEADS * MLA_V), BF16),
        ],
        compiler_params=_params(("parallel", "parallel"), est),
        name="mla_attn",
    )(q, k, v, qm, km, vm)


def _na_bias_tables(rel_bias, rows):
    nqb = rows // NA_QROWS
    wh = NA_WIN_H
    qc = np.arange(GRID_W)
    kc = np.arange(GRID_W)
    cstart = np.clip(qc - NA_WIN_W // 2, 0, GRID_W - NA_WIN_W)
    col_ok = (kc[None, :] >= cstart[:, None]) & (kc[None, :] < cstart[:, None] + NA_WIN_W)
    col_off = np.clip(kc[None, :] - qc[:, None], -(NA_WIN_W - 1), NA_WIN_W - 1) + NA_WIN_W - 1
    n_co = 2 * NA_WIN_W - 1
    onehot = np.zeros((n_co, GRID_W * GRID_W), np.float32)
    onehot[col_off.reshape(-1), np.arange(GRID_W * GRID_W)] = 1.0
    h = rel_bias.shape[0]
    tiles = jnp.dot(rel_bias.astype(F32).reshape(h * (2 * wh - 1), n_co), jnp.asarray(onehot),
                    precision=lax.Precision.HIGHEST).reshape(h, 2 * wh - 1, GRID_W, GRID_W)
    ro_idx = np.zeros((3, NA_QROWS, NA_KROWS), np.int32)
    ok = np.zeros((3, NA_QROWS, GRID_W, NA_KROWS, GRID_W), bool)
    for v, qb in enumerate((0, 1, nqb - 1)):
        kstart = int(np.clip(NA_QROWS * qb - wh // 2, 0, rows - NA_KROWS))
        for i in range(NA_QROWS):
            r = NA_QROWS * qb + i
            rs = int(np.clip(r - wh // 2, 0, rows - wh))
            for j in range(NA_KROWS):
                kr = kstart + j
                in_row = rs <= kr < rs + wh
                ro_idx[v, i, j] = int(np.clip(kr - r + wh - 1, 0, 2 * wh - 2))
                ok[v, i, :, j, :] = col_ok if in_row else False
    t = jnp.take(tiles, jnp.asarray(ro_idx.reshape(-1)), axis=1)
    t = t.reshape(h, 3, NA_QROWS, NA_KROWS, GRID_W, GRID_W).transpose(0, 1, 2, 4, 3, 5)
    t = jnp.where(jnp.asarray(ok)[None], t, NEG_INF)
    return t.reshape(h, 3, NA_QROWS * GRID_W, NA_KROWS * GRID_W)


def _na_attn_kernel(q_ref, k_ref, v_ref, qm_ref, km_ref, vm_ref, bias_ref, mb_ref, o_ref, om_ref, *, rows):
    nqb = rows // NA_QROWS
    qblk = NA_QROWS * GRID_W
    kblk = NA_KROWS * GRID_W
    km = _pad_meta_rows(km_ref[...])
    vm = _pad_meta_rows(vm_ref[...])
    lane = lax.broadcasted_iota(jnp.int32, (1, 2 * NA_HEAD_DIM), 1)
    head_lanes = [lane < NA_HEAD_DIM, lane >= NA_HEAD_DIM]

    def softmax_pv(s, sm, v_loc):
        m = jnp.maximum(jnp.max(s, axis=-1, keepdims=True), jnp.max(sm, axis=-1, keepdims=True))
        p = jnp.exp(s - m)
        pm = jnp.exp(sm - m)
        l = jnp.sum(p, axis=-1, keepdims=True) + jnp.sum(pm, axis=-1, keepdims=True)
        o = (jnp.dot(p.astype(BF16), v_loc, preferred_element_type=F32)
             + jnp.dot(pm.astype(BF16), vm, preferred_element_type=F32))
        return o / l

    def block(qb, carry):
        q0 = pl.multiple_of(qb * qblk, qblk)
        k0 = pl.multiple_of(jnp.clip(NA_QROWS * qb - NA_WIN_H // 2, 0, rows - NA_KROWS) * GRID_W, GRID_W)
        variant = jnp.where(qb == 0, 0, jnp.where(qb == nqb - 1, 2, 1))
        q = q_ref[pl.ds(q0, qblk), :]
        k_loc = k_ref[pl.ds(k0, kblk), :]
        v_loc = v_ref[pl.ds(k0, kblk), :]
        outs = []
        for hh in range(2):
            qh = jnp.where(head_lanes[hh], q, jnp.zeros_like(q))
            s = _dot_nt(qh, k_loc) + bias_ref[0, hh, variant]
            sm = _dot_nt(qh, km) + mb_ref[0, hh]
            outs.append(softmax_pv(s, sm, v_loc))
        o_ref[pl.ds(q0, qblk), :] = jnp.where(head_lanes[0], outs[0], outs[1]).astype(BF16)
        return carry

    lax.fori_loop(0, nqb, block, 0)

    qm = qm_ref[...]
    outs = []
    for hh in range(2):
        qh = jnp.where(head_lanes[hh], qm, jnp.zeros_like(qm))
        sm = _dot_nt(qh, km) + mb_ref[0, hh]
        m = jnp.max(sm, axis=-1, keepdims=True)
        pm = jnp.exp(sm - m)
        l = jnp.sum(pm, axis=-1, keepdims=True)
        outs.append(jnp.dot(pm.astype(BF16), vm, preferred_element_type=F32) / l)
    om_ref[...] = jnp.where(head_lanes[0], outs[0], outs[1]).astype(BF16)


def _na_attn(proj, proj_m, bias_tab, meta_bias, *, batch):
    r = proj.shape[0]
    s_len = r // batch
    rows = s_len // GRID_W
    assert rows % NA_QROWS == 0 and rows >= NA_KROWS, rows
    npair = NA_HEADS // 2
    qblk = NA_QROWS * GRID_W
    kblk = NA_KROWS * GRID_W
    bias5 = bias_tab.reshape(npair, 2, 3, qblk, kblk)
    mb4 = jnp.concatenate([meta_bias.astype(F32), jnp.full((NA_HEADS, V7X_LANES - N_META), NEG_INF, F32)],
                          axis=1).reshape(npair, 2, 1, V7X_LANES)
    est = 2 * (3 * s_len * 128 * 2 + 6 * qblk * kblk * 4 + s_len * 128 * 2) + 6 * qblk * kblk * 4
    return pl.pallas_call(
        functools.partial(_na_attn_kernel, rows=rows),
        grid=(npair, batch),
        in_specs=[
            pl.BlockSpec((s_len, 128), lambda p, b: (b, p)),
            pl.BlockSpec((s_len, 128), lambda p, b: (b, npair + p)),
            pl.BlockSpec((s_len, 128), lambda p, b: (b, 2 * npair + p)),
            pl.BlockSpec((N_META, 128), lambda p, b: (b, p)),
            pl.BlockSpec((N_META, 128), lambda p, b: (b, npair + p)),
            pl.BlockSpec((N_META, 128), lambda p, b: (b, 2 * npair + p)),
            pl.BlockSpec((1, 2, 3, qblk, kblk), lambda p, b: (p, 0, 0, 0, 0)),
            pl.BlockSpec((1, 2, 1, V7X_LANES), lambda p, b: (p, 0, 0, 0)),
        ],
        out_specs=[
            pl.BlockSpec((s_len, 128), lambda p, b: (b, p)),
            pl.BlockSpec((N_META, 128), lambda p, b: (b, p)),
        ],
        out_shape=[
            jax.ShapeDtypeStruct((r, NA_WIDTH), BF16),
            jax.ShapeDtypeStruct((batch * N_META, NA_WIDTH), BF16),
        ],
        compiler_params=_params(("parallel", "parallel"), est),
        name="na_attn",
    )(proj, proj, proj, proj_m, proj_m, proj_m, bias5, mb4)


def _merge_kernel(ona_ref, omla_ref, wna_ref, wmla_ref, ga_ref, gb_ref, o_ref):
    a = jnp.dot(ona_ref[...], wna_ref[...], preferred_element_type=F32)
    b = jnp.dot(omla_ref[...], wmla_ref[...], preferred_element_type=F32)
    o_ref[...] = (ga_ref[...].astype(F32) * a + gb_ref[...].astype(F32) * b).astype(BF16)


def _merge(o_na, o_mla, w_na, w_mla, proj, *, tm):
    r = o_na.shape[0]
    d = w_na.shape[1]
    tn = 1024 if d % 1024 == 0 else 512
    ga_blk = (3 * NA_WIDTH + 2 * MLA_RANK) // tn
    gb_blk = ga_blk + d // tn
    est = 2 * (tm * (NA_WIDTH + MLA_HEADS * MLA_V) * 2 + (NA_WIDTH + MLA_HEADS * MLA_V) * tn * 2
               + 3 * tm * tn * 2) + 3 * tm * tn * 4
    return pl.pallas_call(
        _merge_kernel,
        grid=(r // tm, d // tn),
        in_specs=[
            pl.BlockSpec((tm, NA_WIDTH), lambda i, j: (i, 0)),
            pl.BlockSpec((tm, MLA_HEADS * MLA_V), lambda i, j: (i, 0)),
            pl.BlockSpec((NA_WIDTH, tn), lambda i, j: (0, j)),
            pl.BlockSpec((MLA_HEADS * MLA_V, tn), lambda i, j: (0, j)),
            pl.BlockSpec((tm, tn), lambda i, j: (i, ga_blk + j)),
            pl.BlockSpec((tm, tn), lambda i, j: (i, gb_blk + j)),
        ],
        out_specs=pl.BlockSpec((tm, tn), lambda i, j: (i, j)),
        out_shape=jax.ShapeDtypeStruct((r, d), BF16),
        compiler_params=_params(("parallel", "arbitrary"), est),
        name="merge",
    )(o_na, o_mla, w_na, w_mla, proj, proj)


def _out_proj_kernel(m_ref, w_ref, x_ref, o_ref):
    o_ref[...] = x_ref[...] + jnp.dot(m_ref[...], w_ref[...], preferred_element_type=F32)


def _out_proj(merged, w_out, x2d, *, tm):
    r, d = x2d.shape
    tn = 1024 if d % 1024 == 0 else 512
    est = 2 * (tm * d * 2 + d * tn * 2 + 2 * tm * tn * 4) + tm * tn * 4
    return pl.pallas_call(
        _out_proj_kernel,
        grid=(r // tm, d // tn),
        in_specs=[
            pl.BlockSpec((tm, d), lambda i, j: (i, 0)),
            pl.BlockSpec((d, tn), lambda i, j: (0, j)),
            pl.BlockSpec((tm, tn), lambda i, j: (i, j)),
        ],
        out_specs=pl.BlockSpec((tm, tn), lambda i, j: (i, j)),
        out_shape=jax.ShapeDtypeStruct((r, d), F32),
        compiler_params=_params(("parallel", "arbitrary"), est),
        name="out_proj",
    )(merged, w_out, x2d)


def _peer_q_kernel(h_ref, g_ref, w_ref, q_ref, hnt_ref, hn_ref):
    @pl.when(pl.program_id(1) == 0)
    def _():
        hn = _rmsnorm(h_ref[...], g_ref[...])
        hn_ref[...] = hn.astype(BF16)
        hnt_ref[...] = hn.T.astype(BF16)

    q_ref[...] = jnp.dot(hn_ref[...], w_ref[...], preferred_element_type=F32).astype(BF16)


def _peer_q(h2, g2, w_q, *, tm):
    r, d = h2.shape
    n = w_q.shape[1]
    tn = 1024
    est = 2 * (tm * d * 4 + d * tn * 2 + tm * tn * 2 + tm * d * 2) + tm * d * 2 + tm * tn * 4 + 2 * tm * d * 4
    return pl.pallas_call(
        _peer_q_kernel,
        grid=(r // tm, n // tn),
        in_specs=[
            pl.BlockSpec((tm, d), lambda i, j: (i, 0)),
            pl.BlockSpec((1, d), lambda i, j: (0, 0)),
            pl.BlockSpec((d, tn), lambda i, j: (0, j)),
        ],
        out_specs=[
            pl.BlockSpec((tm, tn), lambda i, j: (i, j)),
            pl.BlockSpec((d, tm), lambda i, j: (0, i)),
        ],
        out_shape=[jax.ShapeDtypeStruct((r, n), BF16), jax.ShapeDtypeStruct((d, r), BF16)],
        scratch_shapes=[pltpu.VMEM((tm, d), BF16)],
        compiler_params=_params(("parallel", "arbitrary"), est),
        name="peer_q",
    )(h2, g2, w_q)


def _compare_exchange(a, b):
    if a is None:
        return b, None
    if b is None:
        return a, None
    return jnp.maximum(a, b), jnp.minimum(a, b)


def _bitonic_merge(xs):
    n = len(xs)
    step = n // 2
    while step >= 1:
        for i in range(n):
            if i & step == 0:
                xs[i], xs[i + step] = _compare_exchange(xs[i], xs[i + step])
        step //= 2
    return xs


def _bitonic_sort(xs):
    n = len(xs)
    xs = list(xs)
    size = 2
    while size <= n:
        step = size // 2
        while step >= 1:
            for i in range(n):
                if i & step == 0:
                    hi, lo = _compare_exchange(xs[i], xs[i + step])
                    xs[i], xs[i + step] = (hi, lo) if i & size == 0 else (lo, hi)
            step //= 2
        size *= 2
    return xs


def _merge_topk(xs, ys, k):
    xs = list(xs) + [None] * (k - len(xs))
    ys = list(ys) + [None] * (k - len(ys))
    return _bitonic_merge([_compare_exchange(xs[i], ys[k - 1 - i])[0] for i in range(k)])


def _pow2_ceil(n):
    return 1 << (n - 1).bit_length()


def _sorted_topk_slabs(s, k):
    slabs = _bitonic_sort([s[8 * v:8 * v + 8, :] for v in range(s.shape[0] // 8)])
    for shift in (4, 2, 1):
        slabs = _merge_topk(slabs, [pltpu.roll(x, shift, axis=0) for x in slabs], k)
    return slabs


def _extract_topk(x, k):
    n, t = x.shape
    row = lax.broadcasted_iota(jnp.int32, (n, t), 0).astype(F32)
    krow = lax.broadcasted_iota(jnp.int32, (k, t), 0)

    def body(i, carry):
        x, rank, vals = carry
        m = jnp.max(x, axis=0, keepdims=True)
        first = jnp.min(jnp.where(x == m, row, float(n)), axis=0, keepdims=True)
        hit = row == first
        rank = jnp.where(hit, jnp.asarray(i, F32), rank)
        x = jnp.where(hit, _F32_MIN, x)
        vals = jnp.where(krow == i, m, vals)
        return x, rank, vals

    _, rank, vals = lax.fori_loop(
        0, k, body, (x, jnp.full((n, t), float(k), F32), jnp.zeros((k, t), F32)))
    return vals, rank


def _route_head_exact(s1, s2):
    k = PEER_TOPK
    top1, rank1 = _extract_topk(s1, k)
    top2, rank2 = _extract_topk(s2, k)

    tm = s1.shape[1]
    pieces, spans = [], []
    off = 0
    for i in range(k):
        cnt = k // (i + 1)
        rows = -(-cnt // 8) * 8
        jrow = lax.broadcasted_iota(jnp.int32, (rows, tm), 0)
        pieces.append(jnp.where(jrow < cnt, top1[i:i + 1, :] + top2[0:rows, :], _F32_MIN))
        spans.append((off, rows))
        off += rows
    cand = jnp.concatenate(pieces, axis=0)
    ctop, crank = _extract_topk(cand, k)
    chosen = crank < float(k)
    cmax = ctop[0:1, :]
    z = jnp.sum(jnp.where(chosen, jnp.exp(cand - cmax), 0.0), axis=0, keepdims=True)

    y1 = jnp.ones(s1.shape, F32)
    for i, (off, rows) in enumerate(spans):
        n_sel = jnp.sum(jnp.where(chosen[off:off + rows, :], 1.0, 0.0), axis=0, keepdims=True)
        y1 = jnp.where(rank1 == float(i), 1.0 - n_sel, y1)
    return y1, jnp.exp(s1 - top1[0:1, :]) / z, -rank2, jnp.exp(s2 - top2[0:1, :])


def _route_tables_by_value(scores):
    k = PEER_TOPK
    tm = scores[0][0].shape[1]

    sublane = lax.broadcasted_iota(jnp.int32, (8, tm), 0)
    tops = []
    for side in range(2):
        packed = None
        for h in range(PEER_HEADS):
            slabs = _sorted_topk_slabs(scores[h][side], k)
            packed = slabs if packed is None else [jnp.where(sublane == h, s, p) for s, p in zip(slabs, packed)]
        tops.append(packed)
    top1, top2 = tops

    cells = {(i, j): top1[i] + top2[j] for i in range(k) for j in range(k // (i + 1))}
    grid_rows = [[cells[i, j] for j in range(k // (i + 1))] for i in range(k)]
    lists = [r for r in grid_rows[1:] if len(r) > 1] + [[r[0] for r in grid_rows if len(r) == 1]]
    while len(lists) > 1:
        lists.sort(key=len)
        size = min(k, _pow2_ceil(len(lists[0]) + len(lists[1])))
        merged = [m for m in _merge_topk(lists[0], lists[1], size) if m is not None]
        lists = [merged] + lists[2:]
    ctop = _merge_topk(grid_rows[0], lists[0], k)
    thr, cmax = ctop[k - 1], ctop[0]

    taken = {c: v >= thr for c, v in cells.items()}
    n_taken = sum(jnp.where(t, 1.0, 0.0) for t in taken.values())
    z = sum(jnp.where(taken[c], jnp.exp(v - cmax), 0.0) for c, v in cells.items())
    inv_z = 1.0 / z
    y1_rows = [1.0 - sum(jnp.where(taken[i, j], 1.0, 0.0) for j in range(k // (i + 1))) for i in range(k)]

    exact = jnp.where(n_taken == float(k), 1.0, 0.0)
    for top in (top1, top2):
        for i in range(k - 1):
            exact = jnp.where(top[i] > top[i + 1], exact, 0.0)

    tables = []
    for h in range(PEER_HEADS):
        s1, s2 = scores[h]
        row = slice(h, h + 1)
        y1 = jnp.ones(s1.shape, F32)
        for i in range(k):
            y1 = jnp.where(s1 == top1[i][row], y1_rows[i][row], y1)
        y2 = jnp.full(s2.shape, -float(k), F32)
        for j in reversed(range(k)):
            y2 = jnp.where(s2 >= top2[j][row], -float(j), y2)
        tables.append((y1, jnp.exp(s1 - top1[0][row]) * inv_z[row], y2, jnp.exp(s2 - top2[0][row])))
        for s, top in ((s1, top1), (s2, top2)):
            members = jnp.sum(jnp.where(s >= top[k - 1][row], 1.0, 0.0), axis=0, keepdims=True)
            exact = jnp.where(jnp.logical_and(sublane == h, members != float(k)), 0.0, exact)
    return tables, exact


def _peer_route_kernel(q_ref, keys_ref, y1_ref, e1_ref, y2_ref, e2_ref):
    half = PEER_DK // 2
    scores = []
    for h in range(PEER_HEADS):
        q = q_ref[:, h * PEER_DK:(h + 1) * PEER_DK]
        scores.append((_dot_nt(keys_ref[h, 0], q[:, :half]),
                       _dot_nt(keys_ref[h, 1], q[:, half:])))

    def store(h, tables):
        for ref, t in zip((y1_ref, e1_ref, y2_ref, e2_ref), tables):
            ref[h] = t

    tables, exact = _route_tables_by_value(scores)
    for h in range(PEER_HEADS):
        store(h, tables[h])

    for h in range(PEER_HEADS):
        @pl.when(jnp.min(exact[h:h + 1, :]) < 0.5)
        def _(h=h):
            store(h, _route_head_exact(*scores[h]))


def _peer_route(q2, sub_keys, *, tm):
    r = q2.shape[0]
    width = PEER_HEADS * PEER_DK
    tab_spec = pl.BlockSpec((PEER_HEADS, PEER_NKEYS, tm), lambda i: (0, 0, i))
    tab = jax.ShapeDtypeStruct((PEER_HEADS, PEER_NKEYS, r), F32)
    est = (2 * (tm * width * 2 + PEER_HEADS * PEER_NKEYS * PEER_DK * 2 + PEER_HEADS * PEER_NKEYS * tm * 16)
           + 40 * PEER_HEADS * PEER_NKEYS * tm * 4)
    return pl.pallas_call(
        _peer_route_kernel,
        grid=(r // tm,),
        in_specs=[
            pl.BlockSpec((tm, width), lambda i: (i, 0)),
            pl.BlockSpec((PEER_HEADS, 2, PEER_NKEYS, PEER_DK // 2), lambda i: (0, 0, 0, 0)),
        ],
        out_specs=[tab_spec, tab_spec, tab_spec, tab_spec],
        out_shape=[tab, tab, tab, tab],
        compiler_params=_params(("parallel",), est),
        name="peer_route",
    )(q2, sub_keys)


def _peer_dense_kernel(hnt_ref, u_ref, vt_ref, y1_ref, e1_ref, y2_ref, e2_ref, h_ref, g_ref, o_ref,
                       acc_ref, act_ref, p_ref, *, te, final_norm):
    j = pl.program_id(1)
    last = pl.num_programs(1) - 1
    d, tm = acc_ref.shape

    @pl.when(j == 0)
    def _():
        acc_ref[...] = jnp.zeros_like(acc_ref)
        act_ref[...] = jnp.zeros_like(act_ref)
        p_ref[...] = jnp.zeros_like(p_ref)

    a_blocks = te // PEER_NKEYS
    a0 = jnp.clip(j - 1, 0, last - 2) * a_blocks
    d_rows = d // a_blocks
    p_gate = p_ref.at[(j + 1) % 2]
    p_prev = p_ref.at[j % 2]
    for ai in range(a_blocks):
        rows = slice(ai * PEER_NKEYS, (ai + 1) * PEER_NKEYS)
        y1_rows = [y1_ref[h, pl.ds(a0 + ai, 1), :] for h in range(PEER_HEADS)]
        e1_rows = [e1_ref[h, pl.ds(a0 + ai, 1), :] for h in range(PEER_HEADS)]
        for c in range(tm // V7X_LANES):
            cols = slice(c * V7X_LANES, (c + 1) * V7X_LANES)
            gate = None
            for h in range(PEER_HEADS):
                sel = y2_ref[h, :, cols] >= y1_rows[h][:, cols]
                term = jnp.where(sel, e1_rows[h][:, cols] * e2_ref[h, :, cols], 0.0)
                gate = term if gate is None else gate + term
            p_gate[rows, cols] = (gate * act_ref[rows, cols]).astype(BF16)
        act_ref[rows, :] = jax.nn.gelu(jnp.dot(u_ref[rows, :], hnt_ref[...], preferred_element_type=F32))
        drows = slice(ai * d_rows, (ai + 1) * d_rows)
        acc_ref[drows, :] += jnp.dot(vt_ref[drows, :], p_prev[...], preferred_element_type=F32)

    @pl.when(j == last)
    def _():
        out = h_ref[...] + acc_ref[...].T
        o_ref[...] = _rmsnorm(out, g_ref[...]) if final_norm else out


def _peer_dense(hn2t, u, vt, tabs, h2, gf, *, tm, te, final_norm):
    d, r = hn2t.shape
    e = u.shape[0]
    y1, e1, y2, e2 = tabs
    n_blocks = e // te
    tab_spec = pl.BlockSpec((PEER_HEADS, PEER_NKEYS, tm), lambda i, j: (0, 0, i))
    est = (2 * (tm * d * 2 + 2 * te * d * 2 + PEER_HEADS * PEER_NKEYS * tm * 16 + tm * d * 4)
           + tm * d * 4 + d * tm * 4 + 6 * te * tm * 4)
    return pl.pallas_call(
        functools.partial(_peer_dense_kernel, te=te, final_norm=final_norm),
        grid=(r // tm, n_blocks + 2),
        in_specs=[
            pl.BlockSpec((d, tm), lambda i, j: (0, i)),
            pl.BlockSpec((te, d), lambda i, j: (jnp.minimum(j, n_blocks - 1), 0)),
            pl.BlockSpec((d, te), lambda i, j: (0, jnp.maximum(j - 2, 0))),
            tab_spec, tab_spec, tab_spec, tab_spec,
            pl.BlockSpec((tm, d), lambda i, j: (i, 0), pipeline_mode=pl.Buffered(1)),
            pl.BlockSpec((1, d), lambda i, j: (0, 0)),
        ],
        out_specs=pl.BlockSpec((tm, d), lambda i, j: (i, 0)),
        out_shape=jax.ShapeDtypeStruct((r, d), F32),
        scratch_shapes=[pltpu.VMEM((d, tm), F32), pltpu.VMEM((te, tm), F32), pltpu.VMEM((2, te, tm), BF16)],
        compiler_params=_params(("parallel", "arbitrary"), est),
        name="peer_dense",
    )(hn2t, u, vt, y1, e1, y2, e2, h2, gf)


def _rope_tables(pos):
    inv_freq = ROPE_THETA ** (-jnp.arange(0, MLA_ROPE, 2, dtype=F32) / MLA_ROPE)
    ang = pos.astype(F32)[:, None] * inv_freq[None, :]
    cos, sin = jnp.cos(ang), jnp.sin(ang)
    zeros = jnp.zeros((pos.shape[0], V7X_LANES - MLA_ROPE), F32)
    return (jnp.concatenate([cos, cos, zeros], axis=1), jnp.concatenate([-sin, sin, zeros], axis=1))


def _swap_halves(w):
    half = w.shape[-1] // 2
    return jnp.concatenate([w[..., half:], w[..., :half]], axis=-1)


def _layer(x_real, x_meta, p, l, *, batch):
    d = x_real.shape[1]
    s_len = x_real.shape[0] // batch
    tm_real = _row_tile(x_real.shape[0], 1024)
    tm_meta = x_meta.shape[0]

    w_in = p["w_in"][l]
    kr0 = 3 * NA_WIDTH + 2 * MLA_RANK
    w_main = jnp.concatenate([w_in[:, :kr0], w_in[:, kr0 + MLA_ROPE:]], axis=1).astype(BF16)
    w_kr = w_in[:, kr0:kr0 + MLA_ROPE]
    w_kr = jnp.concatenate([w_kr, _swap_halves(w_kr)], axis=1).astype(BF16)
    norm_g = jnp.concatenate([p["mla_q_norm_g"][l], p["mla_kv_norm_g"][l]])[None].astype(F32)
    g1 = p["norm1_g"][l][None].astype(F32)
    wq = p["mla_w_uq"][l].reshape(MLA_RANK, MLA_HEADS, MLA_NOPE + MLA_ROPE)
    wq = jnp.concatenate([wq, _swap_halves(wq[..., MLA_NOPE:])], axis=-1)
    wq = wq.transpose(1, 0, 2).astype(BF16)
    wkv = p["mla_w_ukv"][l].reshape(MLA_RANK, MLA_HEADS, MLA_NOPE + MLA_V).transpose(1, 0, 2).astype(BF16)
    cos_r, sin_r = _rope_tables(N_META + jnp.arange(s_len))
    cos_m, sin_m = _rope_tables(jnp.tile(jnp.arange(N_META), batch))
    bias_tab = _na_bias_tables(p["na_rel_bias"][l], s_len // GRID_W)

    proj_r, kr_r = _in_proj(x_real, g1, w_main, w_kr, norm_g, tm=tm_real)
    proj_m, kr_m = _in_proj(x_meta, g1, w_main, w_kr, norm_g, tm=tm_meta)
    q_r, k_r, v_r = _mla_up(proj_r, kr_r, cos_r, sin_r, wq, wkv, tm=_row_tile(s_len, 1024))
    q_m, k_m, v_m = _mla_up(proj_m, kr_m, cos_m, sin_m, wq, wkv, tm=tm_meta)
    omla_r, omla_m = _mla_attn(q_r, k_r, v_r, q_m, k_m, v_m, batch=batch)
    ona_r, ona_m = _na_attn(proj_r, proj_m, bias_tab, p["na_meta_bias"][l], batch=batch)

    w_na = p["w_na_branch"][l].astype(BF16)
    w_mla = p["w_mla_branch"][l].astype(BF16)
    w_out = p["w_out"][l].astype(BF16)
    h_real = _out_proj(_merge(ona_r, omla_r, w_na, w_mla, proj_r, tm=tm_real), w_out, x_real, tm=tm_real)
    h_meta = _out_proj(_merge(ona_m, omla_m, w_na, w_mla, proj_m, tm=tm_meta), w_out, x_meta, tm=tm_meta)
    return h_real, h_meta


def _peer(h2, p, l, gf, *, tm, te, final_norm):
    g2 = p["norm2_g"][l][None].astype(F32)
    w_q = p["peer_w_q"][l].astype(BF16)
    keys = p["peer_sub_keys"][l].astype(BF16)
    u = p["peer_u"][l].astype(BF16)
    vt = p["peer_v"][l].astype(BF16).T
    q2, hn2t = _peer_q(h2, g2, w_q, tm=_row_tile(h2.shape[0], 1024))
    tabs = _peer_route(q2, keys, tm=min(256, h2.shape[0]))
    return _peer_dense(hn2t, u, vt, tabs, h2, gf, tm=tm, te=te, final_norm=final_norm)


def kernel(x, meta_tokens, norm1_g, w_in, na_rel_bias, na_meta_bias, mla_q_norm_g, mla_w_uq, mla_kv_norm_g,
           mla_w_ukv, w_na_branch, w_mla_branch, w_out, norm2_g, peer_w_q, peer_sub_keys, peer_u, peer_v,
           final_norm_g):
    b, s, d = x.shape
    depth = w_in.shape[0]
    p = dict(norm1_g=norm1_g, w_in=w_in, na_rel_bias=na_rel_bias, na_meta_bias=na_meta_bias,
             mla_q_norm_g=mla_q_norm_g, mla_w_uq=mla_w_uq, mla_kv_norm_g=mla_kv_norm_g, mla_w_ukv=mla_w_ukv,
             w_na_branch=w_na_branch, w_mla_branch=w_mla_branch, w_out=w_out, norm2_g=norm2_g,
             peer_w_q=peer_w_q, peer_sub_keys=peer_sub_keys, peer_u=peer_u, peer_v=peer_v)
    x_real = x.reshape(b * s, d)
    x_meta = jnp.broadcast_to(meta_tokens.astype(x.dtype)[None], (b, N_META, d)).reshape(b * N_META, d)
    gf = final_norm_g[None].astype(F32)
    tm = _row_tile(b * s, 512)
    te = 512
    for l in range(depth):
        h_real, h_meta = _layer(x_real, x_meta, p, l, batch=b)
        x_real = _peer(h_real, p, l, gf, tm=tm, te=te, final_norm=l == depth - 1)
        x_meta = _peer(h_meta, p, l, gf, tm=h_meta.shape[0], te=te, final_norm=False)
    return x_real.reshape(b, s, d)
```

```python
import functools

import jax
import jax.numpy as jnp
import numpy as np
from jax import lax
from jax.experimental import pallas as pl
from jax.experimental.pallas import tpu as pltpu

GRID_W = 64
N_META = 16
NA_HEADS = 16
NA_HEAD_DIM = 64
NA_WIN_H = 8
NA_WIN_W = 16
NA_WIDTH = NA_HEADS * NA_HEAD_DIM
MLA_HEADS = 16
MLA_RANK = 512
MLA_NOPE = 128
MLA_ROPE = 64
MLA_V = 128
MLA_QK_PAD = 256
ROPE_THETA = 10000.0
PEER_HEADS = 8
PEER_NKEYS = 128
PEER_DK = 256
PEER_TOPK = 16
NORM_EPS = 1e-6
NEG_INF = -1e30

V7X_VMEM_BYTES = 64 * 1024 * 1024
V7X_VMEM_HEADROOM = 6 * 1024 * 1024
V7X_LANES = 128
BF16_ROWS = 16

NA_QROWS = 4
NA_KROWS = NA_QROWS + NA_WIN_H

F32 = jnp.float32
BF16 = jnp.bfloat16
_F32_MIN = float(np.finfo(np.float32).min)


def _vmem_limit(estimate_bytes):
    want = int(estimate_bytes * 1.25) + 4 * 1024 * 1024
    return min(max(want, 16 * 1024 * 1024), V7X_VMEM_BYTES - V7X_VMEM_HEADROOM)


def _params(semantics, vmem_estimate, flags=None):
    return pltpu.CompilerParams(dimension_semantics=semantics,
                                vmem_limit_bytes=_vmem_limit(vmem_estimate), flags=flags)


def _row_tile(rows, want):
    tile = min(rows, want)
    assert rows % tile == 0, (rows, tile)
    return tile


def _dot_nt(a, b):
    return lax.dot_general(a, b, (((1,), (1,)), ((), ())), preferred_element_type=F32)


def _rmsnorm(x, g):
    return x * lax.rsqrt(jnp.mean(x * x, axis=-1, keepdims=True) + NORM_EPS) * g


def _pad_meta_rows(a):
    return jnp.concatenate([a, jnp.zeros((V7X_LANES - N_META, a.shape[1]), a.dtype)], axis=0)


def _in_proj_kernel(x_ref, g_ref, w_ref, wkr_ref, ng_ref, o_ref, kr_ref, hn_ref, *, tn):
    j = pl.program_id(1)

    @pl.when(j == 0)
    def _():
        hn = _rmsnorm(x_ref[...], g_ref[...]).astype(BF16)
        hn_ref[...] = hn
        kr_ref[...] = jnp.dot(hn, wkr_ref[...], preferred_element_type=F32)

    acc = jnp.dot(hn_ref[...], w_ref[...], preferred_element_type=F32)
    col0 = j * tn
    plain_end = 3 * NA_WIDTH
    norm_end = plain_end + 2 * MLA_RANK

    @pl.when(col0 < plain_end)
    def _():
        scale = jnp.where(col0 < NA_WIDTH, NA_HEAD_DIM ** -0.5, 1.0).astype(F32)
        o_ref[...] = (acc * scale).astype(BF16)

    @pl.when(jnp.logical_and(col0 >= plain_end, col0 < norm_end))
    def _():
        for c in range(tn // MLA_RANK):
            sl = slice(c * MLA_RANK, (c + 1) * MLA_RANK)
            o_ref[:, sl] = _rmsnorm(acc[:, sl], ng_ref[:, sl]).astype(BF16)

    @pl.when(col0 >= norm_end)
    def _():
        o_ref[...] = jax.nn.sigmoid(acc).astype(BF16)


def _in_proj(x2d, g1, w_main, w_kr, norm_g, *, tm):
    r, d = x2d.shape
    n = w_main.shape[1]
    tn = 1024 if d % 1024 == 0 else 512
    norm_blk0 = (3 * NA_WIDTH) // tn
    norm_nblk = (2 * MLA_RANK) // tn
    est = 2 * (tm * d * 4 + d * tn * 2 + tm * tn * 2 + tm * 128 * 4 + d * 128 * 2) + tm * d * 2 + tm * tn * 4
    return pl.pallas_call(
        functools.partial(_in_proj_kernel, tn=tn),
        grid=(r // tm, n // tn),
        in_specs=[
            pl.BlockSpec((tm, d), lambda i, j: (i, 0)),
            pl.BlockSpec((1, d), lambda i, j: (0, 0)),
            pl.BlockSpec((d, tn), lambda i, j: (0, j)),
            pl.BlockSpec((d, 128), lambda i, j: (0, 0)),
            pl.BlockSpec((1, tn), lambda i, j: (0, jnp.clip(j - norm_blk0, 0, norm_nblk - 1))),
        ],
        out_specs=[
            pl.BlockSpec((tm, tn), lambda i, j: (i, j)),
            pl.BlockSpec((tm, 128), lambda i, j: (i, 0)),
        ],
        out_shape=[jax.ShapeDtypeStruct((r, n), BF16), jax.ShapeDtypeStruct((r, 128), F32)],
        scratch_shapes=[pltpu.VMEM((tm, d), BF16)],
        compiler_params=_params(("parallel", "arbitrary"), est),
        name="in_proj",
    )(x2d, g1, w_main, w_kr, norm_g)


def _rope_half(t, cos, sin):
    return t * cos + pltpu.roll(t, 64, axis=1) * sin


def _mla_up_kernel(cq_ref, ckv_ref, kr_ref, cos_ref, sin_ref, wq_ref, wkv_ref, q_ref, k_ref, v_ref):
    cos = cos_ref[...]
    sin = sin_ref[...]
    scale = (MLA_NOPE + MLA_ROPE) ** -0.5
    yq = jnp.dot(cq_ref[...], wq_ref[0], preferred_element_type=F32)
    q_ref[0, :, 0:MLA_NOPE] = (yq[:, 0:MLA_NOPE] * scale).astype(BF16)
    q_ref[0, :, MLA_NOPE:] = (_rope_half(yq[:, MLA_NOPE:], cos, sin) * scale).astype(BF16)
    ykv = jnp.dot(ckv_ref[...], wkv_ref[0], preferred_element_type=F32)
    k_ref[0, :, 0:MLA_NOPE] = ykv[:, 0:MLA_NOPE].astype(BF16)
    k_ref[0, :, MLA_NOPE:] = _rope_half(kr_ref[...], cos, sin).astype(BF16)
    v_ref[0] = ykv[:, MLA_NOPE:].astype(BF16)


def _mla_up(proj, kr, cos_t, sin_t, wq_h, wkv_h, *, tm):
    r = proj.shape[0]
    cq_blk = (3 * NA_WIDTH) // MLA_RANK
    n_pos_blk = cos_t.shape[0] // tm
    est = 2 * (2 * tm * MLA_RANK * 2 + 3 * tm * 128 * 4 + 2 * MLA_RANK * 256 * 2
               + 2 * tm * 256 * 2 + tm * 128 * 2) + 4 * tm * 256 * 4
    return pl.pallas_call(
        _mla_up_kernel,
        grid=(r // tm, MLA_HEADS),
        in_specs=[
            pl.BlockSpec((tm, MLA_RANK), lambda i, h: (i, cq_blk)),
            pl.BlockSpec((tm, MLA_RANK), lambda i, h: (i, cq_blk + 1)),
            pl.BlockSpec((tm, 128), lambda i, h: (i, 0)),
            pl.BlockSpec((tm, 128), lambda i, h: (i % n_pos_blk, 0)),
            pl.BlockSpec((tm, 128), lambda i, h: (i % n_pos_blk, 0)),
            pl.BlockSpec((1, MLA_RANK, 256), lambda i, h: (h, 0, 0)),
            pl.BlockSpec((1, MLA_RANK, 256), lambda i, h: (h, 0, 0)),
        ],
        out_specs=[
            pl.BlockSpec((1, tm, MLA_QK_PAD), lambda i, h: (h, i, 0)),
            pl.BlockSpec((1, tm, MLA_QK_PAD), lambda i, h: (h, i, 0)),
            pl.BlockSpec((1, tm, MLA_V), lambda i, h: (h, i, 0)),
        ],
        out_shape=[
            jax.ShapeDtypeStruct((MLA_HEADS, r, MLA_QK_PAD), BF16),
            jax.ShapeDtypeStruct((MLA_HEADS, r, MLA_QK_PAD), BF16),
            jax.ShapeDtypeStruct((MLA_HEADS, r, MLA_V), BF16),
        ],
        compiler_params=_params(("parallel", "arbitrary"), est),
        name="mla_up",
    )(proj, proj, kr, cos_t, sin_t, wq_h, wkv_h)


def _mla_attn_kernel(q_ref, k_ref, v_ref, qm_ref, km_ref, vm_ref, o_ref, om_ref, *, tq):
    k = k_ref[0]
    v = v_ref[0]
    km = _pad_meta_rows(km_ref[0])
    vm = _pad_meta_rows(vm_ref[0])
    meta_lane = lax.broadcasted_iota(jnp.int32, (1, V7X_LANES), 1) < N_META

    def attend(q):
        s = _dot_nt(q, k)
        sm = jnp.where(meta_lane, _dot_nt(q, km), NEG_INF)
        m = jnp.maximum(jnp.max(s, axis=-1, keepdims=True), jnp.max(sm, axis=-1, keepdims=True))
        p = jnp.exp(s - m)
        pm = jnp.exp(sm - m)
        l = jnp.sum(p, axis=-1, keepdims=True) + jnp.sum(pm, axis=-1, keepdims=True)
        o = (jnp.dot(p.astype(BF16), v, preferred_element_type=F32)
             + jnp.dot(pm.astype(BF16), vm, preferred_element_type=F32))
        return o / l

    s_len = q_ref.shape[1]
    for c in range(s_len // tq):
        rows = slice(c * tq, (c + 1) * tq)
        o_ref[rows, :] = attend(q_ref[0, rows, :]).astype(BF16)
    om_ref[...] = attend(qm_ref[0]).astype(BF16)


def _mla_attn(q, k, v, qm, km, vm, *, batch):
    r = q.shape[1]
    s_len = r // batch
    tq = min(512, s_len)
    est = 2 * (2 * s_len * 256 * 2 + 2 * s_len * 128 * 2) + 3 * tq * s_len * 4 + tq * s_len * 2
    return pl.pallas_call(
        functools.partial(_mla_attn_kernel, tq=tq),
        grid=(batch, MLA_HEADS),
        in_specs=[
            pl.BlockSpec((1, s_len, MLA_QK_PAD), lambda b, h: (h, b, 0)),
            pl.BlockSpec((1, s_len, MLA_QK_PAD), lambda b, h: (h, b, 0)),
            pl.BlockSpec((1, s_len, MLA_V), lambda b, h: (h, b, 0)),
            pl.BlockSpec((1, N_META, MLA_QK_PAD), lambda b, h: (h, b, 0)),
            pl.BlockSpec((1, N_META, MLA_QK_PAD), lambda b, h: (h, b, 0)),
            pl.BlockSpec((1, N_META, MLA_V), lambda b, h: (h, b, 0)),
        ],
        out_specs=[
            pl.BlockSpec((s_len, MLA_V), lambda b, h: (b, h)),
            pl.BlockSpec((N_META, MLA_V), lambda b, h: (b, h)),
        ],
        out_shape=[
            jax.ShapeDtypeStruct((r, MLA_HEADS * MLA_V), BF16),
            jax.ShapeDtypeStruct((batch * N_META, MLA_HEADS * MLA_V), BF16),
        ],
        compiler_params=_params(("parallel", "parallel"), est),
        name="mla_attn",
    )(q, k, v, qm, km, vm)


def _na_bias_tiles(rel_bias):
    qc = np.arange(GRID_W)
    kc = np.arange(GRID_W)
    cstart = np.clip(qc - NA_WIN_W // 2, 0, GRID_W - NA_WIN_W)
    col_ok = (kc[None, :] >= cstart[:, None]) & (kc[None, :] < cstart[:, None] + NA_WIN_W)
    col_off = np.clip(kc[None, :] - qc[:, None], -(NA_WIN_W - 1), NA_WIN_W - 1) + NA_WIN_W - 1
    n_co = 2 * NA_WIN_W - 1
    onehot = np.zeros((n_co, GRID_W * GRID_W), np.float32)
    onehot[col_off.reshape(-1), np.arange(GRID_W * GRID_W)] = 1.0
    h, n_ro = rel_bias.shape[0], rel_bias.shape[1]
    tiles = jnp.dot(rel_bias.astype(F32).reshape(h * n_ro, n_co), jnp.asarray(onehot),
                    precision=lax.Precision.HIGHEST).reshape(h, n_ro, GRID_W, GRID_W)
    return jnp.where(jnp.asarray(col_ok)[None, None], tiles, NEG_INF)


def _na_block_plan(rows):
    nqb = rows // NA_QROWS
    wh = NA_WIN_H
    plan = []
    for qb in (0, 1, nqb - 1):
        kstart = int(np.clip(NA_QROWS * qb - wh // 2, 0, rows - NA_KROWS))
        variant = []
        for i in range(NA_QROWS):
            r = NA_QROWS * qb + i
            rs = int(np.clip(r - wh // 2, 0, rows - wh))
            variant.append([kstart + j - r + wh - 1 if rs <= kstart + j < rs + wh else None
                            for j in range(NA_KROWS)])
        plan.append(variant)
    return plan


def _na_attn_kernel(q_ref, k_ref, v_ref, qm_ref, km_ref, vm_ref, tiles_ref, mb_ref, o_ref, om_ref, bias_ref,
                    *, rows):
    nqb = rows // NA_QROWS
    qblk = NA_QROWS * GRID_W
    kblk = NA_KROWS * GRID_W

    @pl.when(pl.program_id(1) == 0)
    def _():
        masked = jnp.full((GRID_W, GRID_W), NEG_INF, F32)
        for hh in range(2):
            for v, variant in enumerate(_na_block_plan(rows)):
                for i, row_plan in enumerate(variant):
                    for j0 in range(0, NA_KROWS, 2):
                        pair = [masked if ro is None else tiles_ref[0, hh, ro] for ro in row_plan[j0:j0 + 2]]
                        bias_ref[hh, v, i * GRID_W:(i + 1) * GRID_W, j0 * GRID_W:(j0 + 2) * GRID_W] = (
                            jnp.concatenate(pair, axis=1))

    km = _pad_meta_rows(km_ref[...])
    vm = _pad_meta_rows(vm_ref[...])
    lane = lax.broadcasted_iota(jnp.int32, (1, 2 * NA_HEAD_DIM), 1)
    head_lanes = [lane < NA_HEAD_DIM, lane >= NA_HEAD_DIM]

    def softmax_pv(s, sm, v_loc):
        m = jnp.maximum(jnp.max(s, axis=-1, keepdims=True), jnp.max(sm, axis=-1, keepdims=True))
        p = jnp.exp(s - m)
        pm = jnp.exp(sm - m)
        l = jnp.sum(p, axis=-1, keepdims=True) + jnp.sum(pm, axis=-1, keepdims=True)
        o = (jnp.dot(p.astype(BF16), v_loc, preferred_element_type=F32)
             + jnp.dot(pm.astype(BF16), vm, preferred_element_type=F32))
        return o / l

    def block(qb, carry):
        q0 = pl.multiple_of(qb * qblk, qblk)
        k0 = pl.multiple_of(jnp.clip(NA_QROWS * qb - NA_WIN_H // 2, 0, rows - NA_KROWS) * GRID_W, GRID_W)
        variant = jnp.where(qb == 0, 0, jnp.where(qb == nqb - 1, 2, 1))
        q = q_ref[pl.ds(q0, qblk), :]
        k_loc = k_ref[pl.ds(k0, kblk), :]
        v_loc = v_ref[pl.ds(k0, kblk), :]
        outs = []
        for hh in range(2):
            qh = jnp.where(head_lanes[hh], q, jnp.zeros_like(q))
            s = _dot_nt(qh, k_loc) + bias_ref[hh, variant]
            sm = _dot_nt(qh, km) + mb_ref[0, hh]
            outs.append(softmax_pv(s, sm, v_loc))
        o_ref[pl.ds(q0, qblk), :] = jnp.where(head_lanes[0], outs[0], outs[1]).astype(BF16)
        return carry

    lax.fori_loop(0, nqb, block, 0)

    qm = qm_ref[...]
    outs = []
    for hh in range(2):
        qh = jnp.where(head_lanes[hh], qm, jnp.zeros_like(qm))
        sm = _dot_nt(qh, km) + mb_ref[0, hh]
        m = jnp.max(sm, axis=-1, keepdims=True)
        pm = jnp.exp(sm - m)
        l = jnp.sum(pm, axis=-1, keepdims=True)
        outs.append(jnp.dot(pm.astype(BF16), vm, preferred_element_type=F32) / l)
    om_ref[...] = jnp.where(head_lanes[0], outs[0], outs[1]).astype(BF16)


def _na_attn(proj, proj_m, bias_tiles, meta_bias, *, batch):
    r = proj.shape[0]
    s_len = r // batch
    rows = s_len // GRID_W
    assert rows % NA_QROWS == 0 and rows >= NA_KROWS, rows
    npair = NA_HEADS // 2
    qblk = NA_QROWS * GRID_W
    kblk = NA_KROWS * GRID_W
    n_ro = bias_tiles.shape[1]
    tiles5 = bias_tiles.reshape(npair, 2, n_ro, GRID_W, GRID_W)
    mb4 = jnp.concatenate([meta_bias.astype(F32), jnp.full((NA_HEADS, V7X_LANES - N_META), NEG_INF, F32)],
                          axis=1).reshape(npair, 2, 1, V7X_LANES)
    est = 2 * (3 * s_len * 128 * 2 + 6 * qblk * kblk * 4 + s_len * 128 * 2) + 6 * qblk * kblk * 4
    return pl.pallas_call(
        functools.partial(_na_attn_kernel, rows=rows),
        grid=(npair, batch),
        in_specs=[
            pl.BlockSpec((s_len, 128), lambda p, b: (b, p)),
            pl.BlockSpec((s_len, 128), lambda p, b: (b, npair + p)),
            pl.BlockSpec((s_len, 128), lambda p, b: (b, 2 * npair + p)),
            pl.BlockSpec((N_META, 128), lambda p, b: (b, p)),
            pl.BlockSpec((N_META, 128), lambda p, b: (b, npair + p)),
            pl.BlockSpec((N_META, 128), lambda p, b: (b, 2 * npair + p)),
            pl.BlockSpec((1, 2, n_ro, GRID_W, GRID_W), lambda p, b: (p, 0, 0, 0, 0)),
            pl.BlockSpec((1, 2, 1, V7X_LANES), lambda p, b: (p, 0, 0, 0)),
        ],
        out_specs=[
            pl.BlockSpec((s_len, 128), lambda p, b: (b, p)),
            pl.BlockSpec((N_META, 128), lambda p, b: (b, p)),
        ],
        out_shape=[
            jax.ShapeDtypeStruct((r, NA_WIDTH), BF16),
            jax.ShapeDtypeStruct((batch * N_META, NA_WIDTH), BF16),
        ],
        scratch_shapes=[pltpu.VMEM((2, 3, qblk, kblk), F32)],
        compiler_params=_params(("parallel", "arbitrary"), est),
        name="na_attn",
    )(proj, proj, proj, proj_m, proj_m, proj_m, tiles5, mb4)


def _merge_kernel(ona_ref, omla_ref, wna_ref, wmla_ref, ga_ref, gb_ref, o_ref):
    a = jnp.dot(ona_ref[...], wna_ref[...], preferred_element_type=F32)
    b = jnp.dot(omla_ref[...], wmla_ref[...], preferred_element_type=F32)
    o_ref[...] = (ga_ref[...].astype(F32) * a + gb_ref[...].astype(F32) * b).astype(BF16)


def _merge(o_na, o_mla, w_na, w_mla, proj, *, tm):
    r = o_na.shape[0]
    d = w_na.shape[1]
    tn = 1024 if d % 1024 == 0 else 512
    ga_blk = (3 * NA_WIDTH + 2 * MLA_RANK) // tn
    gb_blk = ga_blk + d // tn
    est = 2 * (tm * (NA_WIDTH + MLA_HEADS * MLA_V) * 2 + (NA_WIDTH + MLA_HEADS * MLA_V) * tn * 2
               + 3 * tm * tn * 2) + 3 * tm * tn * 4
    return pl.pallas_call(
        _merge_kernel,
        grid=(r // tm, d // tn),
        in_specs=[
            pl.BlockSpec((tm, NA_WIDTH), lambda i, j: (i, 0)),
            pl.BlockSpec((tm, MLA_HEADS * MLA_V), lambda i, j: (i, 0)),
            pl.BlockSpec((NA_WIDTH, tn), lambda i, j: (0, j)),
            pl.BlockSpec((MLA_HEADS * MLA_V, tn), lambda i, j: (0, j)),
            pl.BlockSpec((tm, tn), lambda i, j: (i, ga_blk + j)),
            pl.BlockSpec((tm, tn), lambda i, j: (i, gb_blk + j)),
        ],
        out_specs=pl.BlockSpec((tm, tn), lambda i, j: (i, j)),
        out_shape=jax.ShapeDtypeStruct((r, d), BF16),
        compiler_params=_params(("parallel", "arbitrary"), est),
        name="merge",
    )(o_na, o_mla, w_na, w_mla, proj, proj)


def _out_proj_kernel(m_ref, w_ref, x_ref, o_ref):
    o_ref[...] = x_ref[...] + jnp.dot(m_ref[...], w_ref[...], preferred_element_type=F32)


def _out_proj(merged, w_out, x2d, *, tm):
    r, d = x2d.shape
    tn = 1024 if d % 1024 == 0 else 512
    est = 2 * (tm * d * 2 + d * tn * 2 + 2 * tm * tn * 4) + tm * tn * 4
    return pl.pallas_call(
        _out_proj_kernel,
        grid=(r // tm, d // tn),
        in_specs=[
            pl.BlockSpec((tm, d), lambda i, j: (i, 0)),
            pl.BlockSpec((d, tn), lambda i, j: (0, j)),
            pl.BlockSpec((tm, tn), lambda i, j: (i, j)),
        ],
        out_specs=pl.BlockSpec((tm, tn), lambda i, j: (i, j)),
        out_shape=jax.ShapeDtypeStruct((r, d), F32),
        compiler_params=_params(("parallel", "arbitrary"), est),
        name="out_proj",
    )(merged, w_out, x2d)


def _peer_q_kernel(h_ref, g_ref, w_ref, q_ref, hnt_ref, hn_ref):
    @pl.when(pl.program_id(1) == 0)
    def _():
        hn = _rmsnorm(h_ref[...], g_ref[...])
        hn_ref[...] = hn.astype(BF16)
        hnt_ref[...] = hn.T.astype(BF16)

    q_ref[...] = jnp.dot(hn_ref[...], w_ref[...], preferred_element_type=F32).astype(BF16)


def _peer_q(h2, g2, w_q, *, tm):
    r, d = h2.shape
    n = w_q.shape[1]
    tn = 1024
    est = 2 * (tm * d * 4 + d * tn * 2 + tm * tn * 2 + tm * d * 2) + tm * d * 2 + tm * tn * 4 + 2 * tm * d * 4
    return pl.pallas_call(
        _peer_q_kernel,
        grid=(r // tm, n // tn),
        in_specs=[
            pl.BlockSpec((tm, d), lambda i, j: (i, 0)),
            pl.BlockSpec((1, d), lambda i, j: (0, 0)),
            pl.BlockSpec((d, tn), lambda i, j: (0, j)),
        ],
        out_specs=[
            pl.BlockSpec((tm, tn), lambda i, j: (i, j)),
            pl.BlockSpec((d, tm), lambda i, j: (0, i)),
        ],
        out_shape=[jax.ShapeDtypeStruct((r, n), BF16), jax.ShapeDtypeStruct((d, r), BF16)],
        scratch_shapes=[pltpu.VMEM((tm, d), BF16)],
        compiler_params=_params(("parallel", "arbitrary"), est),
        name="peer_q",
    )(h2, g2, w_q)


def _compare_exchange(a, b):
    if a is None:
        return b, None
    if b is None:
        return a, None
    return jnp.maximum(a, b), jnp.minimum(a, b)


def _bitonic_merge(xs):
    n = len(xs)
    step = n // 2
    while step >= 1:
        for i in range(n):
            if i & step == 0:
                xs[i], xs[i + step] = _compare_exchange(xs[i], xs[i + step])
        step //= 2
    return xs


def _bitonic_sort(xs):
    n = len(xs)
    xs = list(xs)
    size = 2
    while size <= n:
        step = size // 2
        while step >= 1:
            for i in range(n):
                if i & step == 0:
                    hi, lo = _compare_exchange(xs[i], xs[i + step])
                    xs[i], xs[i + step] = (hi, lo) if i & size == 0 else (lo, hi)
            step //= 2
        size *= 2
    return xs


def _merge_topk(xs, ys, k):
    xs = list(xs) + [None] * (k - len(xs))
    ys = list(ys) + [None] * (k - len(ys))
    return _bitonic_merge([_compare_exchange(xs[i], ys[k - 1 - i])[0] for i in range(k)])


def _pow2_ceil(n):
    return 1 << (n - 1).bit_length()


def _sorted_topk_slabs(s, k):
    slabs = _bitonic_sort([s[8 * v:8 * v + 8, :] for v in range(s.shape[0] // 8)])
    for shift in (4, 2, 1):
        slabs = _merge_topk(slabs, [pltpu.roll(x, shift, axis=0) for x in slabs], k)
    return slabs


def _extract_topk(x, k):
    n, t = x.shape
    row = lax.broadcasted_iota(jnp.int32, (n, t), 0).astype(F32)
    krow = lax.broadcasted_iota(jnp.int32, (k, t), 0)

    def body(i, carry):
        x, rank, vals = carry
        m = jnp.max(x, axis=0, keepdims=True)
        first = jnp.min(jnp.where(x == m, row, float(n)), axis=0, keepdims=True)
        hit = row == first
        rank = jnp.where(hit, jnp.asarray(i, F32), rank)
        x = jnp.where(hit, _F32_MIN, x)
        vals = jnp.where(krow == i, m, vals)
        return x, rank, vals

    _, rank, vals = lax.fori_loop(
        0, k, body, (x, jnp.full((n, t), float(k), F32), jnp.zeros((k, t), F32)))
    return vals, rank


def _route_head_exact(s1, s2):
    k = PEER_TOPK
    top1, rank1 = _extract_topk(s1, k)
    top2, rank2 = _extract_topk(s2, k)

    tm = s1.shape[1]
    pieces, spans = [], []
    off = 0
    for i in range(k):
        cnt = k // (i + 1)
        rows = -(-cnt // 8) * 8
        jrow = lax.broadcasted_iota(jnp.int32, (rows, tm), 0)
        pieces.append(jnp.where(jrow < cnt, top1[i:i + 1, :] + top2[0:rows, :], _F32_MIN))
        spans.append((off, rows))
        off += rows
    cand = jnp.concatenate(pieces, axis=0)
    ctop, crank = _extract_topk(cand, k)
    chosen = crank < float(k)
    cmax = ctop[0:1, :]
    z = jnp.sum(jnp.where(chosen, jnp.exp(cand - cmax), 0.0), axis=0, keepdims=True)

    y1 = jnp.ones(s1.shape, F32)
    for i, (off, rows) in enumerate(spans):
        n_sel = jnp.sum(jnp.where(chosen[off:off + rows, :], 1.0, 0.0), axis=0, keepdims=True)
        y1 = jnp.where(rank1 == float(i), 1.0 - n_sel, y1)
    return y1, jnp.exp(s1 - top1[0:1, :]) / z, -rank2, jnp.exp(s2 - top2[0:1, :])


def _route_tables_by_value(scores):
    k = PEER_TOPK
    tm = scores[0][0].shape[1]

    sublane = lax.broadcasted_iota(jnp.int32, (8, tm), 0)
    tops = []
    for side in range(2):
        packed = None
        for h in range(PEER_HEADS):
            slabs = _sorted_topk_slabs(scores[h][side], k)
            packed = slabs if packed is None else [jnp.where(sublane == h, s, p) for s, p in zip(slabs, packed)]
        tops.append(packed)
    top1, top2 = tops

    cells = {(i, j): top1[i] + top2[j] for i in range(k) for j in range(k // (i + 1))}
    grid_rows = [[cells[i, j] for j in range(k // (i + 1))] for i in range(k)]
    lists = [r for r in grid_rows[1:] if len(r) > 1] + [[r[0] for r in grid_rows if len(r) == 1]]
    while len(lists) > 1:
        lists.sort(key=len)
        size = min(k, _pow2_ceil(len(lists[0]) + len(lists[1])))
        merged = [m for m in _merge_topk(lists[0], lists[1], size) if m is not None]
        lists = [merged] + lists[2:]
    ctop = _merge_topk(grid_rows[0], lists[0], k)
    thr, cmax = ctop[k - 1], ctop[0]

    taken = {c: v >= thr for c, v in cells.items()}
    n_taken = sum(jnp.where(t, 1.0, 0.0) for t in taken.values())
    z = sum(jnp.where(taken[c], jnp.exp(v - cmax), 0.0) for c, v in cells.items())
    inv_z = 1.0 / z
    y1_rows = [1.0 - sum(jnp.where(taken[i, j], 1.0, 0.0) for j in range(k // (i + 1))) for i in range(k)]

    exact = jnp.where(n_taken == float(k), 1.0, 0.0)
    for top in (top1, top2):
        for i in range(k - 1):
            exact = jnp.where(top[i] > top[i + 1], exact, 0.0)

    tables = []
    for h in range(PEER_HEADS):
        s1, s2 = scores[h]
        row = slice(h, h + 1)
        y1 = jnp.ones(s1.shape, F32)
        for i in range(k):
            y1 = jnp.where(s1 == top1[i][row], y1_rows[i][row], y1)
        y2 = jnp.full(s2.shape, -float(k), F32)
        for j in reversed(range(k)):
            y2 = jnp.where(s2 >= top2[j][row], -float(j), y2)
        tables.append((y1, jnp.exp(s1 - top1[0][row]) * inv_z[row], y2, jnp.exp(s2 - top2[0][row])))
        for s, top in ((s1, top1), (s2, top2)):
            members = jnp.sum(jnp.where(s >= top[k - 1][row], 1.0, 0.0), axis=0, keepdims=True)
            exact = jnp.where(jnp.logical_and(sublane == h, members != float(k)), 0.0, exact)
    return tables, exact


def _peer_route_kernel(q_ref, keys_ref, y1_ref, e1_ref, y2_ref, e2_ref):
    half = PEER_DK // 2
    scores = []
    for h in range(PEER_HEADS):
        q = q_ref[:, h * PEER_DK:(h + 1) * PEER_DK]
        scores.append((_dot_nt(keys_ref[h, 0], q[:, :half]),
                       _dot_nt(keys_ref[h, 1], q[:, half:])))

    def store(h, tables):
        y1, e1, y2, e2 = tables
        y1_ref[h] = y1
        e1_ref[h] = e1
        y2_ref[h] = pltpu.bitcast(y2.astype(BF16), jnp.uint32)
        e2_ref[h] = pltpu.bitcast(e2.astype(BF16), jnp.uint32)

    tables, exact = _route_tables_by_value(scores)
    for h in range(PEER_HEADS):
        store(h, tables[h])

    for h in range(PEER_HEADS):
        @pl.when(jnp.min(exact[h:h + 1, :]) < 0.5)
        def _(h=h):
            store(h, _route_head_exact(*scores[h]))


def _peer_route(q2, sub_keys, *, tm):
    r = q2.shape[0]
    width = PEER_HEADS * PEER_DK
    tab_spec = pl.BlockSpec((PEER_HEADS, PEER_NKEYS, tm), lambda i: (0, 0, i))
    tab = jax.ShapeDtypeStruct((PEER_HEADS, PEER_NKEYS, r), F32)
    packed_spec = pl.BlockSpec((PEER_HEADS, PEER_NKEYS // 2, tm), lambda i: (0, 0, i))
    packed = jax.ShapeDtypeStruct((PEER_HEADS, PEER_NKEYS // 2, r), jnp.uint32)
    est = (2 * (tm * width * 2 + PEER_HEADS * PEER_NKEYS * PEER_DK * 2 + PEER_HEADS * PEER_NKEYS * tm * 16)
           + 40 * PEER_HEADS * PEER_NKEYS * tm * 4)
    return pl.pallas_call(
        _peer_route_kernel,
        grid=(r // tm,),
        in_specs=[
            pl.BlockSpec((tm, width), lambda i: (i, 0)),
            pl.BlockSpec((PEER_HEADS, 2, PEER_NKEYS, PEER_DK // 2), lambda i: (0, 0, 0, 0)),
        ],
        out_specs=[tab_spec, tab_spec, packed_spec, packed_spec],
        out_shape=[tab, tab, packed, packed],
        compiler_params=_params(("parallel",), est),
        name="peer_route",
    )(q2, sub_keys)


def _peer_dense_kernel(hnt_ref, u_ref, vt_ref, y1_ref, e1_ref, y2_ref, e2_ref, h_ref, g_ref, o_ref,
                       acc_ref, act_ref, p_ref, *, te, final_norm):
    j = pl.program_id(1)
    d, tm = acc_ref.shape

    @pl.when(j == 0)
    def _():
        acc_ref[...] = jnp.zeros_like(acc_ref)

    act = jax.nn.gelu(jnp.dot(u_ref[...], hnt_ref[...], preferred_element_type=F32))
    act_ref[...] = pltpu.bitcast(act.astype(BF16), jnp.uint32)
    a_blocks = te // PEER_NKEYS
    a0 = j * a_blocks
    zero = jnp.zeros((), BF16)
    for ai in range(a_blocks):
        y1_rows = [jnp.broadcast_to(y1_ref[h, pl.ds(a0 + ai, 1), :], (BF16_ROWS, tm)).astype(BF16)
                   for h in range(PEER_HEADS)]
        e1_rows = [jnp.broadcast_to(e1_ref[h, pl.ds(a0 + ai, 1), :], (BF16_ROWS, tm)).astype(BF16)
                   for h in range(PEER_HEADS)]
        for c in range(tm // V7X_LANES):
            cols = slice(c * V7X_LANES, (c + 1) * V7X_LANES)
            for g in range(PEER_NKEYS // BF16_ROWS):
                b_words = slice(g * BF16_ROWS // 2, (g + 1) * BF16_ROWS // 2)
                gate = None
                for h in range(PEER_HEADS):
                    sel = pltpu.bitcast(y2_ref[h, b_words, cols], BF16) >= y1_rows[h][:, cols]
                    term = jnp.where(sel, e1_rows[h][:, cols] * pltpu.bitcast(e2_ref[h, b_words, cols], BF16), zero)
                    gate = term if gate is None else gate + term
                e_words = slice((ai * PEER_NKEYS + g * BF16_ROWS) // 2, (ai * PEER_NKEYS + (g + 1) * BF16_ROWS) // 2)
                p_ref[e_words, cols] = pltpu.bitcast(gate * pltpu.bitcast(act_ref[e_words, cols], BF16), jnp.uint32)
    acc_ref[...] += jnp.dot(vt_ref[...], pltpu.bitcast(p_ref[...], BF16), preferred_element_type=F32)

    @pl.when(j == pl.num_programs(1) - 1)
    def _():
        out = h_ref[...] + acc_ref[...].T
        o_ref[...] = _rmsnorm(out, g_ref[...]) if final_norm else out


def _peer_dense(hn2t, u, vt, tabs, h2, gf, *, tm, te, final_norm):
    d, r = hn2t.shape
    e = u.shape[0]
    y1, e1, y2, e2 = tabs
    tab_spec = pl.BlockSpec((PEER_HEADS, PEER_NKEYS, tm), lambda i, j: (0, 0, i))
    packed_spec = pl.BlockSpec((PEER_HEADS, PEER_NKEYS // 2, tm), lambda i, j: (0, 0, i))
    est = (2 * (tm * d * 2 + 2 * te * d * 2 + PEER_HEADS * PEER_NKEYS * tm * 12 + tm * d * 4)
           + tm * d * 4 + d * tm * 4 + 6 * te * tm * 4)
    return pl.pallas_call(
        functools.partial(_peer_dense_kernel, te=te, final_norm=final_norm),
        grid=(r // tm, e // te),
        in_specs=[
            pl.BlockSpec((d, tm), lambda i, j: (0, i)),
            pl.BlockSpec((te, d), lambda i, j: (j, 0)),
            pl.BlockSpec((d, te), lambda i, j: (0, j)),
            tab_spec, tab_spec, packed_spec, packed_spec,
            pl.BlockSpec((tm, d), lambda i, j: (i, 0), pipeline_mode=pl.Buffered(1)),
            pl.BlockSpec((1, d), lambda i, j: (0, 0)),
        ],
        out_specs=pl.BlockSpec((tm, d), lambda i, j: (i, 0)),
        out_shape=jax.ShapeDtypeStruct((r, d), F32),
        scratch_shapes=[pltpu.VMEM((d, tm), F32), pltpu.VMEM((te // 2, tm), jnp.uint32),
                        pltpu.VMEM((te // 2, tm), jnp.uint32)],
        compiler_params=_params(("parallel", "arbitrary"), est),
        name="peer_dense",
    )(hn2t, u, vt, y1, e1, y2, e2, h2, gf)


def _rope_tables(pos):
    inv_freq = ROPE_THETA ** (-jnp.arange(0, MLA_ROPE, 2, dtype=F32) / MLA_ROPE)
    ang = pos.astype(F32)[:, None] * inv_freq[None, :]
    cos, sin = jnp.cos(ang), jnp.sin(ang)
    zeros = jnp.zeros((pos.shape[0], V7X_LANES - MLA_ROPE), F32)
    return (jnp.concatenate([cos, cos, zeros], axis=1), jnp.concatenate([-sin, sin, zeros], axis=1))


def _swap_halves(w):
    half = w.shape[-1] // 2
    return jnp.concatenate([w[..., half:], w[..., :half]], axis=-1)


def _layer(x_real, x_meta, p, l, *, batch):
    d = x_real.shape[1]
    s_len = x_real.shape[0] // batch
    tm_real = _row_tile(x_real.shape[0], 1024)
    tm_meta = x_meta.shape[0]

    w_in = p["w_in"][l]
    kr0 = 3 * NA_WIDTH + 2 * MLA_RANK
    w_main = jnp.concatenate([w_in[:, :kr0], w_in[:, kr0 + MLA_ROPE:]], axis=1).astype(BF16)
    w_kr = w_in[:, kr0:kr0 + MLA_ROPE]
    w_kr = jnp.concatenate([w_kr, _swap_halves(w_kr)], axis=1).astype(BF16)
    norm_g = jnp.concatenate([p["mla_q_norm_g"][l], p["mla_kv_norm_g"][l]])[None].astype(F32)
    g1 = p["norm1_g"][l][None].astype(F32)
    wq = p["mla_w_uq"][l].reshape(MLA_RANK, MLA_HEADS, MLA_NOPE + MLA_ROPE)
    wq = jnp.concatenate([wq, _swap_halves(wq[..., MLA_NOPE:])], axis=-1)
    wq = wq.transpose(1, 0, 2).astype(BF16)
    wkv = p["mla_w_ukv"][l].reshape(MLA_RANK, MLA_HEADS, MLA_NOPE + MLA_V).transpose(1, 0, 2).astype(BF16)
    cos_r, sin_r = _rope_tables(N_META + jnp.arange(s_len))
    cos_m, sin_m = _rope_tables(jnp.tile(jnp.arange(N_META), batch))
    bias_tiles = _na_bias_tiles(p["na_rel_bias"][l])

    proj_r, kr_r = _in_proj(x_real, g1, w_main, w_kr, norm_g, tm=tm_real)
    proj_m, kr_m = _in_proj(x_meta, g1, w_main, w_kr, norm_g, tm=tm_meta)
    q_r, k_r, v_r = _mla_up(proj_r, kr_r, cos_r, sin_r, wq, wkv, tm=_row_tile(s_len, 1024))
    q_m, k_m, v_m = _mla_up(proj_m, kr_m, cos_m, sin_m, wq, wkv, tm=tm_meta)
    omla_r, omla_m = _mla_attn(q_r, k_r, v_r, q_m, k_m, v_m, batch=batch)
    ona_r, ona_m = _na_attn(proj_r, proj_m, bias_tiles, p["na_meta_bias"][l], batch=batch)

    w_na = p["w_na_branch"][l].astype(BF16)
    w_mla = p["w_mla_branch"][l].astype(BF16)
    w_out = p["w_out"][l].astype(BF16)
    h_real = _out_proj(_merge(ona_r, omla_r, w_na, w_mla, proj_r, tm=tm_real), w_out, x_real, tm=tm_real)
    h_meta = _out_proj(_merge(ona_m, omla_m, w_na, w_mla, proj_m, tm=tm_meta), w_out, x_meta, tm=tm_meta)
    return h_real, h_meta


def _peer(h2, p, l, gf, *, tm, te, final_norm):
    g2 = p["norm2_g"][l][None].astype(F32)
    w_q = p["peer_w_q"][l].astype(BF16)
    keys = p["peer_sub_keys"][l].astype(BF16)
    u = p["peer_u"][l].astype(BF16)
    vt = p["peer_v"][l].astype(BF16).T
    q2, hn2t = _peer_q(h2, g2, w_q, tm=_row_tile(h2.shape[0], 1024))
    tabs = _peer_route(q2, keys, tm=min(256, h2.shape[0]))
    return _peer_dense(hn2t, u, vt, tabs, h2, gf, tm=tm, te=te, final_norm=final_norm)


def kernel(x, meta_tokens, norm1_g, w_in, na_rel_bias, na_meta_bias, mla_q_norm_g, mla_w_uq, mla_kv_norm_g,
           mla_w_ukv, w_na_branch, w_mla_branch, w_out, norm2_g, peer_w_q, peer_sub_keys, peer_u, peer_v,
           final_norm_g):
    b, s, d = x.shape
    depth = w_in.shape[0]
    p = dict(norm1_g=norm1_g, w_in=w_in, na_rel_bias=na_rel_bias, na_meta_bias=na_meta_bias,
             mla_q_norm_g=mla_q_norm_g, mla_w_uq=mla_w_uq, mla_kv_norm_g=mla_kv_norm_g, mla_w_ukv=mla_w_ukv,
             w_na_branch=w_na_branch, w_mla_branch=w_mla_branch, w_out=w_out, norm2_g=norm2_g,
             peer_w_q=peer_w_q, peer_sub_keys=peer_sub_keys, peer_u=peer_u, peer_v=peer_v)
    x_real = x.reshape(b * s, d)
    x_meta = jnp.broadcast_to(meta_tokens.astype(x.dtype)[None], (b, N_META, d)).reshape(b * N_META, d)
    gf = final_norm_g[None].astype(F32)
    tm = _row_tile(b * s, 512)
    te = 512
    for l in range(depth):
        h_real, h_meta = _layer(x_real, x_meta, p, l, batch=b)
        x_real = _peer(h_real, p, l, gf, tm=tm, te=te, final_norm=l == depth - 1)
        x_meta = _peer(h_meta, p, l, gf, tm=h_meta.shape[0], te=te, final_norm=False)
    return x_real.reshape(b, s, d)
```

```python
import functools

import jax
import jax.numpy as jnp
import numpy as np
from jax import lax
from jax.experimental import pallas as pl
from jax.experimental.pallas import tpu as pltpu

GRID_W = 64
N_META = 16
NA_HEADS = 16
NA_HEAD_DIM = 64
NA_WIN_H = 8
NA_WIN_W = 16
NA_WIDTH = NA_HEADS * NA_HEAD_DIM
MLA_HEADS = 16
MLA_RANK = 512
MLA_NOPE = 128
MLA_ROPE = 64
MLA_V = 128
MLA_QK_PAD = 256
ROPE_THETA = 10000.0
PEER_HEADS = 8
PEER_NKEYS = 128
PEER_DK = 256
PEER_TOPK = 16
NORM_EPS = 1e-6
NEG_INF = -1e30

V7X_VMEM_BYTES = 64 * 1024 * 1024
V7X_VMEM_HEADROOM = 6 * 1024 * 1024
V7X_LANES = 128
BF16_ROWS = 16

NA_QROWS = 4
NA_KROWS = NA_QROWS + NA_WIN_H

F32 = jnp.float32
BF16 = jnp.bfloat16
_F32_MIN = float(np.finfo(np.float32).min)


def _vmem_limit(estimate_bytes):
    want = int(estimate_bytes * 1.25) + 4 * 1024 * 1024
    return min(max(want, 16 * 1024 * 1024), V7X_VMEM_BYTES - V7X_VMEM_HEADROOM)


def _params(semantics, vmem_estimate, flags=None):
    return pltpu.CompilerParams(dimension_semantics=semantics,
                                vmem_limit_bytes=_vmem_limit(vmem_estimate), flags=flags)


def _row_tile(rows, want):
    tile = min(rows, want)
    assert rows % tile == 0, (rows, tile)
    return tile


def _dot_nt(a, b):
    return lax.dot_general(a, b, (((1,), (1,)), ((), ())), preferred_element_type=F32)


def _rmsnorm(x, g):
    return x * lax.rsqrt(jnp.mean(x * x, axis=-1, keepdims=True) + NORM_EPS) * g


def _pad_meta_rows(a):
    return jnp.concatenate([a, jnp.zeros((V7X_LANES - N_META, a.shape[1]), a.dtype)], axis=0)


def _in_proj_kernel(x_ref, g_ref, w_ref, wkr_ref, ng_ref, o_ref, kr_ref, hn_ref, *, tn):
    j = pl.program_id(1)

    @pl.when(j == 0)
    def _():
        hn = _rmsnorm(x_ref[...], g_ref[...]).astype(BF16)
        hn_ref[...] = hn
        kr_ref[...] = jnp.dot(hn, wkr_ref[...], preferred_element_type=F32)

    acc = jnp.dot(hn_ref[...], w_ref[...], preferred_element_type=F32)
    col0 = j * tn
    plain_end = 3 * NA_WIDTH
    norm_end = plain_end + 2 * MLA_RANK

    @pl.when(col0 < plain_end)
    def _():
        scale = jnp.where(col0 < NA_WIDTH, NA_HEAD_DIM ** -0.5, 1.0).astype(F32)
        o_ref[...] = (acc * scale).astype(BF16)

    @pl.when(jnp.logical_and(col0 >= plain_end, col0 < norm_end))
    def _():
        for c in range(tn // MLA_RANK):
            sl = slice(c * MLA_RANK, (c + 1) * MLA_RANK)
            o_ref[:, sl] = _rmsnorm(acc[:, sl], ng_ref[:, sl]).astype(BF16)

    @pl.when(col0 >= norm_end)
    def _():
        o_ref[...] = jax.nn.sigmoid(acc).astype(BF16)


def _in_proj(x2d, g1, w_main, w_kr, norm_g, *, tm):
    r, d = x2d.shape
    n = w_main.shape[1]
    tn = 1024 if d % 1024 == 0 else 512
    norm_blk0 = (3 * NA_WIDTH) // tn
    norm_nblk = (2 * MLA_RANK) // tn
    est = 2 * (tm * d * 4 + d * tn * 2 + tm * tn * 2 + tm * 128 * 4 + d * 128 * 2) + tm * d * 2 + tm * tn * 4
    return pl.pallas_call(
        functools.partial(_in_proj_kernel, tn=tn),
        grid=(r // tm, n // tn),
        in_specs=[
            pl.BlockSpec((tm, d), lambda i, j: (i, 0)),
            pl.BlockSpec((1, d), lambda i, j: (0, 0)),
            pl.BlockSpec((d, tn), lambda i, j: (0, j)),
            pl.BlockSpec((d, 128), lambda i, j: (0, 0)),
            pl.BlockSpec((1, tn), lambda i, j: (0, jnp.clip(j - norm_blk0, 0, norm_nblk - 1))),
        ],
        out_specs=[
            pl.BlockSpec((tm, tn), lambda i, j: (i, j)),
            pl.BlockSpec((tm, 128), lambda i, j: (i, 0)),
        ],
        out_shape=[jax.ShapeDtypeStruct((r, n), BF16), jax.ShapeDtypeStruct((r, 128), F32)],
        scratch_shapes=[pltpu.VMEM((tm, d), BF16)],
        compiler_params=_params(("parallel", "arbitrary"), est),
        name="in_proj",
    )(x2d, g1, w_main, w_kr, norm_g)


def _rope_half(t, cos, sin):
    return t * cos + pltpu.roll(t, 64, axis=1) * sin


def _mla_up_kernel(cq_ref, ckv_ref, kr_ref, cos_ref, sin_ref, wq_ref, wkv_ref, q_ref, k_ref, v_ref):
    cos = cos_ref[...]
    sin = sin_ref[...]
    scale = (MLA_NOPE + MLA_ROPE) ** -0.5
    yq = jnp.dot(cq_ref[...], wq_ref[...], preferred_element_type=F32)
    ykv = jnp.dot(ckv_ref[...], wkv_ref[...], preferred_element_type=F32)
    k_rope = _rope_half(kr_ref[...], cos, sin).astype(BF16)
    for h in range(MLA_HEADS):
        c0 = h * MLA_QK_PAD
        q_ref[h, :, 0:MLA_NOPE] = (yq[:, c0:c0 + MLA_NOPE] * scale).astype(BF16)
        q_ref[h, :, MLA_NOPE:] = (_rope_half(yq[:, c0 + MLA_NOPE:c0 + MLA_QK_PAD], cos, sin) * scale).astype(BF16)
        k_ref[h, :, 0:MLA_NOPE] = ykv[:, c0:c0 + MLA_NOPE].astype(BF16)
        k_ref[h, :, MLA_NOPE:] = k_rope
        v_ref[h] = ykv[:, c0 + MLA_NOPE:c0 + MLA_NOPE + MLA_V].astype(BF16)


def _mla_up(proj, kr, cos_t, sin_t, wq_all, wkv_all, *, tm):
    r = proj.shape[0]
    cq_blk = (3 * NA_WIDTH) // MLA_RANK
    n_pos_blk = cos_t.shape[0] // tm
    width = wq_all.shape[1]
    est = (2 * (2 * tm * MLA_RANK * 2 + 3 * tm * 128 * 4 + 2 * MLA_RANK * width * 2
                + MLA_HEADS * tm * (2 * MLA_QK_PAD + MLA_V) * 2) + 3 * tm * width * 4)
    return pl.pallas_call(
        _mla_up_kernel,
        grid=(r // tm,),
        in_specs=[
            pl.BlockSpec((tm, MLA_RANK), lambda i: (i, cq_blk)),
            pl.BlockSpec((tm, MLA_RANK), lambda i: (i, cq_blk + 1)),
            pl.BlockSpec((tm, 128), lambda i: (i, 0)),
            pl.BlockSpec((tm, 128), lambda i: (i % n_pos_blk, 0)),
            pl.BlockSpec((tm, 128), lambda i: (i % n_pos_blk, 0)),
            pl.BlockSpec((MLA_RANK, width), lambda i: (0, 0)),
            pl.BlockSpec((MLA_RANK, width), lambda i: (0, 0)),
        ],
        out_specs=[
            pl.BlockSpec((MLA_HEADS, tm, MLA_QK_PAD), lambda i: (0, i, 0)),
            pl.BlockSpec((MLA_HEADS, tm, MLA_QK_PAD), lambda i: (0, i, 0)),
            pl.BlockSpec((MLA_HEADS, tm, MLA_V), lambda i: (0, i, 0)),
        ],
        out_shape=[
            jax.ShapeDtypeStruct((MLA_HEADS, r, MLA_QK_PAD), BF16),
            jax.ShapeDtypeStruct((MLA_HEADS, r, MLA_QK_PAD), BF16),
            jax.ShapeDtypeStruct((MLA_HEADS, r, MLA_V), BF16),
        ],
        compiler_params=_params(("parallel",), est),
        name="mla_up",
    )(proj, proj, kr, cos_t, sin_t, wq_all, wkv_all)


def _mla_attn_kernel(q_ref, k_ref, v_ref, qm_ref, km_ref, vm_ref, o_ref, om_ref, *, tq):
    k = k_ref[0]
    v = v_ref[0]
    km = _pad_meta_rows(km_ref[0])
    vm = _pad_meta_rows(vm_ref[0])
    meta_lane = lax.broadcasted_iota(jnp.int32, (1, V7X_LANES), 1) < N_META

    def attend(q):
        s = _dot_nt(q, k)
        sm = jnp.where(meta_lane, _dot_nt(q, km), NEG_INF)
        m = jnp.maximum(jnp.max(s, axis=-1, keepdims=True), jnp.max(sm, axis=-1, keepdims=True))
        p = jnp.exp(s - m)
        pm = jnp.exp(sm - m)
        l = jnp.sum(p, axis=-1, keepdims=True) + jnp.sum(pm, axis=-1, keepdims=True)
        o = (jnp.dot(p.astype(BF16), v, preferred_element_type=F32)
             + jnp.dot(pm.astype(BF16), vm, preferred_element_type=F32))
        return o / l

    s_len = q_ref.shape[1]
    for c in range(s_len // tq):
        rows = slice(c * tq, (c + 1) * tq)
        o_ref[rows, :] = attend(q_ref[0, rows, :]).astype(BF16)
    om_ref[...] = attend(qm_ref[0]).astype(BF16)


def _mla_attn(q, k, v, qm, km, vm, *, batch):
    r = q.shape[1]
    s_len = r // batch
    tq = min(512, s_len)
    est = 2 * (2 * s_len * 256 * 2 + 2 * s_len * 128 * 2) + 3 * tq * s_len * 4 + tq * s_len * 2
    return pl.pallas_call(
        functools.partial(_mla_attn_kernel, tq=tq),
        grid=(batch, MLA_HEADS),
        in_specs=[
            pl.BlockSpec((1, s_len, MLA_QK_PAD), lambda b, h: (h, b, 0)),
            pl.BlockSpec((1, s_len, MLA_QK_PAD), lambda b, h: (h, b, 0)),
            pl.BlockSpec((1, s_len, MLA_V), lambda b, h: (h, b, 0)),
            pl.BlockSpec((1, N_META, MLA_QK_PAD), lambda b, h: (h, b, 0)),
            pl.BlockSpec((1, N_META, MLA_QK_PAD), lambda b, h: (h, b, 0)),
            pl.BlockSpec((1, N_META, MLA_V), lambda b, h: (h, b, 0)),
        ],
        out_specs=[
            pl.BlockSpec((s_len, MLA_V), lambda b, h: (b, h)),
            pl.BlockSpec((N_META, MLA_V), lambda b, h: (b, h)),
        ],
        out_shape=[
            jax.ShapeDtypeStruct((r, MLA_HEADS * MLA_V), BF16),
            jax.ShapeDtypeStruct((batch * N_META, MLA_HEADS * MLA_V), BF16),
        ],
        compiler_params=_params(("parallel", "parallel"), est),
        name="mla_attn",
    )(q, k, v, qm, km, vm)


def _na_bias_tiles(rel_bias):
    qc = np.arange(GRID_W)
    kc = np.arange(GRID_W)
    cstart = np.clip(qc - NA_WIN_W // 2, 0, GRID_W - NA_WIN_W)
    col_ok = (kc[None, :] >= cstart[:, None]) & (kc[None, :] < cstart[:, None] + NA_WIN_W)
    col_off = np.clip(kc[None, :] - qc[:, None], -(NA_WIN_W - 1), NA_WIN_W - 1) + NA_WIN_W - 1
    n_co = 2 * NA_WIN_W - 1
    onehot = np.zeros((n_co, GRID_W * GRID_W), np.float32)
    onehot[col_off.reshape(-1), np.arange(GRID_W * GRID_W)] = 1.0
    h, n_ro = rel_bias.shape[0], rel_bias.shape[1]
    tiles = jnp.dot(rel_bias.astype(F32).reshape(h * n_ro, n_co), jnp.asarray(onehot),
                    precision=lax.Precision.HIGHEST).reshape(h, n_ro, GRID_W, GRID_W)
    return jnp.where(jnp.asarray(col_ok)[None, None], tiles, NEG_INF)


def _na_block_plan(rows):
    nqb = rows // NA_QROWS
    wh = NA_WIN_H
    plan = []
    for qb in (0, 1, nqb - 1):
        kstart = int(np.clip(NA_QROWS * qb - wh // 2, 0, rows - NA_KROWS))
        variant = []
        for i in range(NA_QROWS):
            r = NA_QROWS * qb + i
            rs = int(np.clip(r - wh // 2, 0, rows - wh))
            variant.append([kstart + j - r + wh - 1 if rs <= kstart + j < rs + wh else None
                            for j in range(NA_KROWS)])
        plan.append(variant)
    return plan


def _na_attn_kernel(q_ref, k_ref, v_ref, qm_ref, km_ref, vm_ref, tiles_ref, mb_ref, o_ref, om_ref, bias_ref,
                    *, rows):
    nqb = rows // NA_QROWS
    qblk = NA_QROWS * GRID_W
    kblk = NA_KROWS * GRID_W

    @pl.when(pl.program_id(1) == 0)
    def _():
        masked = jnp.full((GRID_W, GRID_W), NEG_INF, F32)
        for hh in range(2):
            for v, variant in enumerate(_na_block_plan(rows)):
                for i, row_plan in enumerate(variant):
                    for j0 in range(0, NA_KROWS, 2):
                        pair = [masked if ro is None else tiles_ref[0, hh, ro] for ro in row_plan[j0:j0 + 2]]
                        bias_ref[hh, v, i * GRID_W:(i + 1) * GRID_W, j0 * GRID_W:(j0 + 2) * GRID_W] = (
                            jnp.concatenate(pair, axis=1))

    km = _pad_meta_rows(km_ref[...])
    vm = _pad_meta_rows(vm_ref[...])
    lane = lax.broadcasted_iota(jnp.int32, (1, 2 * NA_HEAD_DIM), 1)
    head_lanes = [lane < NA_HEAD_DIM, lane >= NA_HEAD_DIM]

    def softmax_pv(s, sm, v_loc):
        m = jnp.maximum(jnp.max(s, axis=-1, keepdims=True), jnp.max(sm, axis=-1, keepdims=True))
        p = jnp.exp(s - m)
        pm = jnp.exp(sm - m)
        l = jnp.sum(p, axis=-1, keepdims=True) + jnp.sum(pm, axis=-1, keepdims=True)
        o = (jnp.dot(p.astype(BF16), v_loc, preferred_element_type=F32)
             + jnp.dot(pm.astype(BF16), vm, preferred_element_type=F32))
        return o / l

    def block(qb, carry):
        q0 = pl.multiple_of(qb * qblk, qblk)
        k0 = pl.multiple_of(jnp.clip(NA_QROWS * qb - NA_WIN_H // 2, 0, rows - NA_KROWS) * GRID_W, GRID_W)
        variant = jnp.where(qb == 0, 0, jnp.where(qb == nqb - 1, 2, 1))
        q = q_ref[pl.ds(q0, qblk), :]
        k_loc = k_ref[pl.ds(k0, kblk), :]
        v_loc = v_ref[pl.ds(k0, kblk), :]
        outs = []
        for hh in range(2):
            qh = jnp.where(head_lanes[hh], q, jnp.zeros_like(q))
            s = _dot_nt(qh, k_loc) + bias_ref[hh, variant]
            sm = _dot_nt(qh, km) + mb_ref[0, hh]
            outs.append(softmax_pv(s, sm, v_loc))
        o_ref[pl.ds(q0, qblk), :] = jnp.where(head_lanes[0], outs[0], outs[1]).astype(BF16)
        return carry

    lax.fori_loop(0, nqb, block, 0)

    qm = qm_ref[...]
    outs = []
    for hh in range(2):
        qh = jnp.where(head_lanes[hh], qm, jnp.zeros_like(qm))
        sm = _dot_nt(qh, km) + mb_ref[0, hh]
        m = jnp.max(sm, axis=-1, keepdims=True)
        pm = jnp.exp(sm - m)
        l = jnp.sum(pm, axis=-1, keepdims=True)
        outs.append(jnp.dot(pm.astype(BF16), vm, preferred_element_type=F32) / l)
    om_ref[...] = jnp.where(head_lanes[0], outs[0], outs[1]).astype(BF16)


def _na_attn(proj, proj_m, bias_tiles, meta_bias, *, batch):
    r = proj.shape[0]
    s_len = r // batch
    rows = s_len // GRID_W
    assert rows % NA_QROWS == 0 and rows >= NA_KROWS, rows
    npair = NA_HEADS // 2
    qblk = NA_QROWS * GRID_W
    kblk = NA_KROWS * GRID_W
    n_ro = bias_tiles.shape[1]
    tiles5 = bias_tiles.reshape(npair, 2, n_ro, GRID_W, GRID_W)
    mb4 = jnp.concatenate([meta_bias.astype(F32), jnp.full((NA_HEADS, V7X_LANES - N_META), NEG_INF, F32)],
                          axis=1).reshape(npair, 2, 1, V7X_LANES)
    est = 2 * (3 * s_len * 128 * 2 + 6 * qblk * kblk * 4 + s_len * 128 * 2) + 6 * qblk * kblk * 4
    return pl.pallas_call(
        functools.partial(_na_attn_kernel, rows=rows),
        grid=(npair, batch),
        in_specs=[
            pl.BlockSpec((s_len, 128), lambda p, b: (b, p)),
            pl.BlockSpec((s_len, 128), lambda p, b: (b, npair + p)),
            pl.BlockSpec((s_len, 128), lambda p, b: (b, 2 * npair + p)),
            pl.BlockSpec((N_META, 128), lambda p, b: (b, p)),
            pl.BlockSpec((N_META, 128), lambda p, b: (b, npair + p)),
            pl.BlockSpec((N_META, 128), lambda p, b: (b, 2 * npair + p)),
            pl.BlockSpec((1, 2, n_ro, GRID_W, GRID_W), lambda p, b: (p, 0, 0, 0, 0)),
            pl.BlockSpec((1, 2, 1, V7X_LANES), lambda p, b: (p, 0, 0, 0)),
        ],
        out_specs=[
            pl.BlockSpec((s_len, 128), lambda p, b: (b, p)),
            pl.BlockSpec((N_META, 128), lambda p, b: (b, p)),
        ],
        out_shape=[
            jax.ShapeDtypeStruct((r, NA_WIDTH), BF16),
            jax.ShapeDtypeStruct((batch * N_META, NA_WIDTH), BF16),
        ],
        scratch_shapes=[pltpu.VMEM((2, 3, qblk, kblk), F32)],
        compiler_params=_params(("parallel", "arbitrary"), est),
        name="na_attn",
    )(proj, proj, proj, proj_m, proj_m, proj_m, tiles5, mb4)


def _merge_kernel(ona_ref, omla_ref, wna_ref, wmla_ref, ga_ref, gb_ref, o_ref):
    a = jnp.dot(ona_ref[...], wna_ref[...], preferred_element_type=F32)
    b = jnp.dot(omla_ref[...], wmla_ref[...], preferred_element_type=F32)
    o_ref[...] = (ga_ref[...].astype(F32) * a + gb_ref[...].astype(F32) * b).astype(BF16)


def _merge(o_na, o_mla, w_na, w_mla, proj, *, tm):
    r = o_na.shape[0]
    d = w_na.shape[1]
    tn = 1024 if d % 1024 == 0 else 512
    ga_blk = (3 * NA_WIDTH + 2 * MLA_RANK) // tn
    gb_blk = ga_blk + d // tn
    est = 2 * (tm * (NA_WIDTH + MLA_HEADS * MLA_V) * 2 + (NA_WIDTH + MLA_HEADS * MLA_V) * tn * 2
               + 3 * tm * tn * 2) + 3 * tm * tn * 4
    return pl.pallas_call(
        _merge_kernel,
        grid=(r // tm, d // tn),
        in_specs=[
            pl.BlockSpec((tm, NA_WIDTH), lambda i, j: (i, 0)),
            pl.BlockSpec((tm, MLA_HEADS * MLA_V), lambda i, j: (i, 0)),
            pl.BlockSpec((NA_WIDTH, tn), lambda i, j: (0, j)),
            pl.BlockSpec((MLA_HEADS * MLA_V, tn), lambda i, j: (0, j)),
            pl.BlockSpec((tm, tn), lambda i, j: (i, ga_blk + j)),
            pl.BlockSpec((tm, tn), lambda i, j: (i, gb_blk + j)),
        ],
        out_specs=pl.BlockSpec((tm, tn), lambda i, j: (i, j)),
        out_shape=jax.ShapeDtypeStruct((r, d), BF16),
        compiler_params=_params(("parallel", "arbitrary"), est),
        name="merge",
    )(o_na, o_mla, w_na, w_mla, proj, proj)


def _out_proj_kernel(m_ref, w_ref, x_ref, o_ref):
    o_ref[...] = x_ref[...] + jnp.dot(m_ref[...], w_ref[...], preferred_element_type=F32)


def _out_proj(merged, w_out, x2d, *, tm):
    r, d = x2d.shape
    tn = 1024 if d % 1024 == 0 else 512
    est = 2 * (tm * d * 2 + d * tn * 2 + 2 * tm * tn * 4) + tm * tn * 4
    return pl.pallas_call(
        _out_proj_kernel,
        grid=(r // tm, d // tn),
        in_specs=[
            pl.BlockSpec((tm, d), lambda i, j: (i, 0)),
            pl.BlockSpec((d, tn), lambda i, j: (0, j)),
            pl.BlockSpec((tm, tn), lambda i, j: (i, j)),
        ],
        out_specs=pl.BlockSpec((tm, tn), lambda i, j: (i, j)),
        out_shape=jax.ShapeDtypeStruct((r, d), F32),
        compiler_params=_params(("parallel", "arbitrary"), est),
        name="out_proj",
    )(merged, w_out, x2d)


def _peer_q_kernel(h_ref, g_ref, w_ref, q_ref, hnt_ref, hn_ref):
    @pl.when(pl.program_id(1) == 0)
    def _():
        hn = _rmsnorm(h_ref[...], g_ref[...])
        hn_ref[...] = hn.astype(BF16)
        hnt_ref[...] = hn.T.astype(BF16)

    q_ref[...] = jnp.dot(hn_ref[...], w_ref[...], preferred_element_type=F32).astype(BF16)


def _peer_q(h2, g2, w_q, *, tm):
    r, d = h2.shape
    n = w_q.shape[1]
    tn = 1024
    est = 2 * (tm * d * 4 + d * tn * 2 + tm * tn * 2 + tm * d * 2) + tm * d * 2 + tm * tn * 4 + 2 * tm * d * 4
    return pl.pallas_call(
        _peer_q_kernel,
        grid=(r // tm, n // tn),
        in_specs=[
            pl.BlockSpec((tm, d), lambda i, j: (i, 0)),
            pl.BlockSpec((1, d), lambda i, j: (0, 0)),
            pl.BlockSpec((d, tn), lambda i, j: (0, j)),
        ],
        out_specs=[
            pl.BlockSpec((tm, tn), lambda i, j: (i, j)),
            pl.BlockSpec((d, tm), lambda i, j: (0, i)),
        ],
        out_shape=[jax.ShapeDtypeStruct((r, n), BF16), jax.ShapeDtypeStruct((d, r), BF16)],
        scratch_shapes=[pltpu.VMEM((tm, d), BF16)],
        compiler_params=_params(("parallel", "arbitrary"), est),
        name="peer_q",
    )(h2, g2, w_q)


def _compare_exchange(a, b):
    if a is None:
        return b, None
    if b is None:
        return a, None
    return jnp.maximum(a, b), jnp.minimum(a, b)


def _bitonic_merge(xs):
    n = len(xs)
    step = n // 2
    while step >= 1:
        for i in range(n):
            if i & step == 0:
                xs[i], xs[i + step] = _compare_exchange(xs[i], xs[i + step])
        step //= 2
    return xs


def _bitonic_sort(xs):
    n = len(xs)
    xs = list(xs)
    size = 2
    while size <= n:
        step = size // 2
        while step >= 1:
            for i in range(n):
                if i & step == 0:
                    hi, lo = _compare_exchange(xs[i], xs[i + step])
                    xs[i], xs[i + step] = (hi, lo) if i & size == 0 else (lo, hi)
            step //= 2
        size *= 2
    return xs


def _merge_topk(xs, ys, k):
    xs = list(xs) + [None] * (k - len(xs))
    ys = list(ys) + [None] * (k - len(ys))
    return _bitonic_merge([_compare_exchange(xs[i], ys[k - 1 - i])[0] for i in range(k)])


def _pow2_ceil(n):
    return 1 << (n - 1).bit_length()


def _sorted_topk_slabs(s, k):
    slabs = _bitonic_sort([s[8 * v:8 * v + 8, :] for v in range(s.shape[0] // 8)])
    for shift in (4, 2, 1):
        slabs = _merge_topk(slabs, [pltpu.roll(x, shift, axis=0) for x in slabs], k)
    return slabs


def _extract_topk(x, k):
    n, t = x.shape
    row = lax.broadcasted_iota(jnp.int32, (n, t), 0).astype(F32)
    krow = lax.broadcasted_iota(jnp.int32, (k, t), 0)

    def body(i, carry):
        x, rank, vals = carry
        m = jnp.max(x, axis=0, keepdims=True)
        first = jnp.min(jnp.where(x == m, row, float(n)), axis=0, keepdims=True)
        hit = row == first
        rank = jnp.where(hit, jnp.asarray(i, F32), rank)
        x = jnp.where(hit, _F32_MIN, x)
        vals = jnp.where(krow == i, m, vals)
        return x, rank, vals

    _, rank, vals = lax.fori_loop(
        0, k, body, (x, jnp.full((n, t), float(k), F32), jnp.zeros((k, t), F32)))
    return vals, rank


def _route_head_exact(s1, s2):
    k = PEER_TOPK
    top1, rank1 = _extract_topk(s1, k)
    top2, rank2 = _extract_topk(s2, k)

    tm = s1.shape[1]
    pieces, spans = [], []
    off = 0
    for i in range(k):
        cnt = k // (i + 1)
        rows = -(-cnt // 8) * 8
        jrow = lax.broadcasted_iota(jnp.int32, (rows, tm), 0)
        pieces.append(jnp.where(jrow < cnt, top1[i:i + 1, :] + top2[0:rows, :], _F32_MIN))
        spans.append((off, rows))
        off += rows
    cand = jnp.concatenate(pieces, axis=0)
    ctop, crank = _extract_topk(cand, k)
    chosen = crank < float(k)
    cmax = ctop[0:1, :]
    z = jnp.sum(jnp.where(chosen, jnp.exp(cand - cmax), 0.0), axis=0, keepdims=True)

    y1 = jnp.ones(s1.shape, F32)
    for i, (off, rows) in enumerate(spans):
        n_sel = jnp.sum(jnp.where(chosen[off:off + rows, :], 1.0, 0.0), axis=0, keepdims=True)
        y1 = jnp.where(rank1 == float(i), 1.0 - n_sel, y1)
    return y1, jnp.exp(s1 - top1[0:1, :]) / z, -rank2, jnp.exp(s2 - top2[0:1, :])


def _route_tables_by_value(scores):
    k = PEER_TOPK
    tm = scores[0][0].shape[1]

    sublane = lax.broadcasted_iota(jnp.int32, (8, tm), 0)
    tops = []
    for side in range(2):
        packed = None
        for h in range(PEER_HEADS):
            slabs = _sorted_topk_slabs(scores[h][side], k)
            packed = slabs if packed is None else [jnp.where(sublane == h, s, p) for s, p in zip(slabs, packed)]
        tops.append(packed)
    top1, top2 = tops

    cells = {(i, j): top1[i] + top2[j] for i in range(k) for j in range(k // (i + 1))}
    grid_rows = [[cells[i, j] for j in range(k // (i + 1))] for i in range(k)]
    lists = [r for r in grid_rows[1:] if len(r) > 1] + [[r[0] for r in grid_rows if len(r) == 1]]
    while len(lists) > 1:
        lists.sort(key=len)
        size = min(k, _pow2_ceil(len(lists[0]) + len(lists[1])))
        merged = [m for m in _merge_topk(lists[0], lists[1], size) if m is not None]
        lists = [merged] + lists[2:]
    ctop = _merge_topk(grid_rows[0], lists[0], k)
    thr, cmax = ctop[k - 1], ctop[0]

    taken = {c: v >= thr for c, v in cells.items()}
    n_taken = sum(jnp.where(t, 1.0, 0.0) for t in taken.values())
    z = sum(jnp.where(taken[c], jnp.exp(v - cmax), 0.0) for c, v in cells.items())
    inv_z = 1.0 / z
    y1_rows = [1.0 - sum(jnp.where(taken[i, j], 1.0, 0.0) for j in range(k // (i + 1))) for i in range(k)]

    exact = jnp.where(n_taken == float(k), 1.0, 0.0)
    for top in (top1, top2):
        for i in range(k - 1):
            exact = jnp.where(top[i] > top[i + 1], exact, 0.0)

    tables = []
    for h in range(PEER_HEADS):
        s1, s2 = scores[h]
        row = slice(h, h + 1)
        y1 = jnp.ones(s1.shape, F32)
        for i in range(k):
            y1 = jnp.where(s1 == top1[i][row], y1_rows[i][row], y1)
        y2 = jnp.full(s2.shape, -float(k), F32)
        for j in reversed(range(k)):
            y2 = jnp.where(s2 >= top2[j][row], -float(j), y2)
        tables.append((y1, jnp.exp(s1 - top1[0][row]) * inv_z[row], y2, jnp.exp(s2 - top2[0][row])))
        for s, top in ((s1, top1), (s2, top2)):
            members = jnp.sum(jnp.where(s >= top[k - 1][row], 1.0, 0.0), axis=0, keepdims=True)
            exact = jnp.where(jnp.logical_and(sublane == h, members != float(k)), 0.0, exact)
    return tables, exact


def _peer_route_kernel(q_ref, keys_ref, y1_ref, e1_ref, y2_ref, e2_ref):
    half = PEER_DK // 2
    scores = []
    for h in range(PEER_HEADS):
        q = q_ref[:, h * PEER_DK:(h + 1) * PEER_DK]
        scores.append((_dot_nt(keys_ref[h, 0], q[:, :half]),
                       _dot_nt(keys_ref[h, 1], q[:, half:])))

    def store(h, tables):
        y1, e1, y2, e2 = tables
        y1_ref[h] = y1
        e1_ref[h] = e1
        y2_ref[h] = pltpu.bitcast(y2.astype(BF16), jnp.uint32)
        e2_ref[h] = pltpu.bitcast(e2.astype(BF16), jnp.uint32)

    tables, exact = _route_tables_by_value(scores)
    for h in range(PEER_HEADS):
        store(h, tables[h])

    for h in range(PEER_HEADS):
        @pl.when(jnp.min(exact[h:h + 1, :]) < 0.5)
        def _(h=h):
            store(h, _route_head_exact(*scores[h]))


def _peer_route(q2, sub_keys, *, tm):
    r = q2.shape[0]
    width = PEER_HEADS * PEER_DK
    tab_spec = pl.BlockSpec((PEER_HEADS, PEER_NKEYS, tm), lambda i: (0, 0, i))
    tab = jax.ShapeDtypeStruct((PEER_HEADS, PEER_NKEYS, r), F32)
    packed_spec = pl.BlockSpec((PEER_HEADS, PEER_NKEYS // 2, tm), lambda i: (0, 0, i))
    packed = jax.ShapeDtypeStruct((PEER_HEADS, PEER_NKEYS // 2, r), jnp.uint32)
    est = (2 * (tm * width * 2 + PEER_HEADS * PEER_NKEYS * PEER_DK * 2 + PEER_HEADS * PEER_NKEYS * tm * 16)
           + 40 * PEER_HEADS * PEER_NKEYS * tm * 4)
    return pl.pallas_call(
        _peer_route_kernel,
        grid=(r // tm,),
        in_specs=[
            pl.BlockSpec((tm, width), lambda i: (i, 0)),
            pl.BlockSpec((PEER_HEADS, 2, PEER_NKEYS, PEER_DK // 2), lambda i: (0, 0, 0, 0)),
        ],
        out_specs=[tab_spec, tab_spec, packed_spec, packed_spec],
        out_shape=[tab, tab, packed, packed],
        compiler_params=_params(("parallel",), est),
        name="peer_route",
    )(q2, sub_keys)


def _peer_dense_kernel(hnt_ref, u_ref, vt_ref, y1_ref, e1_ref, y2_ref, e2_ref, h_ref, g_ref, o_ref,
                       acc_ref, act_ref, p_ref, *, te, final_norm):
    j = pl.program_id(1)
    d, tm = acc_ref.shape

    @pl.when(j == 0)
    def _():
        acc_ref[...] = jnp.zeros_like(acc_ref)

    act = jax.nn.gelu(jnp.dot(u_ref[...], hnt_ref[...], preferred_element_type=F32).astype(BF16))
    act_ref[...] = pltpu.bitcast(act, jnp.uint32)
    a_blocks = te // PEER_NKEYS
    a0 = j * a_blocks
    zero = jnp.zeros((), BF16)
    for ai in range(a_blocks):
        y1_rows = [jnp.broadcast_to(y1_ref[h, pl.ds(a0 + ai, 1), :], (BF16_ROWS, tm)).astype(BF16)
                   for h in range(PEER_HEADS)]
        e1_rows = [jnp.broadcast_to(e1_ref[h, pl.ds(a0 + ai, 1), :], (BF16_ROWS, tm)).astype(BF16)
                   for h in range(PEER_HEADS)]
        for c in range(tm // V7X_LANES):
            cols = slice(c * V7X_LANES, (c + 1) * V7X_LANES)
            for g in range(PEER_NKEYS // BF16_ROWS):
                b_words = slice(g * BF16_ROWS // 2, (g + 1) * BF16_ROWS // 2)
                gate = None
                for h in range(PEER_HEADS):
                    sel = pltpu.bitcast(y2_ref[h, b_words, cols], BF16) >= y1_rows[h][:, cols]
                    term = jnp.where(sel, e1_rows[h][:, cols] * pltpu.bitcast(e2_ref[h, b_words, cols], BF16), zero)
                    gate = term if gate is None else gate + term
                e_words = slice((ai * PEER_NKEYS + g * BF16_ROWS) // 2, (ai * PEER_NKEYS + (g + 1) * BF16_ROWS) // 2)
                p_ref[e_words, cols] = pltpu.bitcast(gate * pltpu.bitcast(act_ref[e_words, cols], BF16), jnp.uint32)
    acc_ref[...] += jnp.dot(vt_ref[...], pltpu.bitcast(p_ref[...], BF16), preferred_element_type=F32)

    @pl.when(j == pl.num_programs(1) - 1)
    def _():
        out = h_ref[...] + acc_ref[...].T
        o_ref[...] = _rmsnorm(out, g_ref[...]) if final_norm else out


def _peer_dense(hn2t, u, vt, tabs, h2, gf, *, tm, te, final_norm):
    d, r = hn2t.shape
    e = u.shape[0]
    y1, e1, y2, e2 = tabs
    tab_spec = pl.BlockSpec((PEER_HEADS, PEER_NKEYS, tm), lambda i, j: (0, 0, i))
    packed_spec = pl.BlockSpec((PEER_HEADS, PEER_NKEYS // 2, tm), lambda i, j: (0, 0, i))
    est = (2 * (tm * d * 2 + 2 * te * d * 2 + PEER_HEADS * PEER_NKEYS * tm * 12 + tm * d * 4)
           + tm * d * 4 + d * tm * 4 + 6 * te * tm * 4)
    return pl.pallas_call(
        functools.partial(_peer_dense_kernel, te=te, final_norm=final_norm),
        grid=(r // tm, e // te),
        in_specs=[
            pl.BlockSpec((d, tm), lambda i, j: (0, i)),
            pl.BlockSpec((te, d), lambda i, j: (j, 0)),
            pl.BlockSpec((d, te), lambda i, j: (0, j)),
            tab_spec, tab_spec, packed_spec, packed_spec,
            pl.BlockSpec((tm, d), lambda i, j: (i, 0), pipeline_mode=pl.Buffered(1)),
            pl.BlockSpec((1, d), lambda i, j: (0, 0)),
        ],
        out_specs=pl.BlockSpec((tm, d), lambda i, j: (i, 0)),
        out_shape=jax.ShapeDtypeStruct((r, d), F32),
        scratch_shapes=[pltpu.VMEM((d, tm), F32), pltpu.VMEM((te // 2, tm), jnp.uint32),
                        pltpu.VMEM((te // 2, tm), jnp.uint32)],
        compiler_params=_params(("parallel", "arbitrary"), est),
        name="peer_dense",
    )(hn2t, u, vt, y1, e1, y2, e2, h2, gf)


def _rope_tables(pos):
    inv_freq = ROPE_THETA ** (-jnp.arange(0, MLA_ROPE, 2, dtype=F32) / MLA_ROPE)
    ang = pos.astype(F32)[:, None] * inv_freq[None, :]
    cos, sin = jnp.cos(ang), jnp.sin(ang)
    zeros = jnp.zeros((pos.shape[0], V7X_LANES - MLA_ROPE), F32)
    return (jnp.concatenate([cos, cos, zeros], axis=1), jnp.concatenate([-sin, sin, zeros], axis=1))


def _swap_halves(w):
    half = w.shape[-1] // 2
    return jnp.concatenate([w[..., half:], w[..., :half]], axis=-1)


def _layer(x_real, x_meta, p, l, *, batch):
    d = x_real.shape[1]
    s_len = x_real.shape[0] // batch
    tm_real = _row_tile(x_real.shape[0], 1024)
    tm_meta = x_meta.shape[0]

    w_in = p["w_in"][l]
    kr0 = 3 * NA_WIDTH + 2 * MLA_RANK
    w_main = jnp.concatenate([w_in[:, :kr0], w_in[:, kr0 + MLA_ROPE:]], axis=1).astype(BF16)
    w_kr = w_in[:, kr0:kr0 + MLA_ROPE]
    w_kr = jnp.concatenate([w_kr, _swap_halves(w_kr)], axis=1).astype(BF16)
    norm_g = jnp.concatenate([p["mla_q_norm_g"][l], p["mla_kv_norm_g"][l]])[None].astype(F32)
    g1 = p["norm1_g"][l][None].astype(F32)
    wq = p["mla_w_uq"][l].reshape(MLA_RANK, MLA_HEADS, MLA_NOPE + MLA_ROPE)
    wq = jnp.concatenate([wq, _swap_halves(wq[..., MLA_NOPE:])], axis=-1)
    wq = wq.reshape(MLA_RANK, MLA_HEADS * MLA_QK_PAD).astype(BF16)
    wkv = p["mla_w_ukv"][l].astype(BF16)
    cos_r, sin_r = _rope_tables(N_META + jnp.arange(s_len))
    cos_m, sin_m = _rope_tables(jnp.tile(jnp.arange(N_META), batch))
    bias_tiles = _na_bias_tiles(p["na_rel_bias"][l])

    proj_r, kr_r = _in_proj(x_real, g1, w_main, w_kr, norm_g, tm=tm_real)
    proj_m, kr_m = _in_proj(x_meta, g1, w_main, w_kr, norm_g, tm=tm_meta)
    q_r, k_r, v_r = _mla_up(proj_r, kr_r, cos_r, sin_r, wq, wkv, tm=_row_tile(s_len, 256))
    q_m, k_m, v_m = _mla_up(proj_m, kr_m, cos_m, sin_m, wq, wkv, tm=tm_meta)
    omla_r, omla_m = _mla_attn(q_r, k_r, v_r, q_m, k_m, v_m, batch=batch)
    ona_r, ona_m = _na_attn(proj_r, proj_m, bias_tiles, p["na_meta_bias"][l], batch=batch)

    w_na = p["w_na_branch"][l].astype(BF16)
    w_mla = p["w_mla_branch"][l].astype(BF16)
    w_out = p["w_out"][l].astype(BF16)
    h_real = _out_proj(_merge(ona_r, omla_r, w_na, w_mla, proj_r, tm=tm_real), w_out, x_real, tm=tm_real)
    h_meta = _out_proj(_merge(ona_m, omla_m, w_na, w_mla, proj_m, tm=tm_meta), w_out, x_meta, tm=tm_meta)
    return h_real, h_meta


def _peer(h2, p, l, gf, *, tm, te, final_norm):
    g2 = p["norm2_g"][l][None].astype(F32)
    w_q = p["peer_w_q"][l].astype(BF16)
    keys = p["peer_sub_keys"][l].astype(BF16)
    u = p["peer_u"][l].astype(BF16)
    vt = p["peer_v"][l].astype(BF16).T
    q2, hn2t = _peer_q(h2, g2, w_q, tm=_row_tile(h2.shape[0], 1024))
    tabs = _peer_route(q2, keys, tm=min(256, h2.shape[0]))
    return _peer_dense(hn2t, u, vt, tabs, h2, gf, tm=tm, te=te, final_norm=final_norm)


def kernel(x, meta_tokens, norm1_g, w_in, na_rel_bias, na_meta_bias, mla_q_norm_g, mla_w_uq, mla_kv_norm_g,
           mla_w_ukv, w_na_branch, w_mla_branch, w_out, norm2_g, peer_w_q, peer_sub_keys, peer_u, peer_v,
           final_norm_g):
    b, s, d = x.shape
    depth = w_in.shape[0]
    p = dict(norm1_g=norm1_g, w_in=w_in, na_rel_bias=na_rel_bias, na_meta_bias=na_meta_bias,
             mla_q_norm_g=mla_q_norm_g, mla_w_uq=mla_w_uq, mla_kv_norm_g=mla_kv_norm_g, mla_w_ukv=mla_w_ukv,
             w_na_branch=w_na_branch, w_mla_branch=w_mla_branch, w_out=w_out, norm2_g=norm2_g,
             peer_w_q=peer_w_q, peer_sub_keys=peer_sub_keys, peer_u=peer_u, peer_v=peer_v)
    x_real = x.reshape(b * s, d)
    x_meta = jnp.broadcast_to(meta_tokens.astype(x.dtype)[None], (b, N_META, d)).reshape(b * N_META, d)
    gf = final_norm_g[None].astype(F32)
    tm = _row_tile(b * s, 512)
    te = 512
    for l in range(depth):
        h_real, h_meta = _layer(x_real, x_meta, p, l, batch=b)
        x_real = _peer(h_real, p, l, gf, tm=tm, te=te, final_norm=l == depth - 1)
        x_meta = _peer(h_meta, p, l, gf, tm=h_meta.shape[0], te=te, final_norm=False)
    return x_real.reshape(b, s, d)
```

```python
import functools

import jax
import jax.numpy as jnp
import numpy as np
from jax import lax
from jax.experimental import pallas as pl
from jax.experimental.pallas import tpu as pltpu

GRID_W = 64
N_META = 16
NA_HEADS = 16
NA_HEAD_DIM = 64
NA_WIN_H = 8
NA_WIN_W = 16
NA_WIDTH = NA_HEADS * NA_HEAD_DIM
MLA_HEADS = 16
MLA_RANK = 512
MLA_NOPE = 128
MLA_ROPE = 64
MLA_V = 128
MLA_QK_PAD = 256
ROPE_THETA = 10000.0
PEER_HEADS = 8
PEER_NKEYS = 128
PEER_DK = 256
PEER_TOPK = 16
NORM_EPS = 1e-6
NEG_INF = -1e30

V7X_VMEM_BYTES = 64 * 1024 * 1024
V7X_VMEM_HEADROOM = 6 * 1024 * 1024
V7X_LANES = 128
BF16_ROWS = 16

NA_QROWS = 4
NA_KROWS = NA_QROWS + NA_WIN_H

F32 = jnp.float32
BF16 = jnp.bfloat16
_F32_MIN = float(np.finfo(np.float32).min)


def _vmem_limit(estimate_bytes):
    want = int(estimate_bytes * 1.25) + 4 * 1024 * 1024
    return min(max(want, 16 * 1024 * 1024), V7X_VMEM_BYTES - V7X_VMEM_HEADROOM)


def _params(semantics, vmem_estimate, flags=None):
    return pltpu.CompilerParams(dimension_semantics=semantics,
                                vmem_limit_bytes=_vmem_limit(vmem_estimate), flags=flags)


def _row_tile(rows, want):
    tile = min(rows, want)
    assert rows % tile == 0, (rows, tile)
    return tile


def _dot_nt(a, b):
    return lax.dot_general(a, b, (((1,), (1,)), ((), ())), preferred_element_type=F32)


def _rmsnorm(x, g):
    return x * lax.rsqrt(jnp.mean(x * x, axis=-1, keepdims=True) + NORM_EPS) * g


def _pad_meta_rows(a):
    return jnp.concatenate([a, jnp.zeros((V7X_LANES - N_META, a.shape[1]), a.dtype)], axis=0)


def _in_proj_kernel(x_ref, g_ref, w_ref, wkr_ref, ng_ref, o_ref, kr_ref, hn_ref, *, tn):
    j = pl.program_id(1)

    @pl.when(j == 0)
    def _():
        hn = _rmsnorm(x_ref[...], g_ref[...]).astype(BF16)
        hn_ref[...] = hn
        kr_ref[...] = jnp.dot(hn, wkr_ref[...], preferred_element_type=F32)

    acc = jnp.dot(hn_ref[...], w_ref[...], preferred_element_type=F32)
    col0 = j * tn
    plain_end = 3 * NA_WIDTH
    norm_end = plain_end + 2 * MLA_RANK

    @pl.when(col0 < plain_end)
    def _():
        scale = jnp.where(col0 < NA_WIDTH, NA_HEAD_DIM ** -0.5, 1.0).astype(F32)
        o_ref[...] = (acc * scale).astype(BF16)

    @pl.when(jnp.logical_and(col0 >= plain_end, col0 < norm_end))
    def _():
        for c in range(tn // MLA_RANK):
            sl = slice(c * MLA_RANK, (c + 1) * MLA_RANK)
            o_ref[:, sl] = _rmsnorm(acc[:, sl], ng_ref[:, sl]).astype(BF16)

    @pl.when(col0 >= norm_end)
    def _():
        o_ref[...] = jax.nn.sigmoid(acc).astype(BF16)


def _in_proj(x2d, g1, w_main, w_kr, norm_g, *, tm):
    r, d = x2d.shape
    n = w_main.shape[1]
    tn = 1024 if d % 1024 == 0 else 512
    norm_blk0 = (3 * NA_WIDTH) // tn
    norm_nblk = (2 * MLA_RANK) // tn
    est = 2 * (tm * d * 4 + d * tn * 2 + tm * tn * 2 + tm * 128 * 4 + d * 128 * 2) + tm * d * 2 + tm * tn * 4
    return pl.pallas_call(
        functools.partial(_in_proj_kernel, tn=tn),
        grid=(r // tm, n // tn),
        in_specs=[
            pl.BlockSpec((tm, d), lambda i, j: (i, 0)),
            pl.BlockSpec((1, d), lambda i, j: (0, 0)),
            pl.BlockSpec((d, tn), lambda i, j: (0, j)),
            pl.BlockSpec((d, 128), lambda i, j: (0, 0)),
            pl.BlockSpec((1, tn), lambda i, j: (0, jnp.clip(j - norm_blk0, 0, norm_nblk - 1))),
        ],
        out_specs=[
            pl.BlockSpec((tm, tn), lambda i, j: (i, j)),
            pl.BlockSpec((tm, 128), lambda i, j: (i, 0)),
        ],
        out_shape=[jax.ShapeDtypeStruct((r, n), BF16), jax.ShapeDtypeStruct((r, 128), F32)],
        scratch_shapes=[pltpu.VMEM((tm, d), BF16)],
        compiler_params=_params(("parallel", "arbitrary"), est),
        name="in_proj",
    )(x2d, g1, w_main, w_kr, norm_g)


def _rope_half(t, cos, sin):
    return t * cos + pltpu.roll(t, 64, axis=1) * sin


def _mla_up_kernel(cq_ref, ckv_ref, kr_ref, cos_ref, sin_ref, wq_ref, wkv_ref, q_ref, k_ref, v_ref):
    cos = cos_ref[...]
    sin = sin_ref[...]
    scale = (MLA_NOPE + MLA_ROPE) ** -0.5
    yq = jnp.dot(cq_ref[...], wq_ref[...], preferred_element_type=F32)
    ykv = jnp.dot(ckv_ref[...], wkv_ref[...], preferred_element_type=F32)
    k_rope = _rope_half(kr_ref[...], cos, sin).astype(BF16)
    for h in range(MLA_HEADS):
        c0 = h * MLA_QK_PAD
        q_ref[h, :, 0:MLA_NOPE] = (yq[:, c0:c0 + MLA_NOPE] * scale).astype(BF16)
        q_ref[h, :, MLA_NOPE:] = (_rope_half(yq[:, c0 + MLA_NOPE:c0 + MLA_QK_PAD], cos, sin) * scale).astype(BF16)
        k_ref[h, :, 0:MLA_NOPE] = ykv[:, c0:c0 + MLA_NOPE].astype(BF16)
        k_ref[h, :, MLA_NOPE:] = k_rope
        v_ref[h] = ykv[:, c0 + MLA_NOPE:c0 + MLA_NOPE + MLA_V].astype(BF16)


def _mla_up(proj, kr, cos_t, sin_t, wq_all, wkv_all, *, tm):
    r = proj.shape[0]
    cq_blk = (3 * NA_WIDTH) // MLA_RANK
    n_pos_blk = cos_t.shape[0] // tm
    width = wq_all.shape[1]
    est = (2 * (2 * tm * MLA_RANK * 2 + 3 * tm * 128 * 4 + 2 * MLA_RANK * width * 2
                + MLA_HEADS * tm * (2 * MLA_QK_PAD + MLA_V) * 2) + 3 * tm * width * 4)
    return pl.pallas_call(
        _mla_up_kernel,
        grid=(r // tm,),
        in_specs=[
            pl.BlockSpec((tm, MLA_RANK), lambda i: (i, cq_blk)),
            pl.BlockSpec((tm, MLA_RANK), lambda i: (i, cq_blk + 1)),
            pl.BlockSpec((tm, 128), lambda i: (i, 0)),
            pl.BlockSpec((tm, 128), lambda i: (i % n_pos_blk, 0)),
            pl.BlockSpec((tm, 128), lambda i: (i % n_pos_blk, 0)),
            pl.BlockSpec((MLA_RANK, width), lambda i: (0, 0)),
            pl.BlockSpec((MLA_RANK, width), lambda i: (0, 0)),
        ],
        out_specs=[
            pl.BlockSpec((MLA_HEADS, tm, MLA_QK_PAD), lambda i: (0, i, 0)),
            pl.BlockSpec((MLA_HEADS, tm, MLA_QK_PAD), lambda i: (0, i, 0)),
            pl.BlockSpec((MLA_HEADS, tm, MLA_V), lambda i: (0, i, 0)),
        ],
        out_shape=[
            jax.ShapeDtypeStruct((MLA_HEADS, r, MLA_QK_PAD), BF16),
            jax.ShapeDtypeStruct((MLA_HEADS, r, MLA_QK_PAD), BF16),
            jax.ShapeDtypeStruct((MLA_HEADS, r, MLA_V), BF16),
        ],
        compiler_params=_params(("parallel",), est),
        name="mla_up",
    )(proj, proj, kr, cos_t, sin_t, wq_all, wkv_all)


def _mla_attn_kernel(q_ref, k_ref, v_ref, qm_ref, km_ref, vm_ref, o_ref, om_ref, *, tq):
    k = k_ref[0]
    v = v_ref[0]
    km = _pad_meta_rows(km_ref[0])
    vm = _pad_meta_rows(vm_ref[0])
    meta_lane = lax.broadcasted_iota(jnp.int32, (1, V7X_LANES), 1) < N_META

    def attend(q):
        s = _dot_nt(q, k)
        sm = jnp.where(meta_lane, _dot_nt(q, km), NEG_INF)
        m = jnp.maximum(jnp.max(s, axis=-1, keepdims=True), jnp.max(sm, axis=-1, keepdims=True))
        p = jnp.exp(s - m)
        pm = jnp.exp(sm - m)
        l = jnp.sum(p, axis=-1, keepdims=True) + jnp.sum(pm, axis=-1, keepdims=True)
        o = (jnp.dot(p.astype(BF16), v, preferred_element_type=F32)
             + jnp.dot(pm.astype(BF16), vm, preferred_element_type=F32))
        return o / l

    s_len = q_ref.shape[1]
    for c in range(s_len // tq):
        rows = slice(c * tq, (c + 1) * tq)
        o_ref[rows, :] = attend(q_ref[0, rows, :]).astype(BF16)
    om_ref[...] = attend(qm_ref[0]).astype(BF16)


def _mla_attn(q, k, v, qm, km, vm, *, batch):
    r = q.shape[1]
    s_len = r // batch
    tq = min(512, s_len)
    est = 2 * (2 * s_len * 256 * 2 + 2 * s_len * 128 * 2) + 3 * tq * s_len * 4 + tq * s_len * 2
    return pl.pallas_call(
        functools.partial(_mla_attn_kernel, tq=tq),
        grid=(batch, MLA_HEADS),
        in_specs=[
            pl.BlockSpec((1, s_len, MLA_QK_PAD), lambda b, h: (h, b, 0)),
            pl.BlockSpec((1, s_len, MLA_QK_PAD), lambda b, h: (h, b, 0)),
            pl.BlockSpec((1, s_len, MLA_V), lambda b, h: (h, b, 0)),
            pl.BlockSpec((1, N_META, MLA_QK_PAD), lambda b, h: (h, b, 0)),
            pl.BlockSpec((1, N_META, MLA_QK_PAD), lambda b, h: (h, b, 0)),
            pl.BlockSpec((1, N_META, MLA_V), lambda b, h: (h, b, 0)),
        ],
        out_specs=[
            pl.BlockSpec((s_len, MLA_V), lambda b, h: (b, h)),
            pl.BlockSpec((N_META, MLA_V), lambda b, h: (b, h)),
        ],
        out_shape=[
            jax.ShapeDtypeStruct((r, MLA_HEADS * MLA_V), BF16),
            jax.ShapeDtypeStruct((batch * N_META, MLA_HEADS * MLA_V), BF16),
        ],
        compiler_params=_params(("parallel", "parallel"), est),
        name="mla_attn",
    )(q, k, v, qm, km, vm)


def _na_bias_tiles(rel_bias):
    qc = np.arange(GRID_W)
    kc = np.arange(GRID_W)
    cstart = np.clip(qc - NA_WIN_W // 2, 0, GRID_W - NA_WIN_W)
    col_ok = (kc[None, :] >= cstart[:, None]) & (kc[None, :] < cstart[:, None] + NA_WIN_W)
    col_off = np.clip(kc[None, :] - qc[:, None], -(NA_WIN_W - 1), NA_WIN_W - 1) + NA_WIN_W - 1
    n_co = 2 * NA_WIN_W - 1
    onehot = np.zeros((n_co, GRID_W * GRID_W), np.float32)
    onehot[col_off.reshape(-1), np.arange(GRID_W * GRID_W)] = 1.0
    h, n_ro = rel_bias.shape[0], rel_bias.shape[1]
    tiles = jnp.dot(rel_bias.astype(F32).reshape(h * n_ro, n_co), jnp.asarray(onehot),
                    precision=lax.Precision.HIGHEST).reshape(h, n_ro, GRID_W, GRID_W)
    return jnp.where(jnp.asarray(col_ok)[None, None], tiles, NEG_INF)


def _na_block_plan(rows):
    nqb = rows // NA_QROWS
    wh = NA_WIN_H
    plan = []
    for qb in (0, 1, nqb - 1):
        kstart = int(np.clip(NA_QROWS * qb - wh // 2, 0, rows - NA_KROWS))
        variant = []
        for i in range(NA_QROWS):
            r = NA_QROWS * qb + i
            rs = int(np.clip(r - wh // 2, 0, rows - wh))
            variant.append([kstart + j - r + wh - 1 if rs <= kstart + j < rs + wh else None
                            for j in range(NA_KROWS)])
        plan.append(variant)
    return plan


def _na_attn_kernel(q_ref, k_ref, v_ref, qm_ref, km_ref, vm_ref, tiles_ref, mb_ref, o_ref, om_ref, bias_ref,
                    *, rows):
    nqb = rows // NA_QROWS
    qblk = NA_QROWS * GRID_W
    kblk = NA_KROWS * GRID_W

    @pl.when(pl.program_id(1) == 0)
    def _():
        masked = jnp.full((GRID_W, GRID_W), NEG_INF, F32)
        for hh in range(2):
            for v, variant in enumerate(_na_block_plan(rows)):
                for i, row_plan in enumerate(variant):
                    for j0 in range(0, NA_KROWS, 2):
                        pair = [masked if ro is None else tiles_ref[0, hh, ro] for ro in row_plan[j0:j0 + 2]]
                        bias_ref[hh, v, i * GRID_W:(i + 1) * GRID_W, j0 * GRID_W:(j0 + 2) * GRID_W] = (
                            jnp.concatenate(pair, axis=1))

    km = _pad_meta_rows(km_ref[...])
    vm = _pad_meta_rows(vm_ref[...])
    lane = lax.broadcasted_iota(jnp.int32, (1, 2 * NA_HEAD_DIM), 1)
    head_lanes = [lane < NA_HEAD_DIM, lane >= NA_HEAD_DIM]

    def softmax_pv(s, sm, v_loc):
        m = jnp.maximum(jnp.max(s, axis=-1, keepdims=True), jnp.max(sm, axis=-1, keepdims=True))
        p = jnp.exp(s - m)
        pm = jnp.exp(sm - m)
        l = jnp.sum(p, axis=-1, keepdims=True) + jnp.sum(pm, axis=-1, keepdims=True)
        o = (jnp.dot(p.astype(BF16), v_loc, preferred_element_type=F32)
             + jnp.dot(pm.astype(BF16), vm, preferred_element_type=F32))
        return o / l

    def block(qb, carry):
        q0 = pl.multiple_of(qb * qblk, qblk)
        k0 = pl.multiple_of(jnp.clip(NA_QROWS * qb - NA_WIN_H // 2, 0, rows - NA_KROWS) * GRID_W, GRID_W)
        variant = jnp.where(qb == 0, 0, jnp.where(qb == nqb - 1, 2, 1))
        q = q_ref[pl.ds(q0, qblk), :]
        k_loc = k_ref[pl.ds(k0, kblk), :]
        v_loc = v_ref[pl.ds(k0, kblk), :]
        outs = []
        for hh in range(2):
            qh = jnp.where(head_lanes[hh], q, jnp.zeros_like(q))
            s = _dot_nt(qh, k_loc) + bias_ref[hh, variant]
            sm = _dot_nt(qh, km) + mb_ref[0, hh]
            outs.append(softmax_pv(s, sm, v_loc))
        o_ref[pl.ds(q0, qblk), :] = jnp.where(head_lanes[0], outs[0], outs[1]).astype(BF16)
        return carry

    lax.fori_loop(0, nqb, block, 0)

    qm = qm_ref[...]
    outs = []
    for hh in range(2):
        qh = jnp.where(head_lanes[hh], qm, jnp.zeros_like(qm))
        sm = _dot_nt(qh, km) + mb_ref[0, hh]
        m = jnp.max(sm, axis=-1, keepdims=True)
        pm = jnp.exp(sm - m)
        l = jnp.sum(pm, axis=-1, keepdims=True)
        outs.append(jnp.dot(pm.astype(BF16), vm, preferred_element_type=F32) / l)
    om_ref[...] = jnp.where(head_lanes[0], outs[0], outs[1]).astype(BF16)


def _na_attn(proj, proj_m, bias_tiles, meta_bias, *, batch):
    r = proj.shape[0]
    s_len = r // batch
    rows = s_len // GRID_W
    assert rows % NA_QROWS == 0 and rows >= NA_KROWS, rows
    npair = NA_HEADS // 2
    qblk = NA_QROWS * GRID_W
    kblk = NA_KROWS * GRID_W
    n_ro = bias_tiles.shape[1]
    tiles5 = bias_tiles.reshape(npair, 2, n_ro, GRID_W, GRID_W)
    mb4 = jnp.concatenate([meta_bias.astype(F32), jnp.full((NA_HEADS, V7X_LANES - N_META), NEG_INF, F32)],
                          axis=1).reshape(npair, 2, 1, V7X_LANES)
    est = 2 * (3 * s_len * 128 * 2 + 6 * qblk * kblk * 4 + s_len * 128 * 2) + 6 * qblk * kblk * 4
    return pl.pallas_call(
        functools.partial(_na_attn_kernel, rows=rows),
        grid=(npair, batch),
        in_specs=[
            pl.BlockSpec((s_len, 128), lambda p, b: (b, p)),
            pl.BlockSpec((s_len, 128), lambda p, b: (b, npair + p)),
            pl.BlockSpec((s_len, 128), lambda p, b: (b, 2 * npair + p)),
            pl.BlockSpec((N_META, 128), lambda p, b: (b, p)),
            pl.BlockSpec((N_META, 128), lambda p, b: (b, npair + p)),
            pl.BlockSpec((N_META, 128), lambda p, b: (b, 2 * npair + p)),
            pl.BlockSpec((1, 2, n_ro, GRID_W, GRID_W), lambda p, b: (p, 0, 0, 0, 0)),
            pl.BlockSpec((1, 2, 1, V7X_LANES), lambda p, b: (p, 0, 0, 0)),
        ],
        out_specs=[
            pl.BlockSpec((s_len, 128), lambda p, b: (b, p)),
            pl.BlockSpec((N_META, 128), lambda p, b: (b, p)),
        ],
        out_shape=[
            jax.ShapeDtypeStruct((r, NA_WIDTH), BF16),
            jax.ShapeDtypeStruct((batch * N_META, NA_WIDTH), BF16),
        ],
        scratch_shapes=[pltpu.VMEM((2, 3, qblk, kblk), F32)],
        compiler_params=_params(("parallel", "arbitrary"), est),
        name="na_attn",
    )(proj, proj, proj, proj_m, proj_m, proj_m, tiles5, mb4)


def _merge_kernel(ona_ref, omla_ref, wna_ref, wmla_ref, ga_ref, gb_ref, o_ref):
    a = jnp.dot(ona_ref[...], wna_ref[...], preferred_element_type=F32)
    b = jnp.dot(omla_ref[...], wmla_ref[...], preferred_element_type=F32)
    o_ref[...] = (ga_ref[...].astype(F32) * a + gb_ref[...].astype(F32) * b).astype(BF16)


def _merge(o_na, o_mla, w_na, w_mla, proj, *, tm):
    r = o_na.shape[0]
    d = w_na.shape[1]
    tn = 1024 if d % 1024 == 0 else 512
    ga_blk = (3 * NA_WIDTH + 2 * MLA_RANK) // tn
    gb_blk = ga_blk + d // tn
    est = 2 * (tm * (NA_WIDTH + MLA_HEADS * MLA_V) * 2 + (NA_WIDTH + MLA_HEADS * MLA_V) * tn * 2
               + 3 * tm * tn * 2) + 3 * tm * tn * 4
    return pl.pallas_call(
        _merge_kernel,
        grid=(r // tm, d // tn),
        in_specs=[
            pl.BlockSpec((tm, NA_WIDTH), lambda i, j: (i, 0)),
            pl.BlockSpec((tm, MLA_HEADS * MLA_V), lambda i, j: (i, 0)),
            pl.BlockSpec((NA_WIDTH, tn), lambda i, j: (0, j)),
            pl.BlockSpec((MLA_HEADS * MLA_V, tn), lambda i, j: (0, j)),
            pl.BlockSpec((tm, tn), lambda i, j: (i, ga_blk + j)),
            pl.BlockSpec((tm, tn), lambda i, j: (i, gb_blk + j)),
        ],
        out_specs=pl.BlockSpec((tm, tn), lambda i, j: (i, j)),
        out_shape=jax.ShapeDtypeStruct((r, d), BF16),
        compiler_params=_params(("parallel", "arbitrary"), est),
        name="merge",
    )(o_na, o_mla, w_na, w_mla, proj, proj)


def _out_proj_kernel(m_ref, w_ref, x_ref, o_ref):
    o_ref[...] = x_ref[...] + jnp.dot(m_ref[...], w_ref[...], preferred_element_type=F32)


def _out_proj(merged, w_out, x2d, *, tm):
    r, d = x2d.shape
    tn = 1024 if d % 1024 == 0 else 512
    est = 2 * (tm * d * 2 + d * tn * 2 + 2 * tm * tn * 4) + tm * tn * 4
    return pl.pallas_call(
        _out_proj_kernel,
        grid=(r // tm, d // tn),
        in_specs=[
            pl.BlockSpec((tm, d), lambda i, j: (i, 0)),
            pl.BlockSpec((d, tn), lambda i, j: (0, j)),
            pl.BlockSpec((tm, tn), lambda i, j: (i, j)),
        ],
        out_specs=pl.BlockSpec((tm, tn), lambda i, j: (i, j)),
        out_shape=jax.ShapeDtypeStruct((r, d), F32),
        compiler_params=_params(("parallel", "arbitrary"), est),
        name="out_proj",
    )(merged, w_out, x2d)


def _peer_q_kernel(h_ref, g_ref, w_ref, q_ref, hnt_ref, hn_ref):
    @pl.when(pl.program_id(1) == 0)
    def _():
        hn = _rmsnorm(h_ref[...], g_ref[...])
        hn_ref[...] = hn.astype(BF16)
        hnt_ref[...] = hn.T.astype(BF16)

    q_ref[...] = jnp.dot(hn_ref[...], w_ref[...], preferred_element_type=F32).astype(BF16)


def _peer_q(h2, g2, w_q, *, tm):
    r, d = h2.shape
    n = w_q.shape[1]
    tn = 1024
    est = 2 * (tm * d * 4 + d * tn * 2 + tm * tn * 2 + tm * d * 2) + tm * d * 2 + tm * tn * 4 + 2 * tm * d * 4
    return pl.pallas_call(
        _peer_q_kernel,
        grid=(r // tm, n // tn),
        in_specs=[
            pl.BlockSpec((tm, d), lambda i, j: (i, 0)),
            pl.BlockSpec((1, d), lambda i, j: (0, 0)),
            pl.BlockSpec((d, tn), lambda i, j: (0, j)),
        ],
        out_specs=[
            pl.BlockSpec((tm, tn), lambda i, j: (i, j)),
            pl.BlockSpec((d, tm), lambda i, j: (0, i)),
        ],
        out_shape=[jax.ShapeDtypeStruct((r, n), BF16), jax.ShapeDtypeStruct((d, r), BF16)],
        scratch_shapes=[pltpu.VMEM((tm, d), BF16)],
        compiler_params=_params(("parallel", "arbitrary"), est),
        name="peer_q",
    )(h2, g2, w_q)


def _compare_exchange(a, b):
    if a is None:
        return b, None
    if b is None:
        return a, None
    return jnp.maximum(a, b), jnp.minimum(a, b)


def _bitonic_merge(xs):
    n = len(xs)
    step = n // 2
    while step >= 1:
        for i in range(n):
            if i & step == 0:
                xs[i], xs[i + step] = _compare_exchange(xs[i], xs[i + step])
        step //= 2
    return xs


def _bitonic_sort(xs):
    n = len(xs)
    xs = list(xs)
    size = 2
    while size <= n:
        step = size // 2
        while step >= 1:
            for i in range(n):
                if i & step == 0:
                    hi, lo = _compare_exchange(xs[i], xs[i + step])
                    xs[i], xs[i + step] = (hi, lo) if i & size == 0 else (lo, hi)
            step //= 2
        size *= 2
    return xs


def _merge_topk(xs, ys, k):
    xs = list(xs) + [None] * (k - len(xs))
    ys = list(ys) + [None] * (k - len(ys))
    return _bitonic_merge([_compare_exchange(xs[i], ys[k - 1 - i])[0] for i in range(k)])


def _pow2_ceil(n):
    return 1 << (n - 1).bit_length()


def _sorted_topk_slabs(s, k):
    slabs = _bitonic_sort([s[8 * v:8 * v + 8, :] for v in range(s.shape[0] // 8)])
    for shift in (4, 2, 1):
        slabs = _merge_topk(slabs, [pltpu.roll(x, shift, axis=0) for x in slabs], k)
    return slabs


def _extract_topk(x, k):
    n, t = x.shape
    row = lax.broadcasted_iota(jnp.int32, (n, t), 0).astype(F32)
    krow = lax.broadcasted_iota(jnp.int32, (k, t), 0)

    def body(i, carry):
        x, rank, vals = carry
        m = jnp.max(x, axis=0, keepdims=True)
        first = jnp.min(jnp.where(x == m, row, float(n)), axis=0, keepdims=True)
        hit = row == first
        rank = jnp.where(hit, jnp.asarray(i, F32), rank)
        x = jnp.where(hit, _F32_MIN, x)
        vals = jnp.where(krow == i, m, vals)
        return x, rank, vals

    _, rank, vals = lax.fori_loop(
        0, k, body, (x, jnp.full((n, t), float(k), F32), jnp.zeros((k, t), F32)))
    return vals, rank


def _route_head_exact(s1, s2):
    k = PEER_TOPK
    top1, rank1 = _extract_topk(s1, k)
    top2, rank2 = _extract_topk(s2, k)

    tm = s1.shape[1]
    pieces, spans = [], []
    off = 0
    for i in range(k):
        cnt = k // (i + 1)
        rows = -(-cnt // 8) * 8
        jrow = lax.broadcasted_iota(jnp.int32, (rows, tm), 0)
        pieces.append(jnp.where(jrow < cnt, top1[i:i + 1, :] + top2[0:rows, :], _F32_MIN))
        spans.append((off, rows))
        off += rows
    cand = jnp.concatenate(pieces, axis=0)
    ctop, crank = _extract_topk(cand, k)
    chosen = crank < float(k)
    cmax = ctop[0:1, :]
    z = jnp.sum(jnp.where(chosen, jnp.exp(cand - cmax), 0.0), axis=0, keepdims=True)

    y1 = jnp.ones(s1.shape, F32)
    for i, (off, rows) in enumerate(spans):
        n_sel = jnp.sum(jnp.where(chosen[off:off + rows, :], 1.0, 0.0), axis=0, keepdims=True)
        y1 = jnp.where(rank1 == float(i), 1.0 - n_sel, y1)
    return y1, jnp.exp(s1 - top1[0:1, :]) / z, -rank2, jnp.exp(s2 - top2[0:1, :])


def _route_tables_by_value(scores):
    k = PEER_TOPK
    tm = scores[0][0].shape[1]

    sublane = lax.broadcasted_iota(jnp.int32, (8, tm), 0)
    tops = []
    for side in range(2):
        packed = None
        for h in range(PEER_HEADS):
            slabs = _sorted_topk_slabs(scores[h][side], k)
            packed = slabs if packed is None else [jnp.where(sublane == h, s, p) for s, p in zip(slabs, packed)]
        tops.append(packed)
    top1, top2 = tops

    cells = {(i, j): top1[i] + top2[j] for i in range(k) for j in range(k // (i + 1))}
    grid_rows = [[cells[i, j] for j in range(k // (i + 1))] for i in range(k)]
    lists = [r for r in grid_rows[1:] if len(r) > 1] + [[r[0] for r in grid_rows if len(r) == 1]]
    while len(lists) > 1:
        lists.sort(key=len)
        size = min(k, _pow2_ceil(len(lists[0]) + len(lists[1])))
        merged = [m for m in _merge_topk(lists[0], lists[1], size) if m is not None]
        lists = [merged] + lists[2:]
    ctop = _merge_topk(grid_rows[0], lists[0], k)
    thr, cmax = ctop[k - 1], ctop[0]

    taken = {c: v >= thr for c, v in cells.items()}
    n_taken = sum(jnp.where(t, 1.0, 0.0) for t in taken.values())
    z = sum(jnp.where(taken[c], jnp.exp(v - cmax), 0.0) for c, v in cells.items())
    inv_z = 1.0 / z
    y1_rows = [1.0 - sum(jnp.where(taken[i, j], 1.0, 0.0) for j in range(k // (i + 1))) for i in range(k)]

    exact = jnp.where(n_taken == float(k), 1.0, 0.0)
    for top in (top1, top2):
        for i in range(k - 1):
            exact = jnp.where(top[i] > top[i + 1], exact, 0.0)

    tables = []
    for h in range(PEER_HEADS):
        s1, s2 = scores[h]
        row = slice(h, h + 1)
        y1 = jnp.ones(s1.shape, F32)
        for i in range(k):
            y1 = jnp.where(s1 == top1[i][row], y1_rows[i][row], y1)
        y2 = jnp.full(s2.shape, -float(k), F32)
        for j in reversed(range(k)):
            y2 = jnp.where(s2 >= top2[j][row], -float(j), y2)
        tables.append((y1, jnp.exp(s1 - top1[0][row]) * inv_z[row], y2, jnp.exp(s2 - top2[0][row])))
        for s, top in ((s1, top1), (s2, top2)):
            members = jnp.sum(jnp.where(s >= top[k - 1][row], 1.0, 0.0), axis=0, keepdims=True)
            exact = jnp.where(jnp.logical_and(sublane == h, members != float(k)), 0.0, exact)
    return tables, exact


def _peer_route_kernel(q_ref, keys_ref, y1_ref, e1_ref, y2_ref, e2_ref):
    half = PEER_DK // 2
    scores = []
    for h in range(PEER_HEADS):
        q = q_ref[:, h * PEER_DK:(h + 1) * PEER_DK]
        scores.append((_dot_nt(keys_ref[h, 0], q[:, :half]),
                       _dot_nt(keys_ref[h, 1], q[:, half:])))

    def store(h, tables):
        y1, e1, y2, e2 = tables
        y1_ref[h] = y1
        e1_ref[h] = e1
        y2_ref[h] = pltpu.bitcast(y2.astype(BF16), jnp.uint32)
        e2_ref[h] = pltpu.bitcast(e2.astype(BF16), jnp.uint32)

    tables, exact = _route_tables_by_value(scores)
    for h in range(PEER_HEADS):
        store(h, tables[h])

    for h in range(PEER_HEADS):
        @pl.when(jnp.min(exact[h:h + 1, :]) < 0.5)
        def _(h=h):
            store(h, _route_head_exact(*scores[h]))


def _peer_route(q2, sub_keys, *, tm):
    r = q2.shape[0]
    width = PEER_HEADS * PEER_DK
    tab_spec = pl.BlockSpec((PEER_HEADS, PEER_NKEYS, tm), lambda i: (0, 0, i))
    tab = jax.ShapeDtypeStruct((PEER_HEADS, PEER_NKEYS, r), F32)
    packed_spec = pl.BlockSpec((PEER_HEADS, PEER_NKEYS // 2, tm), lambda i: (0, 0, i))
    packed = jax.ShapeDtypeStruct((PEER_HEADS, PEER_NKEYS // 2, r), jnp.uint32)
    est = (2 * (tm * width * 2 + PEER_HEADS * PEER_NKEYS * PEER_DK * 2 + PEER_HEADS * PEER_NKEYS * tm * 16)
           + 40 * PEER_HEADS * PEER_NKEYS * tm * 4)
    return pl.pallas_call(
        _peer_route_kernel,
        grid=(r // tm,),
        in_specs=[
            pl.BlockSpec((tm, width), lambda i: (i, 0)),
            pl.BlockSpec((PEER_HEADS, 2, PEER_NKEYS, PEER_DK // 2), lambda i: (0, 0, 0, 0)),
        ],
        out_specs=[tab_spec, tab_spec, packed_spec, packed_spec],
        out_shape=[tab, tab, packed, packed],
        compiler_params=_params(("parallel",), est),
        name="peer_route",
    )(q2, sub_keys)


def _peer_dense_kernel(hnt_ref, u_ref, vt_ref, y1_ref, e1_ref, y2_ref, e2_ref, h_ref, g_ref, o_ref,
                       acc_ref, act_ref, p_ref, *, te, final_norm):
    j = pl.program_id(1)
    d, tm = acc_ref.shape

    @pl.when(j == 0)
    def _():
        acc_ref[...] = jnp.zeros_like(acc_ref)

    act = jax.nn.gelu(jnp.dot(u_ref[...], hnt_ref[...], preferred_element_type=F32).astype(BF16))
    act_ref[...] = pltpu.bitcast(act, jnp.uint32)
    a_blocks = te // PEER_NKEYS
    a0 = j * a_blocks
    zero = jnp.zeros((), BF16)
    for ai in range(a_blocks):
        y1_rows = [jnp.broadcast_to(y1_ref[h, pl.ds(a0 + ai, 1), :], (BF16_ROWS, tm)).astype(BF16)
                   for h in range(PEER_HEADS)]
        e1_rows = [jnp.broadcast_to(e1_ref[h, pl.ds(a0 + ai, 1), :], (BF16_ROWS, tm)).astype(BF16)
                   for h in range(PEER_HEADS)]
        for c in range(tm // V7X_LANES):
            cols = slice(c * V7X_LANES, (c + 1) * V7X_LANES)
            for g in range(PEER_NKEYS // BF16_ROWS):
                b_words = slice(g * BF16_ROWS // 2, (g + 1) * BF16_ROWS // 2)
                gate = None
                for h in range(PEER_HEADS):
                    sel = pltpu.bitcast(y2_ref[h, b_words, cols], BF16) >= y1_rows[h][:, cols]
                    term = jnp.where(sel, e1_rows[h][:, cols] * pltpu.bitcast(e2_ref[h, b_words, cols], BF16), zero)
                    gate = term if gate is None else gate + term
                e_words = slice((ai * PEER_NKEYS + g * BF16_ROWS) // 2, (ai * PEER_NKEYS + (g + 1) * BF16_ROWS) // 2)
                p_ref[e_words, cols] = pltpu.bitcast(gate * pltpu.bitcast(act_ref[e_words, cols], BF16), jnp.uint32)
    acc_ref[...] += jnp.dot(vt_ref[...], pltpu.bitcast(p_ref[...], BF16), preferred_element_type=F32)

    @pl.when(j == pl.num_programs(1) - 1)
    def _():
        out = h_ref[...] + acc_ref[...].T
        o_ref[...] = _rmsnorm(out, g_ref[...]) if final_norm else out


def _peer_dense(hn2t, u, vt, tabs, h2, gf, *, tm, te, final_norm):
    d, r = hn2t.shape
    e = u.shape[0]
    y1, e1, y2, e2 = tabs
    tab_spec = pl.BlockSpec((PEER_HEADS, PEER_NKEYS, tm), lambda i, j: (0, 0, i))
    packed_spec = pl.BlockSpec((PEER_HEADS, PEER_NKEYS // 2, tm), lambda i, j: (0, 0, i))
    est = (2 * (tm * d * 2 + 2 * te * d * 2 + PEER_HEADS * PEER_NKEYS * tm * 12 + tm * d * 4)
           + tm * d * 4 + d * tm * 4 + 6 * te * tm * 4)
    return pl.pallas_call(
        functools.partial(_peer_dense_kernel, te=te, final_norm=final_norm),
        grid=(r // tm, e // te),
        in_specs=[
            pl.BlockSpec((d, tm), lambda i, j: (0, i)),
            pl.BlockSpec((te, d), lambda i, j: (j, 0)),
            pl.BlockSpec((d, te), lambda i, j: (0, j)),
            tab_spec, tab_spec, packed_spec, packed_spec,
            pl.BlockSpec((tm, d), lambda i, j: (i, 0), pipeline_mode=pl.Buffered(1)),
            pl.BlockSpec((1, d), lambda i, j: (0, 0)),
        ],
        out_specs=pl.BlockSpec((tm, d), lambda i, j: (i, 0)),
        out_shape=jax.ShapeDtypeStruct((r, d), F32),
        scratch_shapes=[pltpu.VMEM((d, tm), F32), pltpu.VMEM((te // 2, tm), jnp.uint32),
                        pltpu.VMEM((te // 2, tm), jnp.uint32)],
        compiler_params=_params(("parallel", "arbitrary"), est),
        name="peer_dense",
    )(hn2t, u, vt, y1, e1, y2, e2, h2, gf)


def _rope_tables(pos):
    inv_freq = ROPE_THETA ** (-jnp.arange(0, MLA_ROPE, 2, dtype=F32) / MLA_ROPE)
    ang = pos.astype(F32)[:, None] * inv_freq[None, :]
    cos, sin = jnp.cos(ang), jnp.sin(ang)
    zeros = jnp.zeros((pos.shape[0], V7X_LANES - MLA_ROPE), F32)
    return (jnp.concatenate([cos, cos, zeros], axis=1), jnp.concatenate([-sin, sin, zeros], axis=1))


def _swap_halves(w):
    half = w.shape[-1] // 2
    return jnp.concatenate([w[..., half:], w[..., :half]], axis=-1)


def _layer(x_real, x_meta, p, l, *, batch):
    d = x_real.shape[1]
    s_len = x_real.shape[0] // batch
    tm_real = _row_tile(x_real.shape[0], 1024)
    tm_meta = x_meta.shape[0]

    w_in = p["w_in"][l]
    kr0 = 3 * NA_WIDTH + 2 * MLA_RANK
    w_main = jnp.concatenate([w_in[:, :kr0], w_in[:, kr0 + MLA_ROPE:]], axis=1).astype(BF16)
    w_kr = w_in[:, kr0:kr0 + MLA_ROPE]
    w_kr = jnp.concatenate([w_kr, _swap_halves(w_kr)], axis=1).astype(BF16)
    norm_g = jnp.concatenate([p["mla_q_norm_g"][l], p["mla_kv_norm_g"][l]])[None].astype(F32)
    g1 = p["norm1_g"][l][None].astype(F32)
    wq = p["mla_w_uq"][l].reshape(MLA_RANK, MLA_HEADS, MLA_NOPE + MLA_ROPE)
    wq = jnp.concatenate([wq, _swap_halves(wq[..., MLA_NOPE:])], axis=-1)
    wq = wq.reshape(MLA_RANK, MLA_HEADS * MLA_QK_PAD).astype(BF16)
    wkv = p["mla_w_ukv"][l].astype(BF16)
    cos_r, sin_r = _rope_tables(N_META + jnp.arange(s_len))
    cos_m, sin_m = _rope_tables(jnp.tile(jnp.arange(N_META), batch))
    bias_tiles = _na_bias_tiles(p["na_rel_bias"][l])

    proj_r, kr_r = _in_proj(x_real, g1, w_main, w_kr, norm_g, tm=tm_real)
    proj_m, kr_m = _in_proj(x_meta, g1, w_main, w_kr, norm_g, tm=tm_meta)
    q_r, k_r, v_r = _mla_up(proj_r, kr_r, cos_r, sin_r, wq, wkv, tm=_row_tile(s_len, 256))
    q_m, k_m, v_m = _mla_up(proj_m, kr_m, cos_m, sin_m, wq, wkv, tm=tm_meta)
    omla_r, omla_m = _mla_attn(q_r, k_r, v_r, q_m, k_m, v_m, batch=batch)
    ona_r, ona_m = _na_attn(proj_r, proj_m, bias_tiles, p["na_meta_bias"][l], batch=batch)

    w_na = p["w_na_branch"][l].astype(BF16)
    w_mla = p["w_mla_branch"][l].astype(BF16)
    w_out = p["w_out"][l].astype(BF16)
    h_real = _out_proj(_merge(ona_r, omla_r, w_na, w_mla, proj_r, tm=tm_real), w_out, x_real, tm=tm_real)
    h_meta = _out_proj(_merge(ona_m, omla_m, w_na, w_mla, proj_m, tm=tm_meta), w_out, x_meta, tm=tm_meta)
    return h_real, h_meta


def _peer(h2, p, l, gf, *, tm, te, final_norm):
    g2 = p["norm2_g"][l][None].astype(F32)
    w_q = p["peer_w_q"][l].astype(BF16)
    keys = p["peer_sub_keys"][l].astype(BF16)
    u = p["peer_u"][l].astype(BF16)
    vt = p["peer_v"][l].astype(BF16).T
    q2, hn2t = _peer_q(h2, g2, w_q, tm=_row_tile(h2.shape[0], 1024))
    tabs = _peer_route(q2, keys, tm=min(256, h2.shape[0]))
    return _peer_dense(hn2t, u, vt, tabs, h2, gf, tm=tm, te=te, final_norm=final_norm)


def kernel(x, meta_tokens, norm1_g, w_in, na_rel_bias, na_meta_bias, mla_q_norm_g, mla_w_uq, mla_kv_norm_g,
           mla_w_ukv, w_na_branch, w_mla_branch, w_out, norm2_g, peer_w_q, peer_sub_keys, peer_u, peer_v,
           final_norm_g):
    b, s, d = x.shape
    depth = w_in.shape[0]
    p = dict(norm1_g=norm1_g, w_in=w_in, na_rel_bias=na_rel_bias, na_meta_bias=na_meta_bias,
             mla_q_norm_g=mla_q_norm_g, mla_w_uq=mla_w_uq, mla_kv_norm_g=mla_kv_norm_g, mla_w_ukv=mla_w_ukv,
             w_na_branch=w_na_branch, w_mla_branch=w_mla_branch, w_out=w_out, norm2_g=norm2_g,
             peer_w_q=peer_w_q, peer_sub_keys=peer_sub_keys, peer_u=peer_u, peer_v=peer_v)
    x_real = x.reshape(b * s, d)
    x_meta = jnp.broadcast_to(meta_tokens.astype(x.dtype)[None], (b, N_META, d)).reshape(b * N_META, d)
    gf = final_norm_g[None].astype(F32)
    tm = _row_tile(b * s, 512)
    te = 1024
    for l in range(depth):
        h_real, h_meta = _layer(x_real, x_meta, p, l, batch=b)
        x_real = _peer(h_real, p, l, gf, tm=tm, te=te, final_norm=l == depth - 1)
        x_meta = _peer(h_meta, p, l, gf, tm=h_meta.shape[0], te=te, final_norm=False)
    return x_real.reshape(b, s, d)
```

```python
import functools

import jax
import jax.numpy as jnp
import numpy as np
from jax import lax
from jax.experimental import pallas as pl
from jax.experimental.pallas import tpu as pltpu

GRID_W = 64
N_META = 16
NA_HEADS = 16
NA_HEAD_DIM = 64
NA_WIN_H = 8
NA_WIN_W = 16
NA_WIDTH = NA_HEADS * NA_HEAD_DIM
MLA_HEADS = 16
MLA_RANK = 512
MLA_NOPE = 128
MLA_ROPE = 64
MLA_V = 128
MLA_QK_PAD = 256
ROPE_THETA = 10000.0
PEER_HEADS = 8
PEER_NKEYS = 128
PEER_DK = 256
PEER_TOPK = 16
NORM_EPS = 1e-6
NEG_INF = -1e30

V7X_VMEM_BYTES = 64 * 1024 * 1024
V7X_VMEM_HEADROOM = 6 * 1024 * 1024
V7X_LANES = 128
BF16_ROWS = 16

NA_QROWS = 4
NA_KROWS = NA_QROWS + NA_WIN_H
NA_PAIR_WIDTH = 2 * NA_HEAD_DIM
KR_WIDTH = 2 * MLA_ROPE

ROW_TILE = 1024
COL_TILE = 1024
MLA_UP_TILE = 256
MLA_Q_CHUNK = 512
ROUTE_TILE = 256
EXPERT_TOKEN_TILE = 512
EXPERT_BLOCK = 1024

F32 = jnp.float32
BF16 = jnp.bfloat16
_F32_MIN = float(np.finfo(np.float32).min)


def _vmem_limit(estimate_bytes):
    want = int(estimate_bytes * 1.25) + 4 * 1024 * 1024
    return min(max(want, 16 * 1024 * 1024), V7X_VMEM_BYTES - V7X_VMEM_HEADROOM)


def _params(semantics, vmem_estimate):
    return pltpu.CompilerParams(dimension_semantics=semantics, vmem_limit_bytes=_vmem_limit(vmem_estimate))


def _row_tile(rows, want):
    tile = min(rows, want)
    assert rows % tile == 0, (rows, tile)
    return tile


def _col_tile(d):
    return COL_TILE if d % COL_TILE == 0 else COL_TILE // 2


def _dot_nt(a, b):
    return lax.dot_general(a, b, (((1,), (1,)), ((), ())), preferred_element_type=F32)


def _rmsnorm(x, g):
    return x * lax.rsqrt(jnp.mean(x * x, axis=-1, keepdims=True) + NORM_EPS) * g


def _pad_meta_rows(a):
    return jnp.concatenate([a, jnp.zeros((V7X_LANES - N_META, a.shape[1]), a.dtype)], axis=0)


def _in_proj_kernel(x_ref, g_ref, w_ref, wkr_ref, ng_ref, o_ref, kr_ref, hn_ref, *, tn):
    j = pl.program_id(1)

    @pl.when(j == 0)
    def _():
        hn = _rmsnorm(x_ref[...], g_ref[...]).astype(BF16)
        hn_ref[...] = hn
        kr_ref[...] = jnp.dot(hn, wkr_ref[...], preferred_element_type=F32)

    acc = jnp.dot(hn_ref[...], w_ref[...], preferred_element_type=F32)
    col0 = j * tn
    plain_end = 3 * NA_WIDTH
    norm_end = plain_end + 2 * MLA_RANK

    @pl.when(col0 < plain_end)
    def _():
        scale = jnp.where(col0 < NA_WIDTH, NA_HEAD_DIM ** -0.5, 1.0).astype(F32)
        o_ref[...] = (acc * scale).astype(BF16)

    @pl.when(jnp.logical_and(col0 >= plain_end, col0 < norm_end))
    def _():
        for c in range(tn // MLA_RANK):
            sl = slice(c * MLA_RANK, (c + 1) * MLA_RANK)
            o_ref[:, sl] = _rmsnorm(acc[:, sl], ng_ref[:, sl]).astype(BF16)

    @pl.when(col0 >= norm_end)
    def _():
        o_ref[...] = jax.nn.sigmoid(acc).astype(BF16)


def _in_proj(x2d, g1, w_main, w_kr, norm_g, *, tm):
    r, d = x2d.shape
    n = w_main.shape[1]
    tn = _col_tile(d)
    norm_blk0 = (3 * NA_WIDTH) // tn
    norm_nblk = (2 * MLA_RANK) // tn
    est = 2 * (tm * d * 4 + d * tn * 2 + tm * tn * 2 + tm * KR_WIDTH * 4 + d * KR_WIDTH * 2) + tm * d * 2 + tm * tn * 4
    return pl.pallas_call(
        functools.partial(_in_proj_kernel, tn=tn),
        grid=(r // tm, n // tn),
        in_specs=[
            pl.BlockSpec((tm, d), lambda i, j: (i, 0)),
            pl.BlockSpec((1, d), lambda i, j: (0, 0)),
            pl.BlockSpec((d, tn), lambda i, j: (0, j)),
            pl.BlockSpec((d, KR_WIDTH), lambda i, j: (0, 0)),
            pl.BlockSpec((1, tn), lambda i, j: (0, jnp.clip(j - norm_blk0, 0, norm_nblk - 1))),
        ],
        out_specs=[
            pl.BlockSpec((tm, tn), lambda i, j: (i, j)),
            pl.BlockSpec((tm, KR_WIDTH), lambda i, j: (i, 0)),
        ],
        out_shape=[jax.ShapeDtypeStruct((r, n), BF16), jax.ShapeDtypeStruct((r, KR_WIDTH), F32)],
        scratch_shapes=[pltpu.VMEM((tm, d), BF16)],
        compiler_params=_params(("parallel", "arbitrary"), est),
        name="in_proj",
    )(x2d, g1, w_main, w_kr, norm_g)


def _rope_half(t, cos, sin):
    return t * cos + pltpu.roll(t, 64, axis=1) * sin


def _mla_up_kernel(cq_ref, ckv_ref, kr_ref, cos_ref, sin_ref, wq_ref, wkv_ref, q_ref, k_ref, v_ref):
    cos = cos_ref[...]
    sin = sin_ref[...]
    scale = (MLA_NOPE + MLA_ROPE) ** -0.5
    yq = jnp.dot(cq_ref[...], wq_ref[...], preferred_element_type=F32)
    ykv = jnp.dot(ckv_ref[...], wkv_ref[...], preferred_element_type=F32)
    k_rope = _rope_half(kr_ref[...], cos, sin).astype(BF16)
    for h in range(MLA_HEADS):
        c0 = h * MLA_QK_PAD
        q_ref[h, :, 0:MLA_NOPE] = (yq[:, c0:c0 + MLA_NOPE] * scale).astype(BF16)
        q_ref[h, :, MLA_NOPE:] = (_rope_half(yq[:, c0 + MLA_NOPE:c0 + MLA_QK_PAD], cos, sin) * scale).astype(BF16)
        k_ref[h, :, 0:MLA_NOPE] = ykv[:, c0:c0 + MLA_NOPE].astype(BF16)
        k_ref[h, :, MLA_NOPE:] = k_rope
        v_ref[h] = ykv[:, c0 + MLA_NOPE:c0 + MLA_NOPE + MLA_V].astype(BF16)


def _mla_up(proj, kr, cos_t, sin_t, wq_all, wkv_all, *, tm):
    r = proj.shape[0]
    cq_blk = (3 * NA_WIDTH) // MLA_RANK
    n_pos_blk = cos_t.shape[0] // tm
    width = wq_all.shape[1]
    est = (2 * (2 * tm * MLA_RANK * 2 + 3 * tm * V7X_LANES * 4 + 2 * MLA_RANK * width * 2
                + MLA_HEADS * tm * (2 * MLA_QK_PAD + MLA_V) * 2) + 3 * tm * width * 4)
    return pl.pallas_call(
        _mla_up_kernel,
        grid=(r // tm,),
        in_specs=[
            pl.BlockSpec((tm, MLA_RANK), lambda i: (i, cq_blk)),
            pl.BlockSpec((tm, MLA_RANK), lambda i: (i, cq_blk + 1)),
            pl.BlockSpec((tm, KR_WIDTH), lambda i: (i, 0)),
            pl.BlockSpec((tm, V7X_LANES), lambda i: (i % n_pos_blk, 0)),
            pl.BlockSpec((tm, V7X_LANES), lambda i: (i % n_pos_blk, 0)),
            pl.BlockSpec((MLA_RANK, width), lambda i: (0, 0)),
            pl.BlockSpec((MLA_RANK, width), lambda i: (0, 0)),
        ],
        out_specs=[
            pl.BlockSpec((MLA_HEADS, tm, MLA_QK_PAD), lambda i: (0, i, 0)),
            pl.BlockSpec((MLA_HEADS, tm, MLA_QK_PAD), lambda i: (0, i, 0)),
            pl.BlockSpec((MLA_HEADS, tm, MLA_V), lambda i: (0, i, 0)),
        ],
        out_shape=[
            jax.ShapeDtypeStruct((MLA_HEADS, r, MLA_QK_PAD), BF16),
            jax.ShapeDtypeStruct((MLA_HEADS, r, MLA_QK_PAD), BF16),
            jax.ShapeDtypeStruct((MLA_HEADS, r, MLA_V), BF16),
        ],
        compiler_params=_params(("parallel",), est),
        name="mla_up",
    )(proj, proj, kr, cos_t, sin_t, wq_all, wkv_all)


def _mla_attn_kernel(q_ref, k_ref, v_ref, qm_ref, km_ref, vm_ref, o_ref, om_ref, *, tq):
    k = k_ref[0]
    v = v_ref[0]
    km = _pad_meta_rows(km_ref[0])
    vm = _pad_meta_rows(vm_ref[0])
    meta_lane = lax.broadcasted_iota(jnp.int32, (1, V7X_LANES), 1) < N_META

    def attend(q):
        s = _dot_nt(q, k)
        sm = jnp.where(meta_lane, _dot_nt(q, km), NEG_INF)
        m = jnp.maximum(jnp.max(s, axis=-1, keepdims=True), jnp.max(sm, axis=-1, keepdims=True))
        p = jnp.exp(s - m)
        pm = jnp.exp(sm - m)
        l = jnp.sum(p, axis=-1, keepdims=True) + jnp.sum(pm, axis=-1, keepdims=True)
        o = (jnp.dot(p.astype(BF16), v, preferred_element_type=F32)
             + jnp.dot(pm.astype(BF16), vm, preferred_element_type=F32))
        return o / l

    s_len = q_ref.shape[1]
    for c in range(s_len // tq):
        rows = slice(c * tq, (c + 1) * tq)
        o_ref[rows, :] = attend(q_ref[0, rows, :]).astype(BF16)
    om_ref[...] = attend(qm_ref[0]).astype(BF16)


def _mla_attn(q, k, v, qm, km, vm, *, batch):
    r = q.shape[1]
    s_len = r // batch
    tq = min(MLA_Q_CHUNK, s_len)
    est = 2 * (2 * s_len * 256 * 2 + 2 * s_len * 128 * 2) + 3 * tq * s_len * 4 + tq * s_len * 2
    return pl.pallas_call(
        functools.partial(_mla_attn_kernel, tq=tq),
        grid=(batch, MLA_HEADS),
        in_specs=[
            pl.BlockSpec((1, s_len, MLA_QK_PAD), lambda b, h: (h, b, 0)),
            pl.BlockSpec((1, s_len, MLA_QK_PAD), lambda b, h: (h, b, 0)),
            pl.BlockSpec((1, s_len, MLA_V), lambda b, h: (h, b, 0)),
            pl.BlockSpec((1, N_META, MLA_QK_PAD), lambda b, h: (h, b, 0)),
            pl.BlockSpec((1, N_META, MLA_QK_PAD), lambda b, h: (h, b, 0)),
            pl.BlockSpec((1, N_META, MLA_V), lambda b, h: (h, b, 0)),
        ],
        out_specs=[
            pl.BlockSpec((s_len, MLA_V), lambda b, h: (b, h)),
            pl.BlockSpec((N_META, MLA_V), lambda b, h: (b, h)),
        ],
        out_shape=[
            jax.ShapeDtypeStruct((r, MLA_HEADS * MLA_V), BF16),
            jax.ShapeDtypeStruct((batch * N_META, MLA_HEADS * MLA_V), BF16),
        ],
        compiler_params=_params(("parallel", "parallel"), est),
        name="mla_attn",
    )(q, k, v, qm, km, vm)


def _na_bias_tiles(rel_bias):
    qc = np.arange(GRID_W)
    kc = np.arange(GRID_W)
    cstart = np.clip(qc - NA_WIN_W // 2, 0, GRID_W - NA_WIN_W)
    col_ok = (kc[None, :] >= cstart[:, None]) & (kc[None, :] < cstart[:, None] + NA_WIN_W)
    col_off = np.clip(kc[None, :] - qc[:, None], -(NA_WIN_W - 1), NA_WIN_W - 1) + NA_WIN_W - 1
    n_co = 2 * NA_WIN_W - 1
    onehot = np.zeros((n_co, GRID_W * GRID_W), np.float32)
    onehot[col_off.reshape(-1), np.arange(GRID_W * GRID_W)] = 1.0
    h, n_ro = rel_bias.shape[0], rel_bias.shape[1]
    tiles = jnp.dot(rel_bias.astype(F32).reshape(h * n_ro, n_co), jnp.asarray(onehot),
                    precision=lax.Precision.HIGHEST).reshape(h, n_ro, GRID_W, GRID_W)
    return jnp.where(jnp.asarray(col_ok)[None, None], tiles, NEG_INF)


def _na_block_plan(rows):
    nqb = rows // NA_QROWS
    wh = NA_WIN_H
    plan = []
    for qb in (0, 1, nqb - 1):
        kstart = int(np.clip(NA_QROWS * qb - wh // 2, 0, rows - NA_KROWS))
        variant = []
        for i in range(NA_QROWS):
            r = NA_QROWS * qb + i
            rs = int(np.clip(r - wh // 2, 0, rows - wh))
            variant.append([kstart + j - r + wh - 1 if rs <= kstart + j < rs + wh else None
                            for j in range(NA_KROWS)])
        plan.append(variant)
    return plan


def _na_attn_kernel(q_ref, k_ref, v_ref, qm_ref, km_ref, vm_ref, tiles_ref, mb_ref, o_ref, om_ref, bias_ref,
                    *, rows):
    nqb = rows // NA_QROWS
    qblk = NA_QROWS * GRID_W
    kblk = NA_KROWS * GRID_W

    @pl.when(pl.program_id(1) == 0)
    def _():
        masked = jnp.full((GRID_W, GRID_W), NEG_INF, F32)
        for hh in range(2):
            for v, variant in enumerate(_na_block_plan(rows)):
                for i, row_plan in enumerate(variant):
                    for j0 in range(0, NA_KROWS, 2):
                        pair = [masked if ro is None else tiles_ref[0, hh, ro] for ro in row_plan[j0:j0 + 2]]
                        bias_ref[hh, v, i * GRID_W:(i + 1) * GRID_W, j0 * GRID_W:(j0 + 2) * GRID_W] = (
                            jnp.concatenate(pair, axis=1))

    km = _pad_meta_rows(km_ref[...])
    vm = _pad_meta_rows(vm_ref[...])
    lane = lax.broadcasted_iota(jnp.int32, (1, 2 * NA_HEAD_DIM), 1)
    head_lanes = [lane < NA_HEAD_DIM, lane >= NA_HEAD_DIM]

    def softmax_pv(s, sm, v_loc):
        m = jnp.maximum(jnp.max(s, axis=-1, keepdims=True), jnp.max(sm, axis=-1, keepdims=True))
        p = jnp.exp(s - m)
        pm = jnp.exp(sm - m)
        l = jnp.sum(p, axis=-1, keepdims=True) + jnp.sum(pm, axis=-1, keepdims=True)
        o = (jnp.dot(p.astype(BF16), v_loc, preferred_element_type=F32)
             + jnp.dot(pm.astype(BF16), vm, preferred_element_type=F32))
        return o / l

    def block(qb, carry):
        q0 = pl.multiple_of(qb * qblk, qblk)
        k0 = pl.multiple_of(jnp.clip(NA_QROWS * qb - NA_WIN_H // 2, 0, rows - NA_KROWS) * GRID_W, GRID_W)
        variant = jnp.where(qb == 0, 0, jnp.where(qb == nqb - 1, 2, 1))
        q = q_ref[pl.ds(q0, qblk), :]
        k_loc = k_ref[pl.ds(k0, kblk), :]
        v_loc = v_ref[pl.ds(k0, kblk), :]
        outs = []
        for hh in range(2):
            qh = jnp.where(head_lanes[hh], q, jnp.zeros_like(q))
            s = _dot_nt(qh, k_loc) + bias_ref[hh, variant]
            sm = _dot_nt(qh, km) + mb_ref[0, hh]
            outs.append(softmax_pv(s, sm, v_loc))
        o_ref[pl.ds(q0, qblk), :] = jnp.where(head_lanes[0], outs[0], outs[1]).astype(BF16)
        return carry

    lax.fori_loop(0, nqb, block, 0)

    qm = qm_ref[...]
    outs = []
    for hh in range(2):
        qh = jnp.where(head_lanes[hh], qm, jnp.zeros_like(qm))
        sm = _dot_nt(qh, km) + mb_ref[0, hh]
        m = jnp.max(sm, axis=-1, keepdims=True)
        pm = jnp.exp(sm - m)
        l = jnp.sum(pm, axis=-1, keepdims=True)
        outs.append(jnp.dot(pm.astype(BF16), vm, preferred_element_type=F32) / l)
    om_ref[...] = jnp.where(head_lanes[0], outs[0], outs[1]).astype(BF16)


def _na_attn(proj, proj_m, bias_tiles, meta_bias, *, batch):
    r = proj.shape[0]
    s_len = r // batch
    rows = s_len // GRID_W
    assert rows % NA_QROWS == 0 and rows >= NA_KROWS, rows
    npair = NA_HEADS // 2
    qblk = NA_QROWS * GRID_W
    kblk = NA_KROWS * GRID_W
    n_ro = bias_tiles.shape[1]
    tiles5 = bias_tiles.reshape(npair, 2, n_ro, GRID_W, GRID_W)
    mb4 = jnp.concatenate([meta_bias.astype(F32), jnp.full((NA_HEADS, V7X_LANES - N_META), NEG_INF, F32)],
                          axis=1).reshape(npair, 2, 1, V7X_LANES)
    est = 2 * (4 * s_len * NA_PAIR_WIDTH * 2) + 12 * qblk * kblk * 4
    return pl.pallas_call(
        functools.partial(_na_attn_kernel, rows=rows),
        grid=(npair, batch),
        in_specs=[
            pl.BlockSpec((s_len, NA_PAIR_WIDTH), lambda p, b: (b, p)),
            pl.BlockSpec((s_len, NA_PAIR_WIDTH), lambda p, b: (b, npair + p)),
            pl.BlockSpec((s_len, NA_PAIR_WIDTH), lambda p, b: (b, 2 * npair + p)),
            pl.BlockSpec((N_META, NA_PAIR_WIDTH), lambda p, b: (b, p)),
            pl.BlockSpec((N_META, NA_PAIR_WIDTH), lambda p, b: (b, npair + p)),
            pl.BlockSpec((N_META, NA_PAIR_WIDTH), lambda p, b: (b, 2 * npair + p)),
            pl.BlockSpec((1, 2, n_ro, GRID_W, GRID_W), lambda p, b: (p, 0, 0, 0, 0)),
            pl.BlockSpec((1, 2, 1, V7X_LANES), lambda p, b: (p, 0, 0, 0)),
        ],
        out_specs=[
            pl.BlockSpec((s_len, NA_PAIR_WIDTH), lambda p, b: (b, p)),
            pl.BlockSpec((N_META, NA_PAIR_WIDTH), lambda p, b: (b, p)),
        ],
        out_shape=[
            jax.ShapeDtypeStruct((r, NA_WIDTH), BF16),
            jax.ShapeDtypeStruct((batch * N_META, NA_WIDTH), BF16),
        ],
        scratch_shapes=[pltpu.VMEM((2, 3, qblk, kblk), F32)],
        compiler_params=_params(("parallel", "arbitrary"), est),
        name="na_attn",
    )(proj, proj, proj, proj_m, proj_m, proj_m, tiles5, mb4)


def _merge_kernel(ona_ref, omla_ref, wna_ref, wmla_ref, ga_ref, gb_ref, o_ref):
    a = jnp.dot(ona_ref[...], wna_ref[...], preferred_element_type=F32)
    b = jnp.dot(omla_ref[...], wmla_ref[...], preferred_element_type=F32)
    o_ref[...] = (ga_ref[...].astype(F32) * a + gb_ref[...].astype(F32) * b).astype(BF16)


def _merge(o_na, o_mla, w_na, w_mla, proj, *, tm):
    r = o_na.shape[0]
    d = w_na.shape[1]
    tn = _col_tile(d)
    ga_blk = (3 * NA_WIDTH + 2 * MLA_RANK) // tn
    gb_blk = ga_blk + d // tn
    est = 2 * (tm * (NA_WIDTH + MLA_HEADS * MLA_V) * 2 + (NA_WIDTH + MLA_HEADS * MLA_V) * tn * 2
               + 3 * tm * tn * 2) + 3 * tm * tn * 4
    return pl.pallas_call(
        _merge_kernel,
        grid=(r // tm, d // tn),
        in_specs=[
            pl.BlockSpec((tm, NA_WIDTH), lambda i, j: (i, 0)),
            pl.BlockSpec((tm, MLA_HEADS * MLA_V), lambda i, j: (i, 0)),
            pl.BlockSpec((NA_WIDTH, tn), lambda i, j: (0, j)),
            pl.BlockSpec((MLA_HEADS * MLA_V, tn), lambda i, j: (0, j)),
            pl.BlockSpec((tm, tn), lambda i, j: (i, ga_blk + j)),
            pl.BlockSpec((tm, tn), lambda i, j: (i, gb_blk + j)),
        ],
        out_specs=pl.BlockSpec((tm, tn), lambda i, j: (i, j)),
        out_shape=jax.ShapeDtypeStruct((r, d), BF16),
        compiler_params=_params(("parallel", "arbitrary"), est),
        name="merge",
    )(o_na, o_mla, w_na, w_mla, proj, proj)


def _out_proj_kernel(m_ref, w_ref, x_ref, o_ref):
    o_ref[...] = x_ref[...] + jnp.dot(m_ref[...], w_ref[...], preferred_element_type=F32)


def _out_proj(merged, w_out, x2d, *, tm):
    r, d = x2d.shape
    tn = _col_tile(d)
    est = 2 * (tm * d * 2 + d * tn * 2 + 2 * tm * tn * 4) + tm * tn * 4
    return pl.pallas_call(
        _out_proj_kernel,
        grid=(r // tm, d // tn),
        in_specs=[
            pl.BlockSpec((tm, d), lambda i, j: (i, 0)),
            pl.BlockSpec((d, tn), lambda i, j: (0, j)),
            pl.BlockSpec((tm, tn), lambda i, j: (i, j)),
        ],
        out_specs=pl.BlockSpec((tm, tn), lambda i, j: (i, j)),
        out_shape=jax.ShapeDtypeStruct((r, d), F32),
        compiler_params=_params(("parallel", "arbitrary"), est),
        name="out_proj",
    )(merged, w_out, x2d)


def _peer_q_kernel(h_ref, g_ref, w_ref, q_ref, hnt_ref, hn_ref):
    @pl.when(pl.program_id(1) == 0)
    def _():
        hn = _rmsnorm(h_ref[...], g_ref[...])
        hn_ref[...] = hn.astype(BF16)
        hnt_ref[...] = hn.T.astype(BF16)

    q_ref[...] = jnp.dot(hn_ref[...], w_ref[...], preferred_element_type=F32).astype(BF16)


def _peer_q(h2, g2, w_q, *, tm):
    r, d = h2.shape
    n = w_q.shape[1]
    tn = COL_TILE
    est = 2 * (tm * d * 4 + d * tn * 2 + tm * tn * 2 + tm * d * 2) + tm * d * 2 + tm * tn * 4 + 2 * tm * d * 4
    return pl.pallas_call(
        _peer_q_kernel,
        grid=(r // tm, n // tn),
        in_specs=[
            pl.BlockSpec((tm, d), lambda i, j: (i, 0)),
            pl.BlockSpec((1, d), lambda i, j: (0, 0)),
            pl.BlockSpec((d, tn), lambda i, j: (0, j)),
        ],
        out_specs=[
            pl.BlockSpec((tm, tn), lambda i, j: (i, j)),
            pl.BlockSpec((d, tm), lambda i, j: (0, i)),
        ],
        out_shape=[jax.ShapeDtypeStruct((r, n), BF16), jax.ShapeDtypeStruct((d, r), BF16)],
        scratch_shapes=[pltpu.VMEM((tm, d), BF16)],
        compiler_params=_params(("parallel", "arbitrary"), est),
        name="peer_q",
    )(h2, g2, w_q)


def _compare_exchange(a, b):
    if a is None:
        return b, None
    if b is None:
        return a, None
    return jnp.maximum(a, b), jnp.minimum(a, b)


def _bitonic_merge(xs):
    n = len(xs)
    step = n // 2
    while step >= 1:
        for i in range(n):
            if i & step == 0:
                xs[i], xs[i + step] = _compare_exchange(xs[i], xs[i + step])
        step //= 2
    return xs


def _bitonic_sort(xs):
    n = len(xs)
    xs = list(xs)
    size = 2
    while size <= n:
        step = size // 2
        while step >= 1:
            for i in range(n):
                if i & step == 0:
                    hi, lo = _compare_exchange(xs[i], xs[i + step])
                    xs[i], xs[i + step] = (hi, lo) if i & size == 0 else (lo, hi)
            step //= 2
        size *= 2
    return xs


def _merge_topk(xs, ys, k):
    xs = list(xs) + [None] * (k - len(xs))
    ys = list(ys) + [None] * (k - len(ys))
    return _bitonic_merge([_compare_exchange(xs[i], ys[k - 1 - i])[0] for i in range(k)])


def _pow2_ceil(n):
    return 1 << (n - 1).bit_length()


def _sorted_topk_slabs(s, k):
    slabs = _bitonic_sort([s[8 * v:8 * v + 8, :] for v in range(s.shape[0] // 8)])
    for shift in (4, 2, 1):
        slabs = _merge_topk(slabs, [pltpu.roll(x, shift, axis=0) for x in slabs], k)
    return slabs


def _extract_topk(x, k):
    n, t = x.shape
    row = lax.broadcasted_iota(jnp.int32, (n, t), 0).astype(F32)
    krow = lax.broadcasted_iota(jnp.int32, (k, t), 0)

    def body(i, carry):
        x, rank, vals = carry
        m = jnp.max(x, axis=0, keepdims=True)
        first = jnp.min(jnp.where(x == m, row, float(n)), axis=0, keepdims=True)
        hit = row == first
        rank = jnp.where(hit, jnp.asarray(i, F32), rank)
        x = jnp.where(hit, _F32_MIN, x)
        vals = jnp.where(krow == i, m, vals)
        return x, rank, vals

    _, rank, vals = lax.fori_loop(
        0, k, body, (x, jnp.full((n, t), float(k), F32), jnp.zeros((k, t), F32)))
    return vals, rank


def _route_head_exact(s1, s2):
    k = PEER_TOPK
    top1, rank1 = _extract_topk(s1, k)
    top2, rank2 = _extract_topk(s2, k)

    tm = s1.shape[1]
    pieces, spans = [], []
    off = 0
    for i in range(k):
        cnt = k // (i + 1)
        rows = -(-cnt // 8) * 8
        jrow = lax.broadcasted_iota(jnp.int32, (rows, tm), 0)
        pieces.append(jnp.where(jrow < cnt, top1[i:i + 1, :] + top2[0:rows, :], _F32_MIN))
        spans.append((off, rows))
        off += rows
    cand = jnp.concatenate(pieces, axis=0)
    ctop, crank = _extract_topk(cand, k)
    chosen = crank < float(k)
    cmax = ctop[0:1, :]
    z = jnp.sum(jnp.where(chosen, jnp.exp(cand - cmax), 0.0), axis=0, keepdims=True)

    y1 = jnp.ones(s1.shape, F32)
    for i, (off, rows) in enumerate(spans):
        n_sel = jnp.sum(jnp.where(chosen[off:off + rows, :], 1.0, 0.0), axis=0, keepdims=True)
        y1 = jnp.where(rank1 == float(i), 1.0 - n_sel, y1)
    return y1, jnp.exp(s1 - top1[0:1, :]) / z, -rank2, jnp.exp(s2 - top2[0:1, :])


def _route_tables_by_value(scores):
    k = PEER_TOPK
    tm = scores[0][0].shape[1]

    sublane = lax.broadcasted_iota(jnp.int32, (8, tm), 0)
    tops = []
    for side in range(2):
        packed = None
        for h in range(PEER_HEADS):
            slabs = _sorted_topk_slabs(scores[h][side], k)
            packed = slabs if packed is None else [jnp.where(sublane == h, s, p) for s, p in zip(slabs, packed)]
        tops.append(packed)
    top1, top2 = tops

    cells = {(i, j): top1[i] + top2[j] for i in range(k) for j in range(k // (i + 1))}
    grid_rows = [[cells[i, j] for j in range(k // (i + 1))] for i in range(k)]
    lists = [r for r in grid_rows[1:] if len(r) > 1] + [[r[0] for r in grid_rows if len(r) == 1]]
    while len(lists) > 1:
        lists.sort(key=len)
        size = min(k, _pow2_ceil(len(lists[0]) + len(lists[1])))
        merged = [m for m in _merge_topk(lists[0], lists[1], size) if m is not None]
        lists = [merged] + lists[2:]
    ctop = _merge_topk(grid_rows[0], lists[0], k)
    thr, cmax = ctop[k - 1], ctop[0]

    taken = {c: v >= thr for c, v in cells.items()}
    n_taken = sum(jnp.where(t, 1.0, 0.0) for t in taken.values())
    z = sum(jnp.where(taken[c], jnp.exp(v - cmax), 0.0) for c, v in cells.items())
    inv_z = 1.0 / z
    y1_rows = [1.0 - sum(jnp.where(taken[i, j], 1.0, 0.0) for j in range(k // (i + 1))) for i in range(k)]

    exact = jnp.where(n_taken == float(k), 1.0, 0.0)
    for top in (top1, top2):
        for i in range(k - 1):
            exact = jnp.where(top[i] > top[i + 1], exact, 0.0)

    tables = []
    for h in range(PEER_HEADS):
        s1, s2 = scores[h]
        row = slice(h, h + 1)
        y1 = jnp.ones(s1.shape, F32)
        for i in range(k):
            y1 = jnp.where(s1 == top1[i][row], y1_rows[i][row], y1)
        y2 = jnp.full(s2.shape, -float(k), F32)
        for j in reversed(range(k)):
            y2 = jnp.where(s2 >= top2[j][row], -float(j), y2)
        tables.append((y1, jnp.exp(s1 - top1[0][row]) * inv_z[row], y2, jnp.exp(s2 - top2[0][row])))
        for s, top in ((s1, top1), (s2, top2)):
            members = jnp.sum(jnp.where(s >= top[k - 1][row], 1.0, 0.0), axis=0, keepdims=True)
            exact = jnp.where(jnp.logical_and(sublane == h, members != float(k)), 0.0, exact)
    return tables, exact


def _peer_route_kernel(q_ref, keys_ref, y1_ref, e1_ref, y2_ref, e2_ref):
    half = PEER_DK // 2
    scores = []
    for h in range(PEER_HEADS):
        q = q_ref[:, h * PEER_DK:(h + 1) * PEER_DK]
        scores.append((_dot_nt(keys_ref[h, 0], q[:, :half]),
                       _dot_nt(keys_ref[h, 1], q[:, half:])))

    def store(h, tables):
        y1, e1, y2, e2 = tables
        y1_ref[h] = y1
        e1_ref[h] = e1
        y2_ref[h] = pltpu.bitcast(y2.astype(BF16), jnp.uint32)
        e2_ref[h] = pltpu.bitcast(e2.astype(BF16), jnp.uint32)

    tables, exact = _route_tables_by_value(scores)
    for h in range(PEER_HEADS):
        store(h, tables[h])

    for h in range(PEER_HEADS):
        @pl.when(jnp.min(exact[h:h + 1, :]) < 0.5)
        def _(h=h):
            store(h, _route_head_exact(*scores[h]))


def _peer_route(q2, sub_keys, *, tm):
    r = q2.shape[0]
    width = PEER_HEADS * PEER_DK
    tab_spec = pl.BlockSpec((PEER_HEADS, PEER_NKEYS, tm), lambda i: (0, 0, i))
    tab = jax.ShapeDtypeStruct((PEER_HEADS, PEER_NKEYS, r), F32)
    packed_spec = pl.BlockSpec((PEER_HEADS, PEER_NKEYS // 2, tm), lambda i: (0, 0, i))
    packed = jax.ShapeDtypeStruct((PEER_HEADS, PEER_NKEYS // 2, r), jnp.uint32)
    est = (2 * (tm * width * 2 + PEER_HEADS * PEER_NKEYS * PEER_DK * 2 + PEER_HEADS * PEER_NKEYS * tm * 16)
           + 40 * PEER_HEADS * PEER_NKEYS * tm * 4)
    return pl.pallas_call(
        _peer_route_kernel,
        grid=(r // tm,),
        in_specs=[
            pl.BlockSpec((tm, width), lambda i: (i, 0)),
            pl.BlockSpec((PEER_HEADS, 2, PEER_NKEYS, PEER_DK // 2), lambda i: (0, 0, 0, 0)),
        ],
        out_specs=[tab_spec, tab_spec, packed_spec, packed_spec],
        out_shape=[tab, tab, packed, packed],
        compiler_params=_params(("parallel",), est),
        name="peer_route",
    )(q2, sub_keys)


def _peer_dense_kernel(hnt_ref, u_ref, vt_ref, y1_ref, e1_ref, y2_ref, e2_ref, h_ref, g_ref, o_ref,
                       acc_ref, act_ref, p_ref, *, te, final_norm):
    j = pl.program_id(1)
    d, tm = acc_ref.shape

    @pl.when(j == 0)
    def _():
        acc_ref[...] = jnp.zeros_like(acc_ref)

    act = jax.nn.gelu(jnp.dot(u_ref[...], hnt_ref[...], preferred_element_type=F32).astype(BF16))
    act_ref[...] = pltpu.bitcast(act, jnp.uint32)
    a_blocks = te // PEER_NKEYS
    a0 = j * a_blocks
    zero = jnp.zeros((), BF16)
    for ai in range(a_blocks):
        y1_rows = [jnp.broadcast_to(y1_ref[h, pl.ds(a0 + ai, 1), :], (BF16_ROWS, tm)).astype(BF16)
                   for h in range(PEER_HEADS)]
        e1_rows = [jnp.broadcast_to(e1_ref[h, pl.ds(a0 + ai, 1), :], (BF16_ROWS, tm)).astype(BF16)
                   for h in range(PEER_HEADS)]
        for c in range(tm // V7X_LANES):
            cols = slice(c * V7X_LANES, (c + 1) * V7X_LANES)
            for g in range(PEER_NKEYS // BF16_ROWS):
                b_words = slice(g * BF16_ROWS // 2, (g + 1) * BF16_ROWS // 2)
                gate = None
                for h in range(PEER_HEADS):
                    sel = pltpu.bitcast(y2_ref[h, b_words, cols], BF16) >= y1_rows[h][:, cols]
                    term = jnp.where(sel, e1_rows[h][:, cols] * pltpu.bitcast(e2_ref[h, b_words, cols], BF16), zero)
                    gate = term if gate is None else gate + term
                e_words = slice((ai * PEER_NKEYS + g * BF16_ROWS) // 2, (ai * PEER_NKEYS + (g + 1) * BF16_ROWS) // 2)
                p_ref[e_words, cols] = pltpu.bitcast(gate * pltpu.bitcast(act_ref[e_words, cols], BF16), jnp.uint32)
    acc_ref[...] += jnp.dot(vt_ref[...], pltpu.bitcast(p_ref[...], BF16), preferred_element_type=F32)

    @pl.when(j == pl.num_programs(1) - 1)
    def _():
        out = h_ref[...] + acc_ref[...].T
        o_ref[...] = _rmsnorm(out, g_ref[...]) if final_norm else out


def _peer_dense(hn2t, u, vt, tabs, h2, gf, *, tm, te, final_norm):
    d, r = hn2t.shape
    e = u.shape[0]
    y1, e1, y2, e2 = tabs
    tab_spec = pl.BlockSpec((PEER_HEADS, PEER_NKEYS, tm), lambda i, j: (0, 0, i))
    packed_spec = pl.BlockSpec((PEER_HEADS, PEER_NKEYS // 2, tm), lambda i, j: (0, 0, i))
    est = (2 * (tm * d * 2 + 2 * te * d * 2 + PEER_HEADS * PEER_NKEYS * tm * 12 + tm * d * 4)
           + tm * d * 4 + d * tm * 4 + 6 * te * tm * 4)
    return pl.pallas_call(
        functools.partial(_peer_dense_kernel, te=te, final_norm=final_norm),
        grid=(r // tm, e // te),
        in_specs=[
            pl.BlockSpec((d, tm), lambda i, j: (0, i)),
            pl.BlockSpec((te, d), lambda i, j: (j, 0)),
            pl.BlockSpec((d, te), lambda i, j: (0, j)),
            tab_spec, tab_spec, packed_spec, packed_spec,
            pl.BlockSpec((tm, d), lambda i, j: (i, 0), pipeline_mode=pl.Buffered(1)),
            pl.BlockSpec((1, d), lambda i, j: (0, 0)),
        ],
        out_specs=pl.BlockSpec((tm, d), lambda i, j: (i, 0)),
        out_shape=jax.ShapeDtypeStruct((r, d), F32),
        scratch_shapes=[pltpu.VMEM((d, tm), F32), pltpu.VMEM((te // 2, tm), jnp.uint32),
                        pltpu.VMEM((te // 2, tm), jnp.uint32)],
        compiler_params=_params(("parallel", "arbitrary"), est),
        name="peer_dense",
    )(hn2t, u, vt, y1, e1, y2, e2, h2, gf)


def _rope_tables(pos):
    inv_freq = ROPE_THETA ** (-jnp.arange(0, MLA_ROPE, 2, dtype=F32) / MLA_ROPE)
    ang = pos.astype(F32)[:, None] * inv_freq[None, :]
    cos, sin = jnp.cos(ang), jnp.sin(ang)
    zeros = jnp.zeros((pos.shape[0], V7X_LANES - MLA_ROPE), F32)
    return (jnp.concatenate([cos, cos, zeros], axis=1), jnp.concatenate([-sin, sin, zeros], axis=1))


def _swap_halves(w):
    half = w.shape[-1] // 2
    return jnp.concatenate([w[..., half:], w[..., :half]], axis=-1)


def _layer(x_real, x_meta, p, l, *, batch):
    d = x_real.shape[1]
    s_len = x_real.shape[0] // batch
    tm_real = _row_tile(x_real.shape[0], ROW_TILE)
    tm_meta = x_meta.shape[0]

    w_in = p["w_in"][l]
    kr0 = 3 * NA_WIDTH + 2 * MLA_RANK
    w_main = jnp.concatenate([w_in[:, :kr0], w_in[:, kr0 + MLA_ROPE:]], axis=1).astype(BF16)
    w_kr = w_in[:, kr0:kr0 + MLA_ROPE]
    w_kr = jnp.concatenate([w_kr, _swap_halves(w_kr)], axis=1).astype(BF16)
    norm_g = jnp.concatenate([p["mla_q_norm_g"][l], p["mla_kv_norm_g"][l]])[None].astype(F32)
    g1 = p["norm1_g"][l][None].astype(F32)
    wq = p["mla_w_uq"][l].reshape(MLA_RANK, MLA_HEADS, MLA_NOPE + MLA_ROPE)
    wq = jnp.concatenate([wq, _swap_halves(wq[..., MLA_NOPE:])], axis=-1)
    wq = wq.reshape(MLA_RANK, MLA_HEADS * MLA_QK_PAD).astype(BF16)
    wkv = p["mla_w_ukv"][l].astype(BF16)
    cos_r, sin_r = _rope_tables(N_META + jnp.arange(s_len))
    cos_m, sin_m = _rope_tables(jnp.tile(jnp.arange(N_META), batch))
    bias_tiles = _na_bias_tiles(p["na_rel_bias"][l])

    proj_r, kr_r = _in_proj(x_real, g1, w_main, w_kr, norm_g, tm=tm_real)
    proj_m, kr_m = _in_proj(x_meta, g1, w_main, w_kr, norm_g, tm=tm_meta)
    q_r, k_r, v_r = _mla_up(proj_r, kr_r, cos_r, sin_r, wq, wkv, tm=_row_tile(s_len, MLA_UP_TILE))
    q_m, k_m, v_m = _mla_up(proj_m, kr_m, cos_m, sin_m, wq, wkv, tm=tm_meta)
    omla_r, omla_m = _mla_attn(q_r, k_r, v_r, q_m, k_m, v_m, batch=batch)
    ona_r, ona_m = _na_attn(proj_r, proj_m, bias_tiles, p["na_meta_bias"][l], batch=batch)

    w_na = p["w_na_branch"][l].astype(BF16)
    w_mla = p["w_mla_branch"][l].astype(BF16)
    w_out = p["w_out"][l].astype(BF16)
    h_real = _out_proj(_merge(ona_r, omla_r, w_na, w_mla, proj_r, tm=tm_real), w_out, x_real, tm=tm_real)
    h_meta = _out_proj(_merge(ona_m, omla_m, w_na, w_mla, proj_m, tm=tm_meta), w_out, x_meta, tm=tm_meta)
    return h_real, h_meta


def _peer(h2, p, l, gf, *, tm, te, final_norm):
    g2 = p["norm2_g"][l][None].astype(F32)
    w_q = p["peer_w_q"][l].astype(BF16)
    keys = p["peer_sub_keys"][l].astype(BF16)
    u = p["peer_u"][l].astype(BF16)
    vt = p["peer_v"][l].astype(BF16).T
    q2, hn2t = _peer_q(h2, g2, w_q, tm=_row_tile(h2.shape[0], ROW_TILE))
    tabs = _peer_route(q2, keys, tm=_row_tile(h2.shape[0], ROUTE_TILE))
    return _peer_dense(hn2t, u, vt, tabs, h2, gf, tm=tm, te=te, final_norm=final_norm)


def kernel(x, meta_tokens, norm1_g, w_in, na_rel_bias, na_meta_bias, mla_q_norm_g, mla_w_uq, mla_kv_norm_g,
           mla_w_ukv, w_na_branch, w_mla_branch, w_out, norm2_g, peer_w_q, peer_sub_keys, peer_u, peer_v,
           final_norm_g):
    b, s, d = x.shape
    depth = w_in.shape[0]
    p = dict(norm1_g=norm1_g, w_in=w_in, na_rel_bias=na_rel_bias, na_meta_bias=na_meta_bias,
             mla_q_norm_g=mla_q_norm_g, mla_w_uq=mla_w_uq, mla_kv_norm_g=mla_kv_norm_g, mla_w_ukv=mla_w_ukv,
             w_na_branch=w_na_branch, w_mla_branch=w_mla_branch, w_out=w_out, norm2_g=norm2_g,
             peer_w_q=peer_w_q, peer_sub_keys=peer_sub_keys, peer_u=peer_u, peer_v=peer_v)
    x_real = x.reshape(b * s, d)
    x_meta = jnp.broadcast_to(meta_tokens.astype(x.dtype)[None], (b, N_META, d)).reshape(b * N_META, d)
    gf = final_norm_g[None].astype(F32)
    tm = _row_tile(b * s, EXPERT_TOKEN_TILE)
    te = EXPERT_BLOCK
    for l in range(depth):
        h_real, h_meta = _layer(x_real, x_meta, p, l, batch=b)
        x_real = _peer(h_real, p, l, gf, tm=tm, te=te, final_norm=l == depth - 1)
        x_meta = _peer(h_meta, p, l, gf, tm=h_meta.shape[0], te=te, final_norm=False)
    return x_real.reshape(b, s, d)
```

```python
import functools

import jax
import jax.numpy as jnp
import numpy as np
from jax import lax
from jax.experimental import pallas as pl
from jax.experimental.pallas import tpu as pltpu

GRID_W = 64
N_META = 16
NA_HEADS = 16
NA_HEAD_DIM = 64
NA_WIN_H = 8
NA_WIN_W = 16
NA_WIDTH = NA_HEADS * NA_HEAD_DIM
MLA_HEADS = 16
MLA_RANK = 512
MLA_NOPE = 128
MLA_ROPE = 64
MLA_V = 128
MLA_QK_PAD = 256
ROPE_THETA = 10000.0
PEER_HEADS = 8
PEER_NKEYS = 128
PEER_DK = 256
PEER_TOPK = 16
NORM_EPS = 1e-6
NEG_INF = -1e30
LOG2_E = 1.4426950408889634

V7X_VMEM_BYTES = 64 * 1024 * 1024
V7X_VMEM_HEADROOM = 6 * 1024 * 1024
V7X_LANES = 128
BF16_ROWS = 16

NA_QROWS = 4
NA_KROWS = NA_QROWS + NA_WIN_H
NA_PAIR_WIDTH = 2 * NA_HEAD_DIM
KR_WIDTH = 2 * MLA_ROPE

ROW_TILE = 1024
COL_TILE = 1024
MLA_UP_TILE = 256
MLA_Q_CHUNK = 512
ROUTE_TILE = 256
EXPERT_TOKEN_TILE = 512
EXPERT_BLOCK = 1024

F32 = jnp.float32
BF16 = jnp.bfloat16
_F32_MIN = float(np.finfo(np.float32).min)


def _vmem_limit(estimate_bytes):
    want = int(estimate_bytes * 1.25) + 4 * 1024 * 1024
    return min(max(want, 16 * 1024 * 1024), V7X_VMEM_BYTES - V7X_VMEM_HEADROOM)


def _params(semantics, vmem_estimate):
    return pltpu.CompilerParams(dimension_semantics=semantics, vmem_limit_bytes=_vmem_limit(vmem_estimate))


def _row_tile(rows, want):
    tile = min(rows, want)
    assert rows % tile == 0, (rows, tile)
    return tile


def _col_tile(d):
    return COL_TILE if d % COL_TILE == 0 else COL_TILE // 2


def _dot_nt(a, b):
    return lax.dot_general(a, b, (((1,), (1,)), ((), ())), preferred_element_type=F32)


def _rmsnorm(x, g):
    return x * lax.rsqrt(jnp.mean(x * x, axis=-1, keepdims=True) + NORM_EPS) * g


def _pad_meta_rows(a):
    return jnp.concatenate([a, jnp.zeros((V7X_LANES - N_META, a.shape[1]), a.dtype)], axis=0)


def _in_proj_kernel(x_ref, g_ref, w_ref, wkr_ref, ng_ref, o_ref, kr_ref, hn_ref, *, tn):
    j = pl.program_id(1)

    @pl.when(j == 0)
    def _():
        hn = _rmsnorm(x_ref[...], g_ref[...]).astype(BF16)
        hn_ref[...] = hn
        kr_ref[...] = jnp.dot(hn, wkr_ref[...], preferred_element_type=F32)

    acc = jnp.dot(hn_ref[...], w_ref[...], preferred_element_type=F32)
    col0 = j * tn
    plain_end = 3 * NA_WIDTH
    norm_end = plain_end + 2 * MLA_RANK

    @pl.when(col0 < plain_end)
    def _():
        scale = jnp.where(col0 < NA_WIDTH, NA_HEAD_DIM ** -0.5, 1.0).astype(F32)
        o_ref[...] = (acc * scale).astype(BF16)

    @pl.when(jnp.logical_and(col0 >= plain_end, col0 < norm_end))
    def _():
        for c in range(tn // MLA_RANK):
            sl = slice(c * MLA_RANK, (c + 1) * MLA_RANK)
            o_ref[:, sl] = _rmsnorm(acc[:, sl], ng_ref[:, sl]).astype(BF16)

    @pl.when(col0 >= norm_end)
    def _():
        o_ref[...] = (0.5 * jnp.tanh(0.5 * acc) + 0.5).astype(BF16)


def _in_proj(x2d, g1, w_main, w_kr, norm_g, *, tm):
    r, d = x2d.shape
    n = w_main.shape[1]
    tn = _col_tile(d)
    norm_blk0 = (3 * NA_WIDTH) // tn
    norm_nblk = (2 * MLA_RANK) // tn
    est = 2 * (tm * d * 4 + d * tn * 2 + tm * tn * 2 + tm * KR_WIDTH * 4 + d * KR_WIDTH * 2) + tm * d * 2 + tm * tn * 4
    return pl.pallas_call(
        functools.partial(_in_proj_kernel, tn=tn),
        grid=(r // tm, n // tn),
        in_specs=[
            pl.BlockSpec((tm, d), lambda i, j: (i, 0)),
            pl.BlockSpec((1, d), lambda i, j: (0, 0)),
            pl.BlockSpec((d, tn), lambda i, j: (0, j)),
            pl.BlockSpec((d, KR_WIDTH), lambda i, j: (0, 0)),
            pl.BlockSpec((1, tn), lambda i, j: (0, jnp.clip(j - norm_blk0, 0, norm_nblk - 1))),
        ],
        out_specs=[
            pl.BlockSpec((tm, tn), lambda i, j: (i, j)),
            pl.BlockSpec((tm, KR_WIDTH), lambda i, j: (i, 0)),
        ],
        out_shape=[jax.ShapeDtypeStruct((r, n), BF16), jax.ShapeDtypeStruct((r, KR_WIDTH), F32)],
        scratch_shapes=[pltpu.VMEM((tm, d), BF16)],
        compiler_params=_params(("parallel", "arbitrary"), est),
        name="in_proj",
    )(x2d, g1, w_main, w_kr, norm_g)


def _rope_half(t, cos, sin):
    return t * cos + pltpu.roll(t, 64, axis=1) * sin


def _mla_up_kernel(cq_ref, ckv_ref, kr_ref, cos_ref, sin_ref, wq_ref, wkv_ref, q_ref, k_ref, v_ref):
    cos = cos_ref[...]
    sin = sin_ref[...]
    scale = (MLA_NOPE + MLA_ROPE) ** -0.5 * LOG2_E
    yq = jnp.dot(cq_ref[...], wq_ref[...], preferred_element_type=F32)
    ykv = jnp.dot(ckv_ref[...], wkv_ref[...], preferred_element_type=F32)
    k_rope = _rope_half(kr_ref[...], cos, sin).astype(BF16)
    for h in range(MLA_HEADS):
        c0 = h * MLA_QK_PAD
        q_ref[h, :, 0:MLA_NOPE] = (yq[:, c0:c0 + MLA_NOPE] * scale).astype(BF16)
        q_ref[h, :, MLA_NOPE:] = (_rope_half(yq[:, c0 + MLA_NOPE:c0 + MLA_QK_PAD], cos, sin) * scale).astype(BF16)
        k_ref[h, :, 0:MLA_NOPE] = ykv[:, c0:c0 + MLA_NOPE].astype(BF16)
        k_ref[h, :, MLA_NOPE:] = k_rope
        v_ref[h] = ykv[:, c0 + MLA_NOPE:c0 + MLA_NOPE + MLA_V].astype(BF16)


def _mla_up(proj, kr, cos_t, sin_t, wq_all, wkv_all, *, tm):
    r = proj.shape[0]
    cq_blk = (3 * NA_WIDTH) // MLA_RANK
    n_pos_blk = cos_t.shape[0] // tm
    width = wq_all.shape[1]
    est = (2 * (2 * tm * MLA_RANK * 2 + 3 * tm * V7X_LANES * 4 + 2 * MLA_RANK * width * 2
                + MLA_HEADS * tm * (2 * MLA_QK_PAD + MLA_V) * 2) + 3 * tm * width * 4)
    return pl.pallas_call(
        _mla_up_kernel,
        grid=(r // tm,),
        in_specs=[
            pl.BlockSpec((tm, MLA_RANK), lambda i: (i, cq_blk)),
            pl.BlockSpec((tm, MLA_RANK), lambda i: (i, cq_blk + 1)),
            pl.BlockSpec((tm, KR_WIDTH), lambda i: (i, 0)),
            pl.BlockSpec((tm, V7X_LANES), lambda i: (i % n_pos_blk, 0)),
            pl.BlockSpec((tm, V7X_LANES), lambda i: (i % n_pos_blk, 0)),
            pl.BlockSpec((MLA_RANK, width), lambda i: (0, 0)),
            pl.BlockSpec((MLA_RANK, width), lambda i: (0, 0)),
        ],
        out_specs=[
            pl.BlockSpec((MLA_HEADS, tm, MLA_QK_PAD), lambda i: (0, i, 0)),
            pl.BlockSpec((MLA_HEADS, tm, MLA_QK_PAD), lambda i: (0, i, 0)),
            pl.BlockSpec((MLA_HEADS, tm, MLA_V), lambda i: (0, i, 0)),
        ],
        out_shape=[
            jax.ShapeDtypeStruct((MLA_HEADS, r, MLA_QK_PAD), BF16),
            jax.ShapeDtypeStruct((MLA_HEADS, r, MLA_QK_PAD), BF16),
            jax.ShapeDtypeStruct((MLA_HEADS, r, MLA_V), BF16),
        ],
        compiler_params=_params(("parallel",), est),
        name="mla_up",
    )(proj, proj, kr, cos_t, sin_t, wq_all, wkv_all)


def _mla_attn_kernel(q_ref, k_ref, v_ref, qm_ref, km_ref, vm_ref, o_ref, om_ref, *, tq):
    k = k_ref[0]
    v = v_ref[0]
    km = _pad_meta_rows(km_ref[0])
    vm = _pad_meta_rows(vm_ref[0])
    meta_lane = lax.broadcasted_iota(jnp.int32, (1, V7X_LANES), 1) < N_META

    def attend(q):
        s = _dot_nt(q, k)
        sm = jnp.where(meta_lane, _dot_nt(q, km), NEG_INF)
        m = jnp.maximum(jnp.max(s, axis=-1, keepdims=True), jnp.max(sm, axis=-1, keepdims=True))
        p = jnp.exp2(s - m)
        pm = jnp.exp2(sm - m)
        l = jnp.sum(p, axis=-1, keepdims=True) + jnp.sum(pm, axis=-1, keepdims=True)
        o = (jnp.dot(p.astype(BF16), v, preferred_element_type=F32)
             + jnp.dot(pm.astype(BF16), vm, preferred_element_type=F32))
        return o / l

    s_len = q_ref.shape[1]
    for c in range(s_len // tq):
        rows = slice(c * tq, (c + 1) * tq)
        o_ref[rows, :] = attend(q_ref[0, rows, :]).astype(BF16)
    om_ref[...] = attend(qm_ref[0]).astype(BF16)


def _mla_attn(q, k, v, qm, km, vm, *, batch):
    r = q.shape[1]
    s_len = r // batch
    tq = min(MLA_Q_CHUNK, s_len)
    est = 2 * (2 * s_len * 256 * 2 + 2 * s_len * 128 * 2) + 3 * tq * s_len * 4 + tq * s_len * 2
    return pl.pallas_call(
        functools.partial(_mla_attn_kernel, tq=tq),
        grid=(batch, MLA_HEADS),
        in_specs=[
            pl.BlockSpec((1, s_len, MLA_QK_PAD), lambda b, h: (h, b, 0)),
            pl.BlockSpec((1, s_len, MLA_QK_PAD), lambda b, h: (h, b, 0)),
            pl.BlockSpec((1, s_len, MLA_V), lambda b, h: (h, b, 0)),
            pl.BlockSpec((1, N_META, MLA_QK_PAD), lambda b, h: (h, b, 0)),
            pl.BlockSpec((1, N_META, MLA_QK_PAD), lambda b, h: (h, b, 0)),
            pl.BlockSpec((1, N_META, MLA_V), lambda b, h: (h, b, 0)),
        ],
        out_specs=[
            pl.BlockSpec((s_len, MLA_V), lambda b, h: (b, h)),
            pl.BlockSpec((N_META, MLA_V), lambda b, h: (b, h)),
        ],
        out_shape=[
            jax.ShapeDtypeStruct((r, MLA_HEADS * MLA_V), BF16),
            jax.ShapeDtypeStruct((batch * N_META, MLA_HEADS * MLA_V), BF16),
        ],
        compiler_params=_params(("parallel", "parallel"), est),
        name="mla_attn",
    )(q, k, v, qm, km, vm)


def _na_bias_tiles(rel_bias):
    qc = np.arange(GRID_W)
    kc = np.arange(GRID_W)
    cstart = np.clip(qc - NA_WIN_W // 2, 0, GRID_W - NA_WIN_W)
    col_ok = (kc[None, :] >= cstart[:, None]) & (kc[None, :] < cstart[:, None] + NA_WIN_W)
    col_off = np.clip(kc[None, :] - qc[:, None], -(NA_WIN_W - 1), NA_WIN_W - 1) + NA_WIN_W - 1
    n_co = 2 * NA_WIN_W - 1
    onehot = np.zeros((n_co, GRID_W * GRID_W), np.float32)
    onehot[col_off.reshape(-1), np.arange(GRID_W * GRID_W)] = 1.0
    h, n_ro = rel_bias.shape[0], rel_bias.shape[1]
    tiles = jnp.dot(rel_bias.astype(F32).reshape(h * n_ro, n_co), jnp.asarray(onehot),
                    precision=lax.Precision.HIGHEST).reshape(h, n_ro, GRID_W, GRID_W)
    return jnp.where(jnp.asarray(col_ok)[None, None], tiles, NEG_INF)


def _na_block_plan(rows):
    nqb = rows // NA_QROWS
    wh = NA_WIN_H
    plan = []
    for qb in (0, 1, nqb - 1):
        kstart = int(np.clip(NA_QROWS * qb - wh // 2, 0, rows - NA_KROWS))
        variant = []
        for i in range(NA_QROWS):
            r = NA_QROWS * qb + i
            rs = int(np.clip(r - wh // 2, 0, rows - wh))
            variant.append([kstart + j - r + wh - 1 if rs <= kstart + j < rs + wh else None
                            for j in range(NA_KROWS)])
        plan.append(variant)
    return plan


def _na_attn_kernel(q_ref, k_ref, v_ref, qm_ref, km_ref, vm_ref, tiles_ref, mb_ref, o_ref, om_ref, bias_ref,
                    *, rows):
    nqb = rows // NA_QROWS
    qblk = NA_QROWS * GRID_W
    kblk = NA_KROWS * GRID_W

    @pl.when(pl.program_id(1) == 0)
    def _():
        masked = jnp.full((GRID_W, GRID_W), NEG_INF, F32)
        for hh in range(2):
            for v, variant in enumerate(_na_block_plan(rows)):
                for i, row_plan in enumerate(variant):
                    for j0 in range(0, NA_KROWS, 2):
                        pair = [masked if ro is None else tiles_ref[0, hh, ro] for ro in row_plan[j0:j0 + 2]]
                        bias_ref[hh, v, i * GRID_W:(i + 1) * GRID_W, j0 * GRID_W:(j0 + 2) * GRID_W] = (
                            jnp.concatenate(pair, axis=1))

    km = _pad_meta_rows(km_ref[...])
    vm = _pad_meta_rows(vm_ref[...])
    lane = lax.broadcasted_iota(jnp.int32, (1, 2 * NA_HEAD_DIM), 1)
    head_lanes = [lane < NA_HEAD_DIM, lane >= NA_HEAD_DIM]

    def softmax_pv(s, sm, v_loc):
        m = jnp.maximum(jnp.max(s, axis=-1, keepdims=True), jnp.max(sm, axis=-1, keepdims=True))
        p = jnp.exp(s - m)
        pm = jnp.exp(sm - m)
        l = jnp.sum(p, axis=-1, keepdims=True) + jnp.sum(pm, axis=-1, keepdims=True)
        o = (jnp.dot(p.astype(BF16), v_loc, preferred_element_type=F32)
             + jnp.dot(pm.astype(BF16), vm, preferred_element_type=F32))
        return o / l

    def block(qb, carry):
        q0 = pl.multiple_of(qb * qblk, qblk)
        k0 = pl.multiple_of(jnp.clip(NA_QROWS * qb - NA_WIN_H // 2, 0, rows - NA_KROWS) * GRID_W, GRID_W)
        variant = jnp.where(qb == 0, 0, jnp.where(qb == nqb - 1, 2, 1))
        q = q_ref[pl.ds(q0, qblk), :]
        k_loc = k_ref[pl.ds(k0, kblk), :]
        v_loc = v_ref[pl.ds(k0, kblk), :]
        outs = []
        for hh in range(2):
            qh = jnp.where(head_lanes[hh], q, jnp.zeros_like(q))
            s = _dot_nt(qh, k_loc) + bias_ref[hh, variant]
            sm = _dot_nt(qh, km) + mb_ref[0, hh]
            outs.append(softmax_pv(s, sm, v_loc))
        o_ref[pl.ds(q0, qblk), :] = jnp.where(head_lanes[0], outs[0], outs[1]).astype(BF16)
        return carry

    lax.fori_loop(0, nqb, block, 0)

    qm = qm_ref[...]
    outs = []
    for hh in range(2):
        qh = jnp.where(head_lanes[hh], qm, jnp.zeros_like(qm))
        sm = _dot_nt(qh, km) + mb_ref[0, hh]
        m = jnp.max(sm, axis=-1, keepdims=True)
        pm = jnp.exp(sm - m)
        l = jnp.sum(pm, axis=-1, keepdims=True)
        outs.append(jnp.dot(pm.astype(BF16), vm, preferred_element_type=F32) / l)
    om_ref[...] = jnp.where(head_lanes[0], outs[0], outs[1]).astype(BF16)


def _na_attn(proj, proj_m, bias_tiles, meta_bias, *, batch):
    r = proj.shape[0]
    s_len = r // batch
    rows = s_len // GRID_W
    assert rows % NA_QROWS == 0 and rows >= NA_KROWS, rows
    npair = NA_HEADS // 2
    qblk = NA_QROWS * GRID_W
    kblk = NA_KROWS * GRID_W
    n_ro = bias_tiles.shape[1]
    tiles5 = bias_tiles.reshape(npair, 2, n_ro, GRID_W, GRID_W)
    mb4 = jnp.concatenate([meta_bias.astype(F32), jnp.full((NA_HEADS, V7X_LANES - N_META), NEG_INF, F32)],
                          axis=1).reshape(npair, 2, 1, V7X_LANES)
    est = 2 * (4 * s_len * NA_PAIR_WIDTH * 2) + 12 * qblk * kblk * 4
    return pl.pallas_call(
        functools.partial(_na_attn_kernel, rows=rows),
        grid=(npair, batch),
        in_specs=[
            pl.BlockSpec((s_len, NA_PAIR_WIDTH), lambda p, b: (b, p)),
            pl.BlockSpec((s_len, NA_PAIR_WIDTH), lambda p, b: (b, npair + p)),
            pl.BlockSpec((s_len, NA_PAIR_WIDTH), lambda p, b: (b, 2 * npair + p)),
            pl.BlockSpec((N_META, NA_PAIR_WIDTH), lambda p, b: (b, p)),
            pl.BlockSpec((N_META, NA_PAIR_WIDTH), lambda p, b: (b, npair + p)),
            pl.BlockSpec((N_META, NA_PAIR_WIDTH), lambda p, b: (b, 2 * npair + p)),
            pl.BlockSpec((1, 2, n_ro, GRID_W, GRID_W), lambda p, b: (p, 0, 0, 0, 0)),
            pl.BlockSpec((1, 2, 1, V7X_LANES), lambda p, b: (p, 0, 0, 0)),
        ],
        out_specs=[
            pl.BlockSpec((s_len, NA_PAIR_WIDTH), lambda p, b: (b, p)),
            pl.BlockSpec((N_META, NA_PAIR_WIDTH), lambda p, b: (b, p)),
        ],
        out_shape=[
            jax.ShapeDtypeStruct((r, NA_WIDTH), BF16),
            jax.ShapeDtypeStruct((batch * N_META, NA_WIDTH), BF16),
        ],
        scratch_shapes=[pltpu.VMEM((2, 3, qblk, kblk), F32)],
        compiler_params=_params(("parallel", "arbitrary"), est),
        name="na_attn",
    )(proj, proj, proj, proj_m, proj_m, proj_m, tiles5, mb4)


def _merge_kernel(ona_ref, omla_ref, wna_ref, wmla_ref, ga_ref, gb_ref, o_ref):
    a = jnp.dot(ona_ref[...], wna_ref[...], preferred_element_type=F32)
    b = jnp.dot(omla_ref[...], wmla_ref[...], preferred_element_type=F32)
    o_ref[...] = (ga_ref[...].astype(F32) * a + gb_ref[...].astype(F32) * b).astype(BF16)


def _merge(o_na, o_mla, w_na, w_mla, proj, *, tm):
    r = o_na.shape[0]
    d = w_na.shape[1]
    tn = _col_tile(d)
    ga_blk = (3 * NA_WIDTH + 2 * MLA_RANK) // tn
    gb_blk = ga_blk + d // tn
    est = 2 * (tm * (NA_WIDTH + MLA_HEADS * MLA_V) * 2 + (NA_WIDTH + MLA_HEADS * MLA_V) * tn * 2
               + 3 * tm * tn * 2) + 3 * tm * tn * 4
    return pl.pallas_call(
        _merge_kernel,
        grid=(r // tm, d // tn),
        in_specs=[
            pl.BlockSpec((tm, NA_WIDTH), lambda i, j: (i, 0)),
            pl.BlockSpec((tm, MLA_HEADS * MLA_V), lambda i, j: (i, 0)),
            pl.BlockSpec((NA_WIDTH, tn), lambda i, j: (0, j)),
            pl.BlockSpec((MLA_HEADS * MLA_V, tn), lambda i, j: (0, j)),
            pl.BlockSpec((tm, tn), lambda i, j: (i, ga_blk + j)),
            pl.BlockSpec((tm, tn), lambda i, j: (i, gb_blk + j)),
        ],
        out_specs=pl.BlockSpec((tm, tn), lambda i, j: (i, j)),
        out_shape=jax.ShapeDtypeStruct((r, d), BF16),
        compiler_params=_params(("parallel", "arbitrary"), est),
        name="merge",
    )(o_na, o_mla, w_na, w_mla, proj, proj)


def _out_proj_kernel(m_ref, w_ref, x_ref, o_ref):
    o_ref[...] = x_ref[...] + jnp.dot(m_ref[...], w_ref[...], preferred_element_type=F32)


def _out_proj(merged, w_out, x2d, *, tm):
    r, d = x2d.shape
    tn = _col_tile(d)
    est = 2 * (tm * d * 2 + d * tn * 2 + 2 * tm * tn * 4) + tm * tn * 4
    return pl.pallas_call(
        _out_proj_kernel,
        grid=(r // tm, d // tn),
        in_specs=[
            pl.BlockSpec((tm, d), lambda i, j: (i, 0)),
            pl.BlockSpec((d, tn), lambda i, j: (0, j)),
            pl.BlockSpec((tm, tn), lambda i, j: (i, j)),
        ],
        out_specs=pl.BlockSpec((tm, tn), lambda i, j: (i, j)),
        out_shape=jax.ShapeDtypeStruct((r, d), F32),
        compiler_params=_params(("parallel", "arbitrary"), est),
        name="out_proj",
    )(merged, w_out, x2d)


def _peer_q_kernel(h_ref, g_ref, w_ref, q_ref, hnt_ref, hn_ref):
    @pl.when(pl.program_id(1) == 0)
    def _():
        hn = _rmsnorm(h_ref[...], g_ref[...])
        hn_ref[...] = hn.astype(BF16)
        hnt_ref[...] = hn.T.astype(BF16)

    q_ref[...] = jnp.dot(hn_ref[...], w_ref[...], preferred_element_type=F32).astype(BF16)


def _peer_q(h2, g2, w_q, *, tm):
    r, d = h2.shape
    n = w_q.shape[1]
    tn = COL_TILE
    est = 2 * (tm * d * 4 + d * tn * 2 + tm * tn * 2 + tm * d * 2) + tm * d * 2 + tm * tn * 4 + 2 * tm * d * 4
    return pl.pallas_call(
        _peer_q_kernel,
        grid=(r // tm, n // tn),
        in_specs=[
            pl.BlockSpec((tm, d), lambda i, j: (i, 0)),
            pl.BlockSpec((1, d), lambda i, j: (0, 0)),
            pl.BlockSpec((d, tn), lambda i, j: (0, j)),
        ],
        out_specs=[
            pl.BlockSpec((tm, tn), lambda i, j: (i, j)),
            pl.BlockSpec((d, tm), lambda i, j: (0, i)),
        ],
        out_shape=[jax.ShapeDtypeStruct((r, n), BF16), jax.ShapeDtypeStruct((d, r), BF16)],
        scratch_shapes=[pltpu.VMEM((tm, d), BF16)],
        compiler_params=_params(("parallel", "arbitrary"), est),
        name="peer_q",
    )(h2, g2, w_q)


def _compare_exchange(a, b):
    if a is None:
        return b, None
    if b is None:
        return a, None
    return jnp.maximum(a, b), jnp.minimum(a, b)


def _bitonic_merge(xs):
    n = len(xs)
    step = n // 2
    while step >= 1:
        for i in range(n):
            if i & step == 0:
                xs[i], xs[i + step] = _compare_exchange(xs[i], xs[i + step])
        step //= 2
    return xs


def _bitonic_sort(xs):
    n = len(xs)
    xs = list(xs)
    size = 2
    while size <= n:
        step = size // 2
        while step >= 1:
            for i in range(n):
                if i & step == 0:
                    hi, lo = _compare_exchange(xs[i], xs[i + step])
                    xs[i], xs[i + step] = (hi, lo) if i & size == 0 else (lo, hi)
            step //= 2
        size *= 2
    return xs


def _merge_topk(xs, ys, k):
    xs = list(xs) + [None] * (k - len(xs))
    ys = list(ys) + [None] * (k - len(ys))
    return _bitonic_merge([_compare_exchange(xs[i], ys[k - 1 - i])[0] for i in range(k)])


def _pow2_ceil(n):
    return 1 << (n - 1).bit_length()


def _sorted_topk_slabs(s, k):
    slabs = _bitonic_sort([s[8 * v:8 * v + 8, :] for v in range(s.shape[0] // 8)])
    for shift in (4, 2, 1):
        slabs = _merge_topk(slabs, [pltpu.roll(x, shift, axis=0) for x in slabs], k)
    return slabs


def _extract_topk(x, k):
    n, t = x.shape
    row = lax.broadcasted_iota(jnp.int32, (n, t), 0).astype(F32)
    krow = lax.broadcasted_iota(jnp.int32, (k, t), 0)

    def body(i, carry):
        x, rank, vals = carry
        m = jnp.max(x, axis=0, keepdims=True)
        first = jnp.min(jnp.where(x == m, row, float(n)), axis=0, keepdims=True)
        hit = row == first
        rank = jnp.where(hit, jnp.asarray(i, F32), rank)
        x = jnp.where(hit, _F32_MIN, x)
        vals = jnp.where(krow == i, m, vals)
        return x, rank, vals

    _, rank, vals = lax.fori_loop(
        0, k, body, (x, jnp.full((n, t), float(k), F32), jnp.zeros((k, t), F32)))
    return vals, rank


def _route_head_exact(s1, s2):
    k = PEER_TOPK
    top1, rank1 = _extract_topk(s1, k)
    top2, rank2 = _extract_topk(s2, k)

    tm = s1.shape[1]
    pieces, spans = [], []
    off = 0
    for i in range(k):
        cnt = k // (i + 1)
        rows = -(-cnt // 8) * 8
        jrow = lax.broadcasted_iota(jnp.int32, (rows, tm), 0)
        pieces.append(jnp.where(jrow < cnt, top1[i:i + 1, :] + top2[0:rows, :], _F32_MIN))
        spans.append((off, rows))
        off += rows
    cand = jnp.concatenate(pieces, axis=0)
    ctop, crank = _extract_topk(cand, k)
    chosen = crank < float(k)
    cmax = ctop[0:1, :]
    z = jnp.sum(jnp.where(chosen, jnp.exp(cand - cmax), 0.0), axis=0, keepdims=True)

    y1 = jnp.ones(s1.shape, F32)
    for i, (off, rows) in enumerate(spans):
        n_sel = jnp.sum(jnp.where(chosen[off:off + rows, :], 1.0, 0.0), axis=0, keepdims=True)
        y1 = jnp.where(rank1 == float(i), 1.0 - n_sel, y1)
    return y1, jnp.exp(s1 - top1[0:1, :]) / z, -rank2, jnp.exp(s2 - top2[0:1, :])


def _route_tables_by_value(scores):
    k = PEER_TOPK
    tm = scores[0][0].shape[1]

    sublane = lax.broadcasted_iota(jnp.int32, (8, tm), 0)
    tops = []
    for side in range(2):
        packed = None
        for h in range(PEER_HEADS):
            slabs = _sorted_topk_slabs(scores[h][side], k)
            packed = slabs if packed is None else [jnp.where(sublane == h, s, p) for s, p in zip(slabs, packed)]
        tops.append(packed)
    top1, top2 = tops

    cells = {(i, j): top1[i] + top2[j] for i in range(k) for j in range(k // (i + 1))}
    grid_rows = [[cells[i, j] for j in range(k // (i + 1))] for i in range(k)]
    lists = [r for r in grid_rows[1:] if len(r) > 1] + [[r[0] for r in grid_rows if len(r) == 1]]
    while len(lists) > 1:
        lists.sort(key=len)
        size = min(k, _pow2_ceil(len(lists[0]) + len(lists[1])))
        merged = [m for m in _merge_topk(lists[0], lists[1], size) if m is not None]
        lists = [merged] + lists[2:]
    ctop = _merge_topk(grid_rows[0], lists[0], k)
    thr, cmax = ctop[k - 1], ctop[0]

    taken = {c: v >= thr for c, v in cells.items()}
    n_taken = sum(jnp.where(t, 1.0, 0.0) for t in taken.values())
    z = sum(jnp.where(taken[c], jnp.exp(v - cmax), 0.0) for c, v in cells.items())
    inv_z = 1.0 / z
    y1_rows = [1.0 - sum(jnp.where(taken[i, j], 1.0, 0.0) for j in range(k // (i + 1))) for i in range(k)]

    exact = jnp.where(n_taken == float(k), 1.0, 0.0)
    for top in (top1, top2):
        for i in range(k - 1):
            exact = jnp.where(top[i] > top[i + 1], exact, 0.0)

    tables = []
    for h in range(PEER_HEADS):
        s1, s2 = scores[h]
        row = slice(h, h + 1)
        y1 = jnp.ones(s1.shape, F32)
        for i in range(k):
            y1 = jnp.where(s1 == top1[i][row], y1_rows[i][row], y1)
        y2 = jnp.full(s2.shape, -float(k), F32)
        for j in reversed(range(k)):
            y2 = jnp.where(s2 >= top2[j][row], -float(j), y2)
        tables.append((y1, jnp.exp(s1 - top1[0][row]) * inv_z[row], y2, jnp.exp(s2 - top2[0][row])))
        for s, top in ((s1, top1), (s2, top2)):
            members = jnp.sum(jnp.where(s >= top[k - 1][row], 1.0, 0.0), axis=0, keepdims=True)
            exact = jnp.where(jnp.logical_and(sublane == h, members != float(k)), 0.0, exact)
    return tables, exact


def _peer_route_kernel(q_ref, keys_ref, y1_ref, e1_ref, y2_ref, e2_ref):
    half = PEER_DK // 2
    scores = []
    for h in range(PEER_HEADS):
        q = q_ref[:, h * PEER_DK:(h + 1) * PEER_DK]
        scores.append((_dot_nt(keys_ref[h, 0], q[:, :half]),
                       _dot_nt(keys_ref[h, 1], q[:, half:])))

    def store(h, tables):
        y1, e1, y2, e2 = tables
        y1_ref[h] = y1
        e1_ref[h] = e1
        y2_ref[h] = pltpu.bitcast(y2.astype(BF16), jnp.uint32)
        e2_ref[h] = pltpu.bitcast(e2.astype(BF16), jnp.uint32)

    tables, exact = _route_tables_by_value(scores)
    for h in range(PEER_HEADS):
        store(h, tables[h])

    for h in range(PEER_HEADS):
        @pl.when(jnp.min(exact[h:h + 1, :]) < 0.5)
        def _(h=h):
            store(h, _route_head_exact(*scores[h]))


def _peer_route(q2, sub_keys, *, tm):
    r = q2.shape[0]
    width = PEER_HEADS * PEER_DK
    tab_spec = pl.BlockSpec((PEER_HEADS, PEER_NKEYS, tm), lambda i: (0, 0, i))
    tab = jax.ShapeDtypeStruct((PEER_HEADS, PEER_NKEYS, r), F32)
    packed_spec = pl.BlockSpec((PEER_HEADS, PEER_NKEYS // 2, tm), lambda i: (0, 0, i))
    packed = jax.ShapeDtypeStruct((PEER_HEADS, PEER_NKEYS // 2, r), jnp.uint32)
    est = (2 * (tm * width * 2 + PEER_HEADS * PEER_NKEYS * PEER_DK * 2 + PEER_HEADS * PEER_NKEYS * tm * 16)
           + 40 * PEER_HEADS * PEER_NKEYS * tm * 4)
    return pl.pallas_call(
        _peer_route_kernel,
        grid=(r // tm,),
        in_specs=[
            pl.BlockSpec((tm, width), lambda i: (i, 0)),
            pl.BlockSpec((PEER_HEADS, 2, PEER_NKEYS, PEER_DK // 2), lambda i: (0, 0, 0, 0)),
        ],
        out_specs=[tab_spec, tab_spec, packed_spec, packed_spec],
        out_shape=[tab, tab, packed, packed],
        compiler_params=_params(("parallel",), est),
        name="peer_route",
    )(q2, sub_keys)


def _peer_dense_kernel(hnt_ref, u_ref, vt_ref, y1_ref, e1_ref, y2_ref, e2_ref, h_ref, g_ref, o_ref,
                       acc_ref, act_ref, p_ref, *, te, final_norm):
    j = pl.program_id(1)
    d, tm = acc_ref.shape

    @pl.when(j == 0)
    def _():
        acc_ref[...] = jnp.zeros_like(acc_ref)

    act = jax.nn.gelu(jnp.dot(u_ref[...], hnt_ref[...], preferred_element_type=F32).astype(BF16))
    act_ref[...] = pltpu.bitcast(act, jnp.uint32)
    a_blocks = te // PEER_NKEYS
    a0 = j * a_blocks
    zero = jnp.zeros((), BF16)
    for ai in range(a_blocks):
        y1_rows = [jnp.broadcast_to(y1_ref[h, pl.ds(a0 + ai, 1), :], (BF16_ROWS, tm)).astype(BF16)
                   for h in range(PEER_HEADS)]
        e1_rows = [jnp.broadcast_to(e1_ref[h, pl.ds(a0 + ai, 1), :], (BF16_ROWS, tm)).astype(BF16)
                   for h in range(PEER_HEADS)]
        for c in range(tm // V7X_LANES):
            cols = slice(c * V7X_LANES, (c + 1) * V7X_LANES)
            for g in range(PEER_NKEYS // BF16_ROWS):
                b_words = slice(g * BF16_ROWS // 2, (g + 1) * BF16_ROWS // 2)
                gate = None
                for h in range(PEER_HEADS):
                    sel = pltpu.bitcast(y2_ref[h, b_words, cols], BF16) >= y1_rows[h][:, cols]
                    term = jnp.where(sel, e1_rows[h][:, cols] * pltpu.bitcast(e2_ref[h, b_words, cols], BF16), zero)
                    gate = term if gate is None else gate + term
                e_words = slice((ai * PEER_NKEYS + g * BF16_ROWS) // 2, (ai * PEER_NKEYS + (g + 1) * BF16_ROWS) // 2)
                p_ref[e_words, cols] = pltpu.bitcast(gate * pltpu.bitcast(act_ref[e_words, cols], BF16), jnp.uint32)
    acc_ref[...] += jnp.dot(vt_ref[...], pltpu.bitcast(p_ref[...], BF16), preferred_element_type=F32)

    @pl.when(j == pl.num_programs(1) - 1)
    def _():
        out = h_ref[...] + acc_ref[...].T
        o_ref[...] = _rmsnorm(out, g_ref[...]) if final_norm else out


def _peer_dense(hn2t, u, vt, tabs, h2, gf, *, tm, te, final_norm):
    d, r = hn2t.shape
    e = u.shape[0]
    y1, e1, y2, e2 = tabs
    tab_spec = pl.BlockSpec((PEER_HEADS, PEER_NKEYS, tm), lambda i, j: (0, 0, i))
    packed_spec = pl.BlockSpec((PEER_HEADS, PEER_NKEYS // 2, tm), lambda i, j: (0, 0, i))
    est = (2 * (tm * d * 2 + 2 * te * d * 2 + PEER_HEADS * PEER_NKEYS * tm * 12 + tm * d * 4)
           + tm * d * 4 + d * tm * 4 + 6 * te * tm * 4)
    return pl.pallas_call(
        functools.partial(_peer_dense_kernel, te=te, final_norm=final_norm),
        grid=(r // tm, e // te),
        in_specs=[
            pl.BlockSpec((d, tm), lambda i, j: (0, i)),
            pl.BlockSpec((te, d), lambda i, j: (j, 0)),
            pl.BlockSpec((d, te), lambda i, j: (0, j)),
            tab_spec, tab_spec, packed_spec, packed_spec,
            pl.BlockSpec((tm, d), lambda i, j: (i, 0), pipeline_mode=pl.Buffered(1)),
            pl.BlockSpec((1, d), lambda i, j: (0, 0)),
        ],
        out_specs=pl.BlockSpec((tm, d), lambda i, j: (i, 0)),
        out_shape=jax.ShapeDtypeStruct((r, d), F32),
        scratch_shapes=[pltpu.VMEM((d, tm), F32), pltpu.VMEM((te // 2, tm), jnp.uint32),
                        pltpu.VMEM((te // 2, tm), jnp.uint32)],
        compiler_params=_params(("parallel", "arbitrary"), est),
        name="peer_dense",
    )(hn2t, u, vt, y1, e1, y2, e2, h2, gf)


def _rope_tables(pos):
    inv_freq = ROPE_THETA ** (-jnp.arange(0, MLA_ROPE, 2, dtype=F32) / MLA_ROPE)
    ang = pos.astype(F32)[:, None] * inv_freq[None, :]
    cos, sin = jnp.cos(ang), jnp.sin(ang)
    zeros = jnp.zeros((pos.shape[0], V7X_LANES - MLA_ROPE), F32)
    return (jnp.concatenate([cos, cos, zeros], axis=1), jnp.concatenate([-sin, sin, zeros], axis=1))


def _swap_halves(w):
    half = w.shape[-1] // 2
    return jnp.concatenate([w[..., half:], w[..., :half]], axis=-1)


def _layer(x_real, x_meta, p, l, *, batch):
    d = x_real.shape[1]
    s_len = x_real.shape[0] // batch
    tm_real = _row_tile(x_real.shape[0], ROW_TILE)
    tm_meta = x_meta.shape[0]

    w_in = p["w_in"][l]
    kr0 = 3 * NA_WIDTH + 2 * MLA_RANK
    w_main = jnp.concatenate([w_in[:, :kr0], w_in[:, kr0 + MLA_ROPE:]], axis=1).astype(BF16)
    w_kr = w_in[:, kr0:kr0 + MLA_ROPE]
    w_kr = jnp.concatenate([w_kr, _swap_halves(w_kr)], axis=1).astype(BF16)
    norm_g = jnp.concatenate([p["mla_q_norm_g"][l], p["mla_kv_norm_g"][l]])[None].astype(F32)
    g1 = p["norm1_g"][l][None].astype(F32)
    wq = p["mla_w_uq"][l].reshape(MLA_RANK, MLA_HEADS, MLA_NOPE + MLA_ROPE)
    wq = jnp.concatenate([wq, _swap_halves(wq[..., MLA_NOPE:])], axis=-1)
    wq = wq.reshape(MLA_RANK, MLA_HEADS * MLA_QK_PAD).astype(BF16)
    wkv = p["mla_w_ukv"][l].astype(BF16)
    cos_r, sin_r = _rope_tables(N_META + jnp.arange(s_len))
    cos_m, sin_m = _rope_tables(jnp.tile(jnp.arange(N_META), batch))
    bias_tiles = _na_bias_tiles(p["na_rel_bias"][l])

    proj_r, kr_r = _in_proj(x_real, g1, w_main, w_kr, norm_g, tm=tm_real)
    proj_m, kr_m = _in_proj(x_meta, g1, w_main, w_kr, norm_g, tm=tm_meta)
    q_r, k_r, v_r = _mla_up(proj_r, kr_r, cos_r, sin_r, wq, wkv, tm=_row_tile(s_len, MLA_UP_TILE))
    q_m, k_m, v_m = _mla_up(proj_m, kr_m, cos_m, sin_m, wq, wkv, tm=tm_meta)
    omla_r, omla_m = _mla_attn(q_r, k_r, v_r, q_m, k_m, v_m, batch=batch)
    ona_r, ona_m = _na_attn(proj_r, proj_m, bias_tiles, p["na_meta_bias"][l], batch=batch)

    w_na = p["w_na_branch"][l].astype(BF16)
    w_mla = p["w_mla_branch"][l].astype(BF16)
    w_out = p["w_out"][l].astype(BF16)
    h_real = _out_proj(_merge(ona_r, omla_r, w_na, w_mla, proj_r, tm=tm_real), w_out, x_real, tm=tm_real)
    h_meta = _out_proj(_merge(ona_m, omla_m, w_na, w_mla, proj_m, tm=tm_meta), w_out, x_meta, tm=tm_meta)
    return h_real, h_meta


def _peer(h2, p, l, gf, *, tm, te, final_norm):
    g2 = p["norm2_g"][l][None].astype(F32)
    w_q = p["peer_w_q"][l].astype(BF16)
    keys = p["peer_sub_keys"][l].astype(BF16)
    u = p["peer_u"][l].astype(BF16)
    vt = p["peer_v"][l].astype(BF16).T
    q2, hn2t = _peer_q(h2, g2, w_q, tm=_row_tile(h2.shape[0], ROW_TILE))
    tabs = _peer_route(q2, keys, tm=_row_tile(h2.shape[0], ROUTE_TILE))
    return _peer_dense(hn2t, u, vt, tabs, h2, gf, tm=tm, te=te, final_norm=final_norm)


def kernel(x, meta_tokens, norm1_g, w_in, na_rel_bias, na_meta_bias, mla_q_norm_g, mla_w_uq, mla_kv_norm_g,
           mla_w_ukv, w_na_branch, w_mla_branch, w_out, norm2_g, peer_w_q, peer_sub_keys, peer_u, peer_v,
           final_norm_g):
    b, s, d = x.shape
    depth = w_in.shape[0]
    p = dict(norm1_g=norm1_g, w_in=w_in, na_rel_bias=na_rel_bias, na_meta_bias=na_meta_bias,
             mla_q_norm_g=mla_q_norm_g, mla_w_uq=mla_w_uq, mla_kv_norm_g=mla_kv_norm_g, mla_w_ukv=mla_w_ukv,
             w_na_branch=w_na_branch, w_mla_branch=w_mla_branch, w_out=w_out, norm2_g=norm2_g,
             peer_w_q=peer_w_q, peer_sub_keys=peer_sub_keys, peer_u=peer_u, peer_v=peer_v)
    x_real = x.reshape(b * s, d)
    x_meta = jnp.broadcast_to(meta_tokens.astype(x.dtype)[None], (b, N_META, d)).reshape(b * N_META, d)
    gf = final_norm_g[None].astype(F32)
    tm = _row_tile(b * s, EXPERT_TOKEN_TILE)
    te = EXPERT_BLOCK
    for l in range(depth):
        h_real, h_meta = _layer(x_real, x_meta, p, l, batch=b)
        x_real = _peer(h_real, p, l, gf, tm=tm, te=te, final_norm=l == depth - 1)
        x_meta = _peer(h_meta, p, l, gf, tm=h_meta.shape[0], te=te, final_norm=False)
    return x_real.reshape(b, s, d)
```

```python
import functools

import jax
import jax.numpy as jnp
import numpy as np
from jax import lax
from jax.experimental import pallas as pl
from jax.experimental.pallas import tpu as pltpu

GRID_W = 64
N_META = 16
NA_HEADS = 16
NA_HEAD_DIM = 64
NA_WIN_H = 8
NA_WIN_W = 16
NA_WIDTH = NA_HEADS * NA_HEAD_DIM
MLA_HEADS = 16
MLA_RANK = 512
MLA_NOPE = 128
MLA_ROPE = 64
MLA_V = 128
MLA_QK_PAD = 256
ROPE_THETA = 10000.0
PEER_HEADS = 8
PEER_NKEYS = 128
PEER_DK = 256
PEER_TOPK = 16
NORM_EPS = 1e-6
NEG_INF = -1e30

V7X_VMEM_BYTES = 64 * 1024 * 1024
V7X_VMEM_HEADROOM = 6 * 1024 * 1024
V7X_LANES = 128
BF16_ROWS = 16

NA_QROWS = 4
NA_KROWS = NA_QROWS + NA_WIN_H
NA_PAIR_WIDTH = 2 * NA_HEAD_DIM
KR_WIDTH = 2 * MLA_ROPE

ROW_TILE = 1024
COL_TILE = 1024
MLA_UP_TILE = 256
MLA_Q_CHUNK = 512
ROUTE_TILE = 256
EXPERT_TOKEN_TILE = 512
EXPERT_BLOCK = 1024

F32 = jnp.float32
BF16 = jnp.bfloat16
_F32_MIN = float(np.finfo(np.float32).min)


def _vmem_limit(estimate_bytes):
    want = int(estimate_bytes * 1.25) + 4 * 1024 * 1024
    return min(max(want, 16 * 1024 * 1024), V7X_VMEM_BYTES - V7X_VMEM_HEADROOM)


def _params(semantics, vmem_estimate):
    return pltpu.CompilerParams(dimension_semantics=semantics, vmem_limit_bytes=_vmem_limit(vmem_estimate))


def _row_tile(rows, want):
    tile = min(rows, want)
    assert rows % tile == 0, (rows, tile)
    return tile


def _col_tile(d):
    return COL_TILE if d % COL_TILE == 0 else COL_TILE // 2


def _dot_nt(a, b):
    return lax.dot_general(a, b, (((1,), (1,)), ((), ())), preferred_element_type=F32)


def _rmsnorm(x, g):
    return x * lax.rsqrt(jnp.mean(x * x, axis=-1, keepdims=True) + NORM_EPS) * g


def _pad_meta_rows(a):
    return jnp.concatenate([a, jnp.zeros((V7X_LANES - N_META, a.shape[1]), a.dtype)], axis=0)


def _in_proj_kernel(x_ref, g_ref, w_ref, wkr_ref, ng_ref, o_ref, kr_ref, hn_ref, *, tn):
    j = pl.program_id(1)

    @pl.when(j == 0)
    def _():
        hn = _rmsnorm(x_ref[...], g_ref[...]).astype(BF16)
        hn_ref[...] = hn
        kr_ref[...] = jnp.dot(hn, wkr_ref[...], preferred_element_type=F32)

    acc = jnp.dot(hn_ref[...], w_ref[...], preferred_element_type=F32)
    col0 = j * tn
    plain_end = 3 * NA_WIDTH
    norm_end = plain_end + 2 * MLA_RANK

    @pl.when(col0 < plain_end)
    def _():
        scale = jnp.where(col0 < NA_WIDTH, NA_HEAD_DIM ** -0.5, 1.0).astype(F32)
        o_ref[...] = (acc * scale).astype(BF16)

    @pl.when(jnp.logical_and(col0 >= plain_end, col0 < norm_end))
    def _():
        for c in range(tn // MLA_RANK):
            sl = slice(c * MLA_RANK, (c + 1) * MLA_RANK)
            o_ref[:, sl] = _rmsnorm(acc[:, sl], ng_ref[:, sl]).astype(BF16)

    @pl.when(col0 >= norm_end)
    def _():
        o_ref[...] = jax.nn.sigmoid(acc).astype(BF16)


def _in_proj(x2d, g1, w_main, w_kr, norm_g, *, tm):
    r, d = x2d.shape
    n = w_main.shape[1]
    tn = _col_tile(d)
    norm_blk0 = (3 * NA_WIDTH) // tn
    norm_nblk = (2 * MLA_RANK) // tn
    est = 2 * (tm * d * 4 + d * tn * 2 + tm * tn * 2 + tm * KR_WIDTH * 4 + d * KR_WIDTH * 2) + tm * d * 2 + tm * tn * 4
    return pl.pallas_call(
        functools.partial(_in_proj_kernel, tn=tn),
        grid=(r // tm, n // tn),
        in_specs=[
            pl.BlockSpec((tm, d), lambda i, j: (i, 0)),
            pl.BlockSpec((1, d), lambda i, j: (0, 0)),
            pl.BlockSpec((d, tn), lambda i, j: (0, j)),
            pl.BlockSpec((d, KR_WIDTH), lambda i, j: (0, 0)),
            pl.BlockSpec((1, tn), lambda i, j: (0, jnp.clip(j - norm_blk0, 0, norm_nblk - 1))),
        ],
        out_specs=[
            pl.BlockSpec((tm, tn), lambda i, j: (i, j)),
            pl.BlockSpec((tm, KR_WIDTH), lambda i, j: (i, 0)),
        ],
        out_shape=[jax.ShapeDtypeStruct((r, n), BF16), jax.ShapeDtypeStruct((r, KR_WIDTH), F32)],
        scratch_shapes=[pltpu.VMEM((tm, d), BF16)],
        compiler_params=_params(("parallel", "arbitrary"), est),
        name="in_proj",
    )(x2d, g1, w_main, w_kr, norm_g)


def _rope_half(t, cos, sin):
    return t * cos + pltpu.roll(t, 64, axis=1) * sin


def _mla_up_kernel(cq_ref, ckv_ref, kr_ref, cos_ref, sin_ref, wq_ref, wkv_ref, q_ref, k_ref, v_ref):
    cos = cos_ref[...]
    sin = sin_ref[...]
    scale = (MLA_NOPE + MLA_ROPE) ** -0.5
    yq = jnp.dot(cq_ref[...], wq_ref[...], preferred_element_type=F32)
    ykv = jnp.dot(ckv_ref[...], wkv_ref[...], preferred_element_type=F32)
    k_rope = _rope_half(kr_ref[...], cos, sin).astype(BF16)
    for h in range(MLA_HEADS):
        c0 = h * MLA_QK_PAD
        q_ref[h, :, 0:MLA_NOPE] = (yq[:, c0:c0 + MLA_NOPE] * scale).astype(BF16)
        q_ref[h, :, MLA_NOPE:] = (_rope_half(yq[:, c0 + MLA_NOPE:c0 + MLA_QK_PAD], cos, sin) * scale).astype(BF16)
        k_ref[h, :, 0:MLA_NOPE] = ykv[:, c0:c0 + MLA_NOPE].astype(BF16)
        k_ref[h, :, MLA_NOPE:] = k_rope
        v_ref[h] = ykv[:, c0 + MLA_NOPE:c0 + MLA_NOPE + MLA_V].astype(BF16)


def _mla_up(proj, kr, cos_t, sin_t, wq_all, wkv_all, *, tm):
    r = proj.shape[0]
    cq_blk = (3 * NA_WIDTH) // MLA_RANK
    n_pos_blk = cos_t.shape[0] // tm
    width = wq_all.shape[1]
    est = (2 * (2 * tm * MLA_RANK * 2 + 3 * tm * V7X_LANES * 4 + 2 * MLA_RANK * width * 2
                + MLA_HEADS * tm * (2 * MLA_QK_PAD + MLA_V) * 2) + 3 * tm * width * 4)
    return pl.pallas_call(
        _mla_up_kernel,
        grid=(r // tm,),
        in_specs=[
            pl.BlockSpec((tm, MLA_RANK), lambda i: (i, cq_blk)),
            pl.BlockSpec((tm, MLA_RANK), lambda i: (i, cq_blk + 1)),
            pl.BlockSpec((tm, KR_WIDTH), lambda i: (i, 0)),
            pl.BlockSpec((tm, V7X_LANES), lambda i: (i % n_pos_blk, 0)),
            pl.BlockSpec((tm, V7X_LANES), lambda i: (i % n_pos_blk, 0)),
            pl.BlockSpec((MLA_RANK, width), lambda i: (0, 0)),
            pl.BlockSpec((MLA_RANK, width), lambda i: (0, 0)),
        ],
        out_specs=[
            pl.BlockSpec((MLA_HEADS, tm, MLA_QK_PAD), lambda i: (0, i, 0)),
            pl.BlockSpec((MLA_HEADS, tm, MLA_QK_PAD), lambda i: (0, i, 0)),
            pl.BlockSpec((MLA_HEADS, tm, MLA_V), lambda i: (0, i, 0)),
        ],
        out_shape=[
            jax.ShapeDtypeStruct((MLA_HEADS, r, MLA_QK_PAD), BF16),
            jax.ShapeDtypeStruct((MLA_HEADS, r, MLA_QK_PAD), BF16),
            jax.ShapeDtypeStruct((MLA_HEADS, r, MLA_V), BF16),
        ],
        compiler_params=_params(("parallel",), est),
        name="mla_up",
    )(proj, proj, kr, cos_t, sin_t, wq_all, wkv_all)


def _mla_attn_kernel(q_ref, k_ref, v_ref, qm_ref, km_ref, vm_ref, o_ref, om_ref, *, tq):
    k = k_ref[0]
    v = v_ref[0]
    km = _pad_meta_rows(km_ref[0])
    vm = _pad_meta_rows(vm_ref[0])
    meta_lane = lax.broadcasted_iota(jnp.int32, (1, V7X_LANES), 1) < N_META

    def attend(q):
        s = _dot_nt(q, k)
        sm = jnp.where(meta_lane, _dot_nt(q, km), NEG_INF)
        m = jnp.maximum(jnp.max(s, axis=-1, keepdims=True), jnp.max(sm, axis=-1, keepdims=True))
        p = jnp.exp(s - m)
        pm = jnp.exp(sm - m)
        l = jnp.sum(p, axis=-1, keepdims=True) + jnp.sum(pm, axis=-1, keepdims=True)
        o = (jnp.dot(p.astype(BF16), v, preferred_element_type=F32)
             + jnp.dot(pm.astype(BF16), vm, preferred_element_type=F32))
        return o / l

    s_len = q_ref.shape[1]
    for c in range(s_len // tq):
        rows = slice(c * tq, (c + 1) * tq)
        o_ref[rows, :] = attend(q_ref[0, rows, :]).astype(BF16)
    om_ref[...] = attend(qm_ref[0]).astype(BF16)


def _mla_attn(q, k, v, qm, km, vm, *, batch):
    r = q.shape[1]
    s_len = r // batch
    tq = min(MLA_Q_CHUNK, s_len)
    est = 2 * (2 * s_len * 256 * 2 + 2 * s_len * 128 * 2) + 3 * tq * s_len * 4 + tq * s_len * 2
    return pl.pallas_call(
        functools.partial(_mla_attn_kernel, tq=tq),
        grid=(batch, MLA_HEADS),
        in_specs=[
            pl.BlockSpec((1, s_len, MLA_QK_PAD), lambda b, h: (h, b, 0)),
            pl.BlockSpec((1, s_len, MLA_QK_PAD), lambda b, h: (h, b, 0)),
            pl.BlockSpec((1, s_len, MLA_V), lambda b, h: (h, b, 0)),
            pl.BlockSpec((1, N_META, MLA_QK_PAD), lambda b, h: (h, b, 0)),
            pl.BlockSpec((1, N_META, MLA_QK_PAD), lambda b, h: (h, b, 0)),
            pl.BlockSpec((1, N_META, MLA_V), lambda b, h: (h, b, 0)),
        ],
        out_specs=[
            pl.BlockSpec((s_len, MLA_V), lambda b, h: (b, h)),
            pl.BlockSpec((N_META, MLA_V), lambda b, h: (b, h)),
        ],
        out_shape=[
            jax.ShapeDtypeStruct((r, MLA_HEADS * MLA_V), BF16),
            jax.ShapeDtypeStruct((batch * N_META, MLA_HEADS * MLA_V), BF16),
        ],
        compiler_params=_params(("parallel", "parallel"), est),
        name="mla_attn",
    )(q, k, v, qm, km, vm)


def _na_bias_tiles(rel_bias):
    qc = np.arange(GRID_W)
    kc = np.arange(GRID_W)
    cstart = np.clip(qc - NA_WIN_W // 2, 0, GRID_W - NA_WIN_W)
    col_ok = (kc[None, :] >= cstart[:, None]) & (kc[None, :] < cstart[:, None] + NA_WIN_W)
    col_off = np.clip(kc[None, :] - qc[:, None], -(NA_WIN_W - 1), NA_WIN_W - 1) + NA_WIN_W - 1
    n_co = 2 * NA_WIN_W - 1
    onehot = np.zeros((n_co, GRID_W * GRID_W), np.float32)
    onehot[col_off.reshape(-1), np.arange(GRID_W * GRID_W)] = 1.0
    h, n_ro = rel_bias.shape[0], rel_bias.shape[1]
    tiles = jnp.dot(rel_bias.astype(F32).reshape(h * n_ro, n_co), jnp.asarray(onehot),
                    precision=lax.Precision.HIGHEST).reshape(h, n_ro, GRID_W, GRID_W)
    return jnp.where(jnp.asarray(col_ok)[None, None], tiles, NEG_INF)


def _na_block_plan(rows):
    nqb = rows // NA_QROWS
    wh = NA_WIN_H
    plan = []
    for qb in (0, 1, nqb - 1):
        kstart = int(np.clip(NA_QROWS * qb - wh // 2, 0, rows - NA_KROWS))
        variant = []
        for i in range(NA_QROWS):
            r = NA_QROWS * qb + i
            rs = int(np.clip(r - wh // 2, 0, rows - wh))
            variant.append([kstart + j - r + wh - 1 if rs <= kstart + j < rs + wh else None
                            for j in range(NA_KROWS)])
        plan.append(variant)
    return plan


def _na_attn_kernel(q_ref, k_ref, v_ref, qm_ref, km_ref, vm_ref, tiles_ref, mb_ref, o_ref, om_ref, bias_ref,
                    *, rows):
    nqb = rows // NA_QROWS
    qblk = NA_QROWS * GRID_W
    kblk = NA_KROWS * GRID_W

    @pl.when(pl.program_id(1) == 0)
    def _():
        masked = jnp.full((GRID_W, GRID_W), NEG_INF, F32)
        for hh in range(2):
            for v, variant in enumerate(_na_block_plan(rows)):
                for i, row_plan in enumerate(variant):
                    for j0 in range(0, NA_KROWS, 2):
                        pair = [masked if ro is None else tiles_ref[0, hh, ro] for ro in row_plan[j0:j0 + 2]]
                        bias_ref[hh, v, i * GRID_W:(i + 1) * GRID_W, j0 * GRID_W:(j0 + 2) * GRID_W] = (
                            jnp.concatenate(pair, axis=1))

    km = _pad_meta_rows(km_ref[...])
    vm = _pad_meta_rows(vm_ref[...])
    lane = lax.broadcasted_iota(jnp.int32, (1, 2 * NA_HEAD_DIM), 1)
    head_lanes = [lane < NA_HEAD_DIM, lane >= NA_HEAD_DIM]

    def softmax_pv(s, sm, v_loc):
        m = jnp.maximum(jnp.max(s, axis=-1, keepdims=True), jnp.max(sm, axis=-1, keepdims=True))
        p = jnp.exp(s - m)
        pm = jnp.exp(sm - m)
        l = jnp.sum(p, axis=-1, keepdims=True) + jnp.sum(pm, axis=-1, keepdims=True)
        o = (jnp.dot(p.astype(BF16), v_loc, preferred_element_type=F32)
             + jnp.dot(pm.astype(BF16), vm, preferred_element_type=F32))
        return o / l

    def block(qb, carry):
        q0 = pl.multiple_of(qb * qblk, qblk)
        k0 = pl.multiple_of(jnp.clip(NA_QROWS * qb - NA_WIN_H // 2, 0, rows - NA_KROWS) * GRID_W, GRID_W)
        variant = jnp.where(qb == 0, 0, jnp.where(qb == nqb - 1, 2, 1))
        q = q_ref[pl.ds(q0, qblk), :]
        k_loc = k_ref[pl.ds(k0, kblk), :]
        v_loc = v_ref[pl.ds(k0, kblk), :]
        outs = []
        for hh in range(2):
            qh = jnp.where(head_lanes[hh], q, jnp.zeros_like(q))
            s = _dot_nt(qh, k_loc) + bias_ref[hh, variant]
            sm = _dot_nt(qh, km) + mb_ref[0, hh]
            outs.append(softmax_pv(s, sm, v_loc))
        o_ref[pl.ds(q0, qblk), :] = jnp.where(head_lanes[0], outs[0], outs[1]).astype(BF16)
        return carry

    lax.fori_loop(0, nqb, block, 0)

    qm = qm_ref[...]
    outs = []
    for hh in range(2):
        qh = jnp.where(head_lanes[hh], qm, jnp.zeros_like(qm))
        sm = _dot_nt(qh, km) + mb_ref[0, hh]
        m = jnp.max(sm, axis=-1, keepdims=True)
        pm = jnp.exp(sm - m)
        l = jnp.sum(pm, axis=-1, keepdims=True)
        outs.append(jnp.dot(pm.astype(BF16), vm, preferred_element_type=F32) / l)
    om_ref[...] = jnp.where(head_lanes[0], outs[0], outs[1]).astype(BF16)


def _na_attn(proj, proj_m, bias_tiles, meta_bias, *, batch):
    r = proj.shape[0]
    s_len = r // batch
    rows = s_len // GRID_W
    assert rows % NA_QROWS == 0 and rows >= NA_KROWS, rows
    npair = NA_HEADS // 2
    qblk = NA_QROWS * GRID_W
    kblk = NA_KROWS * GRID_W
    n_ro = bias_tiles.shape[1]
    tiles5 = bias_tiles.reshape(npair, 2, n_ro, GRID_W, GRID_W)
    mb4 = jnp.concatenate([meta_bias.astype(F32), jnp.full((NA_HEADS, V7X_LANES - N_META), NEG_INF, F32)],
                          axis=1).reshape(npair, 2, 1, V7X_LANES)
    est = 2 * (4 * s_len * NA_PAIR_WIDTH * 2) + 12 * qblk * kblk * 4
    return pl.pallas_call(
        functools.partial(_na_attn_kernel, rows=rows),
        grid=(npair, batch),
        in_specs=[
            pl.BlockSpec((s_len, NA_PAIR_WIDTH), lambda p, b: (b, p)),
            pl.BlockSpec((s_len, NA_PAIR_WIDTH), lambda p, b: (b, npair + p)),
            pl.BlockSpec((s_len, NA_PAIR_WIDTH), lambda p, b: (b, 2 * npair + p)),
            pl.BlockSpec((N_META, NA_PAIR_WIDTH), lambda p, b: (b, p)),
            pl.BlockSpec((N_META, NA_PAIR_WIDTH), lambda p, b: (b, npair + p)),
            pl.BlockSpec((N_META, NA_PAIR_WIDTH), lambda p, b: (b, 2 * npair + p)),
            pl.BlockSpec((1, 2, n_ro, GRID_W, GRID_W), lambda p, b: (p, 0, 0, 0, 0)),
            pl.BlockSpec((1, 2, 1, V7X_LANES), lambda p, b: (p, 0, 0, 0)),
        ],
        out_specs=[
            pl.BlockSpec((s_len, NA_PAIR_WIDTH), lambda p, b: (b, p)),
            pl.BlockSpec((N_META, NA_PAIR_WIDTH), lambda p, b: (b, p)),
        ],
        out_shape=[
            jax.ShapeDtypeStruct((r, NA_WIDTH), BF16),
            jax.ShapeDtypeStruct((batch * N_META, NA_WIDTH), BF16),
        ],
        scratch_shapes=[pltpu.VMEM((2, 3, qblk, kblk), F32)],
        compiler_params=_params(("parallel", "arbitrary"), est),
        name="na_attn",
    )(proj, proj, proj, proj_m, proj_m, proj_m, tiles5, mb4)


def _merge_kernel(ona_ref, omla_ref, wna_ref, wmla_ref, ga_ref, gb_ref, o_ref):
    a = jnp.dot(ona_ref[...], wna_ref[...], preferred_element_type=F32)
    b = jnp.dot(omla_ref[...], wmla_ref[...], preferred_element_type=F32)
    o_ref[...] = (ga_ref[...].astype(F32) * a + gb_ref[...].astype(F32) * b).astype(BF16)


def _merge(o_na, o_mla, w_na, w_mla, proj, *, tm):
    r = o_na.shape[0]
    d = w_na.shape[1]
    tn = _col_tile(d)
    ga_blk = (3 * NA_WIDTH + 2 * MLA_RANK) // tn
    gb_blk = ga_blk + d // tn
    est = 2 * (tm * (NA_WIDTH + MLA_HEADS * MLA_V) * 2 + (NA_WIDTH + MLA_HEADS * MLA_V) * tn * 2
               + 3 * tm * tn * 2) + 3 * tm * tn * 4
    return pl.pallas_call(
        _merge_kernel,
        grid=(r // tm, d // tn),
        in_specs=[
            pl.BlockSpec((tm, NA_WIDTH), lambda i, j: (i, 0)),
            pl.BlockSpec((tm, MLA_HEADS * MLA_V), lambda i, j: (i, 0)),
            pl.BlockSpec((NA_WIDTH, tn), lambda i, j: (0, j)),
            pl.BlockSpec((MLA_HEADS * MLA_V, tn), lambda i, j: (0, j)),
            pl.BlockSpec((tm, tn), lambda i, j: (i, ga_blk + j)),
            pl.BlockSpec((tm, tn), lambda i, j: (i, gb_blk + j)),
        ],
        out_specs=pl.BlockSpec((tm, tn), lambda i, j: (i, j)),
        out_shape=jax.ShapeDtypeStruct((r, d), BF16),
        compiler_params=_params(("parallel", "arbitrary"), est),
        name="merge",
    )(o_na, o_mla, w_na, w_mla, proj, proj)


def _out_proj_kernel(m_ref, w_ref, x_ref, o_ref):
    o_ref[...] = x_ref[...] + jnp.dot(m_ref[...], w_ref[...], preferred_element_type=F32)


def _out_proj(merged, w_out, x2d, *, tm):
    r, d = x2d.shape
    tn = _col_tile(d)
    est = 2 * (tm * d * 2 + d * tn * 2 + 2 * tm * tn * 4) + tm * tn * 4
    return pl.pallas_call(
        _out_proj_kernel,
        grid=(r // tm, d // tn),
        in_specs=[
            pl.BlockSpec((tm, d), lambda i, j: (i, 0)),
            pl.BlockSpec((d, tn), lambda i, j: (0, j)),
            pl.BlockSpec((tm, tn), lambda i, j: (i, j)),
        ],
        out_specs=pl.BlockSpec((tm, tn), lambda i, j: (i, j)),
        out_shape=jax.ShapeDtypeStruct((r, d), F32),
        compiler_params=_params(("parallel", "arbitrary"), est),
        name="out_proj",
    )(merged, w_out, x2d)


def _peer_q_kernel(h_ref, g_ref, w_ref, q_ref, hnt_ref, hn_ref):
    @pl.when(pl.program_id(1) == 0)
    def _():
        hn = _rmsnorm(h_ref[...], g_ref[...])
        hn_ref[...] = hn.astype(BF16)
        hnt_ref[...] = hn.T.astype(BF16)

    q_ref[...] = jnp.dot(hn_ref[...], w_ref[...], preferred_element_type=F32).astype(BF16)


def _peer_q(h2, g2, w_q, *, tm):
    r, d = h2.shape
    n = w_q.shape[1]
    tn = COL_TILE
    est = 2 * (tm * d * 4 + d * tn * 2 + tm * tn * 2 + tm * d * 2) + tm * d * 2 + tm * tn * 4 + 2 * tm * d * 4
    return pl.pallas_call(
        _peer_q_kernel,
        grid=(r // tm, n // tn),
        in_specs=[
            pl.BlockSpec((tm, d), lambda i, j: (i, 0)),
            pl.BlockSpec((1, d), lambda i, j: (0, 0)),
            pl.BlockSpec((d, tn), lambda i, j: (0, j)),
        ],
        out_specs=[
            pl.BlockSpec((tm, tn), lambda i, j: (i, j)),
            pl.BlockSpec((d, tm), lambda i, j: (0, i)),
        ],
        out_shape=[jax.ShapeDtypeStruct((r, n), BF16), jax.ShapeDtypeStruct((d, r), BF16)],
        scratch_shapes=[pltpu.VMEM((tm, d), BF16)],
        compiler_params=_params(("parallel", "arbitrary"), est),
        name="peer_q",
    )(h2, g2, w_q)


def _compare_exchange(a, b):
    if a is None:
        return b, None
    if b is None:
        return a, None
    return jnp.maximum(a, b), jnp.minimum(a, b)


def _bitonic_merge(xs):
    n = len(xs)
    step = n // 2
    while step >= 1:
        for i in range(n):
            if i & step == 0:
                xs[i], xs[i + step] = _compare_exchange(xs[i], xs[i + step])
        step //= 2
    return xs


def _bitonic_sort(xs):
    n = len(xs)
    xs = list(xs)
    size = 2
    while size <= n:
        step = size // 2
        while step >= 1:
            for i in range(n):
                if i & step == 0:
                    hi, lo = _compare_exchange(xs[i], xs[i + step])
                    xs[i], xs[i + step] = (hi, lo) if i & size == 0 else (lo, hi)
            step //= 2
        size *= 2
    return xs


def _merge_topk(xs, ys, k):
    xs = list(xs) + [None] * (k - len(xs))
    ys = list(ys) + [None] * (k - len(ys))
    return _bitonic_merge([_compare_exchange(xs[i], ys[k - 1 - i])[0] for i in range(k)])


def _pow2_ceil(n):
    return 1 << (n - 1).bit_length()


def _sorted_topk_slabs(s, k):
    slabs = _bitonic_sort([s[8 * v:8 * v + 8, :] for v in range(s.shape[0] // 8)])
    for shift in (4, 2, 1):
        slabs = _merge_topk(slabs, [pltpu.roll(x, shift, axis=0) for x in slabs], k)
    return slabs


def _extract_topk(x, k):
    n, t = x.shape
    row = lax.broadcasted_iota(jnp.int32, (n, t), 0).astype(F32)
    krow = lax.broadcasted_iota(jnp.int32, (k, t), 0)

    def body(i, carry):
        x, rank, vals = carry
        m = jnp.max(x, axis=0, keepdims=True)
        first = jnp.min(jnp.where(x == m, row, float(n)), axis=0, keepdims=True)
        hit = row == first
        rank = jnp.where(hit, jnp.asarray(i, F32), rank)
        x = jnp.where(hit, _F32_MIN, x)
        vals = jnp.where(krow == i, m, vals)
        return x, rank, vals

    _, rank, vals = lax.fori_loop(
        0, k, body, (x, jnp.full((n, t), float(k), F32), jnp.zeros((k, t), F32)))
    return vals, rank


def _route_head_exact(s1, s2):
    k = PEER_TOPK
    top1, rank1 = _extract_topk(s1, k)
    top2, rank2 = _extract_topk(s2, k)

    tm = s1.shape[1]
    pieces, spans = [], []
    off = 0
    for i in range(k):
        cnt = k // (i + 1)
        rows = -(-cnt // 8) * 8
        jrow = lax.broadcasted_iota(jnp.int32, (rows, tm), 0)
        pieces.append(jnp.where(jrow < cnt, top1[i:i + 1, :] + top2[0:rows, :], _F32_MIN))
        spans.append((off, rows))
        off += rows
    cand = jnp.concatenate(pieces, axis=0)
    ctop, crank = _extract_topk(cand, k)
    chosen = crank < float(k)
    cmax = ctop[0:1, :]
    z = jnp.sum(jnp.where(chosen, jnp.exp(cand - cmax), 0.0), axis=0, keepdims=True)

    y1 = jnp.ones(s1.shape, F32)
    for i, (off, rows) in enumerate(spans):
        n_sel = jnp.sum(jnp.where(chosen[off:off + rows, :], 1.0, 0.0), axis=0, keepdims=True)
        y1 = jnp.where(rank1 == float(i), 1.0 - n_sel, y1)
    return y1, jnp.exp(s1 - top1[0:1, :]) / z, -rank2, jnp.exp(s2 - top2[0:1, :])


def _route_tables_by_value(scores):
    k = PEER_TOPK
    tm = scores[0][0].shape[1]

    sublane = lax.broadcasted_iota(jnp.int32, (8, tm), 0)
    tops = []
    for side in range(2):
        packed = None
        for h in range(PEER_HEADS):
            slabs = _sorted_topk_slabs(scores[h][side], k)
            packed = slabs if packed is None else [jnp.where(sublane == h, s, p) for s, p in zip(slabs, packed)]
        tops.append(packed)
    top1, top2 = tops

    cells = {(i, j): top1[i] + top2[j] for i in range(k) for j in range(k // (i + 1))}
    grid_rows = [[cells[i, j] for j in range(k // (i + 1))] for i in range(k)]
    lists = [r for r in grid_rows[1:] if len(r) > 1] + [[r[0] for r in grid_rows if len(r) == 1]]
    while len(lists) > 1:
        lists.sort(key=len)
        size = min(k, _pow2_ceil(len(lists[0]) + len(lists[1])))
        merged = [m for m in _merge_topk(lists[0], lists[1], size) if m is not None]
        lists = [merged] + lists[2:]
    ctop = _merge_topk(grid_rows[0], lists[0], k)
    thr, cmax = ctop[k - 1], ctop[0]

    taken = {c: v >= thr for c, v in cells.items()}
    n_taken = sum(jnp.where(t, 1.0, 0.0) for t in taken.values())
    z = sum(jnp.where(taken[c], jnp.exp(v - cmax), 0.0) for c, v in cells.items())
    inv_z = 1.0 / z
    y1_rows = [1.0 - sum(jnp.where(taken[i, j], 1.0, 0.0) for j in range(k // (i + 1))) for i in range(k)]

    exact = jnp.where(n_taken == float(k), 1.0, 0.0)
    for top in (top1, top2):
        for i in range(k - 1):
            exact = jnp.where(top[i] > top[i + 1], exact, 0.0)

    tables = []
    for h in range(PEER_HEADS):
        s1, s2 = scores[h]
        row = slice(h, h + 1)
        y1 = jnp.ones(s1.shape, F32)
        for i in range(k):
            y1 = jnp.where(s1 == top1[i][row], y1_rows[i][row], y1)
        y2 = jnp.full(s2.shape, -float(k), F32)
        for j in reversed(range(k)):
            y2 = jnp.where(s2 >= top2[j][row], -float(j), y2)
        tables.append((y1, jnp.exp(s1 - top1[0][row]) * inv_z[row], y2, jnp.exp(s2 - top2[0][row])))
        for s, top in ((s1, top1), (s2, top2)):
            members = jnp.sum(jnp.where(s >= top[k - 1][row], 1.0, 0.0), axis=0, keepdims=True)
            exact = jnp.where(jnp.logical_and(sublane == h, members != float(k)), 0.0, exact)
    return tables, exact


def _peer_route_kernel(q_ref, keys_ref, y1_ref, e1_ref, y2_ref, e2_ref):
    half = PEER_DK // 2
    scores = []
    for h in range(PEER_HEADS):
        q = q_ref[:, h * PEER_DK:(h + 1) * PEER_DK]
        scores.append((_dot_nt(keys_ref[h, 0], q[:, :half]),
                       _dot_nt(keys_ref[h, 1], q[:, half:])))

    def store(h, cols, tables):
        y1, e1, y2, e2 = tables
        y1_ref[h, :, cols] = y1
        e1_ref[h, :, cols] = e1
        y2_ref[h, :, cols] = pltpu.bitcast(y2.astype(BF16), jnp.uint32)
        e2_ref[h, :, cols] = pltpu.bitcast(e2.astype(BF16), jnp.uint32)

    tm = q_ref.shape[0]
    tables, exact = _route_tables_by_value(scores)
    for h in range(PEER_HEADS):
        store(h, slice(0, tm), tables[h])

    for h in range(PEER_HEADS):
        for c0 in range(0, tm, V7X_LANES):
            cols = slice(c0, c0 + V7X_LANES)

            @pl.when(jnp.min(exact[h:h + 1, cols]) < 0.5)
            def _(h=h, cols=cols):
                store(h, cols, _route_head_exact(scores[h][0][:, cols], scores[h][1][:, cols]))


def _peer_route(q2, sub_keys, *, tm):
    r = q2.shape[0]
    width = PEER_HEADS * PEER_DK
    tab_spec = pl.BlockSpec((PEER_HEADS, PEER_NKEYS, tm), lambda i: (0, 0, i))
    tab = jax.ShapeDtypeStruct((PEER_HEADS, PEER_NKEYS, r), F32)
    packed_spec = pl.BlockSpec((PEER_HEADS, PEER_NKEYS // 2, tm), lambda i: (0, 0, i))
    packed = jax.ShapeDtypeStruct((PEER_HEADS, PEER_NKEYS // 2, r), jnp.uint32)
    est = (2 * (tm * width * 2 + PEER_HEADS * PEER_NKEYS * PEER_DK * 2 + PEER_HEADS * PEER_NKEYS * tm * 16)
           + 40 * PEER_HEADS * PEER_NKEYS * tm * 4)
    return pl.pallas_call(
        _peer_route_kernel,
        grid=(r // tm,),
        in_specs=[
            pl.BlockSpec((tm, width), lambda i: (i, 0)),
            pl.BlockSpec((PEER_HEADS, 2, PEER_NKEYS, PEER_DK // 2), lambda i: (0, 0, 0, 0)),
        ],
        out_specs=[tab_spec, tab_spec, packed_spec, packed_spec],
        out_shape=[tab, tab, packed, packed],
        compiler_params=_params(("parallel",), est),
        name="peer_route",
    )(q2, sub_keys)


def _peer_dense_kernel(hnt_ref, u_ref, vt_ref, y1_ref, e1_ref, y2_ref, e2_ref, h_ref, g_ref, o_ref,
                       acc_ref, act_ref, p_ref, *, te, final_norm):
    j = pl.program_id(1)
    d, tm = acc_ref.shape

    @pl.when(j == 0)
    def _():
        acc_ref[...] = jnp.zeros_like(acc_ref)

    act = jax.nn.gelu(jnp.dot(u_ref[...], hnt_ref[...], preferred_element_type=F32).astype(BF16))
    act_ref[...] = pltpu.bitcast(act, jnp.uint32)
    a_blocks = te // PEER_NKEYS
    a0 = j * a_blocks
    zero = jnp.zeros((), BF16)
    for ai in range(a_blocks):
        y1_rows = [jnp.broadcast_to(y1_ref[h, pl.ds(a0 + ai, 1), :], (BF16_ROWS, tm)).astype(BF16)
                   for h in range(PEER_HEADS)]
        e1_rows = [jnp.broadcast_to(e1_ref[h, pl.ds(a0 + ai, 1), :], (BF16_ROWS, tm)).astype(BF16)
                   for h in range(PEER_HEADS)]
        for c in range(tm // V7X_LANES):
            cols = slice(c * V7X_LANES, (c + 1) * V7X_LANES)
            for g in range(PEER_NKEYS // BF16_ROWS):
                b_words = slice(g * BF16_ROWS // 2, (g + 1) * BF16_ROWS // 2)
                gate = None
                for h in range(PEER_HEADS):
                    sel = pltpu.bitcast(y2_ref[h, b_words, cols], BF16) >= y1_rows[h][:, cols]
                    term = jnp.where(sel, e1_rows[h][:, cols] * pltpu.bitcast(e2_ref[h, b_words, cols], BF16), zero)
                    gate = term if gate is None else gate + term
                e_words = slice((ai * PEER_NKEYS + g * BF16_ROWS) // 2, (ai * PEER_NKEYS + (g + 1) * BF16_ROWS) // 2)
                p_ref[e_words, cols] = pltpu.bitcast(gate * pltpu.bitcast(act_ref[e_words, cols], BF16), jnp.uint32)
    acc_ref[...] += jnp.dot(vt_ref[...], pltpu.bitcast(p_ref[...], BF16), preferred_element_type=F32)

    @pl.when(j == pl.num_programs(1) - 1)
    def _():
        out = h_ref[...] + acc_ref[...].T
        o_ref[...] = _rmsnorm(out, g_ref[...]) if final_norm else out


def _peer_dense(hn2t, u, vt, tabs, h2, gf, *, tm, te, final_norm):
    d, r = hn2t.shape
    e = u.shape[0]
    y1, e1, y2, e2 = tabs
    tab_spec = pl.BlockSpec((PEER_HEADS, PEER_NKEYS, tm), lambda i, j: (0, 0, i))
    packed_spec = pl.BlockSpec((PEER_HEADS, PEER_NKEYS // 2, tm), lambda i, j: (0, 0, i))
    est = (2 * (tm * d * 2 + 2 * te * d * 2 + PEER_HEADS * PEER_NKEYS * tm * 12 + tm * d * 4)
           + tm * d * 4 + d * tm * 4 + 6 * te * tm * 4)
    return pl.pallas_call(
        functools.partial(_peer_dense_kernel, te=te, final_norm=final_norm),
        grid=(r // tm, e // te),
        in_specs=[
            pl.BlockSpec((d, tm), lambda i, j: (0, i)),
            pl.BlockSpec((te, d), lambda i, j: (j, 0)),
            pl.BlockSpec((d, te), lambda i, j: (0, j)),
            tab_spec, tab_spec, packed_spec, packed_spec,
            pl.BlockSpec((tm, d), lambda i, j: (i, 0), pipeline_mode=pl.Buffered(1)),
            pl.BlockSpec((1, d), lambda i, j: (0, 0)),
        ],
        out_specs=pl.BlockSpec((tm, d), lambda i, j: (i, 0)),
        out_shape=jax.ShapeDtypeStruct((r, d), F32),
        scratch_shapes=[pltpu.VMEM((d, tm), F32), pltpu.VMEM((te // 2, tm), jnp.uint32),
                        pltpu.VMEM((te // 2, tm), jnp.uint32)],
        compiler_params=_params(("parallel", "arbitrary"), est),
        name="peer_dense",
    )(hn2t, u, vt, y1, e1, y2, e2, h2, gf)


def _rope_tables(pos):
    inv_freq = ROPE_THETA ** (-jnp.arange(0, MLA_ROPE, 2, dtype=F32) / MLA_ROPE)
    ang = pos.astype(F32)[:, None] * inv_freq[None, :]
    cos, sin = jnp.cos(ang), jnp.sin(ang)
    zeros = jnp.zeros((pos.shape[0], V7X_LANES - MLA_ROPE), F32)
    return (jnp.concatenate([cos, cos, zeros], axis=1), jnp.concatenate([-sin, sin, zeros], axis=1))


def _swap_halves(w):
    half = w.shape[-1] // 2
    return jnp.concatenate([w[..., half:], w[..., :half]], axis=-1)


def _layer(x_real, x_meta, p, l, *, batch):
    d = x_real.shape[1]
    s_len = x_real.shape[0] // batch
    tm_real = _row_tile(x_real.shape[0], ROW_TILE)
    tm_meta = x_meta.shape[0]

    w_in = p["w_in"][l]
    kr0 = 3 * NA_WIDTH + 2 * MLA_RANK
    w_main = jnp.concatenate([w_in[:, :kr0], w_in[:, kr0 + MLA_ROPE:]], axis=1).astype(BF16)
    w_kr = w_in[:, kr0:kr0 + MLA_ROPE]
    w_kr = jnp.concatenate([w_kr, _swap_halves(w_kr)], axis=1).astype(BF16)
    norm_g = jnp.concatenate([p["mla_q_norm_g"][l], p["mla_kv_norm_g"][l]])[None].astype(F32)
    g1 = p["norm1_g"][l][None].astype(F32)
    wq = p["mla_w_uq"][l].reshape(MLA_RANK, MLA_HEADS, MLA_NOPE + MLA_ROPE)
    wq = jnp.concatenate([wq, _swap_halves(wq[..., MLA_NOPE:])], axis=-1)
    wq = wq.reshape(MLA_RANK, MLA_HEADS * MLA_QK_PAD).astype(BF16)
    wkv = p["mla_w_ukv"][l].astype(BF16)
    cos_r, sin_r = _rope_tables(N_META + jnp.arange(s_len))
    cos_m, sin_m = _rope_tables(jnp.tile(jnp.arange(N_META), batch))
    bias_tiles = _na_bias_tiles(p["na_rel_bias"][l])

    proj_r, kr_r = _in_proj(x_real, g1, w_main, w_kr, norm_g, tm=tm_real)
    proj_m, kr_m = _in_proj(x_meta, g1, w_main, w_kr, norm_g, tm=tm_meta)
    q_r, k_r, v_r = _mla_up(proj_r, kr_r, cos_r, sin_r, wq, wkv, tm=_row_tile(s_len, MLA_UP_TILE))
    q_m, k_m, v_m = _mla_up(proj_m, kr_m, cos_m, sin_m, wq, wkv, tm=tm_meta)
    omla_r, omla_m = _mla_attn(q_r, k_r, v_r, q_m, k_m, v_m, batch=batch)
    ona_r, ona_m = _na_attn(proj_r, proj_m, bias_tiles, p["na_meta_bias"][l], batch=batch)

    w_na = p["w_na_branch"][l].astype(BF16)
    w_mla = p["w_mla_branch"][l].astype(BF16)
    w_out = p["w_out"][l].astype(BF16)
    h_real = _out_proj(_merge(ona_r, omla_r, w_na, w_mla, proj_r, tm=tm_real), w_out, x_real, tm=tm_real)
    h_meta = _out_proj(_merge(ona_m, omla_m, w_na, w_mla, proj_m, tm=tm_meta), w_out, x_meta, tm=tm_meta)
    return h_real, h_meta


def _peer(h2, p, l, gf, *, tm, te, final_norm):
    g2 = p["norm2_g"][l][None].astype(F32)
    w_q = p["peer_w_q"][l].astype(BF16)
    keys = p["peer_sub_keys"][l].astype(BF16)
    u = p["peer_u"][l].astype(BF16)
    vt = p["peer_v"][l].astype(BF16).T
    q2, hn2t = _peer_q(h2, g2, w_q, tm=_row_tile(h2.shape[0], ROW_TILE))
    tabs = _peer_route(q2, keys, tm=_row_tile(h2.shape[0], ROUTE_TILE))
    return _peer_dense(hn2t, u, vt, tabs, h2, gf, tm=tm, te=te, final_norm=final_norm)


def kernel(x, meta_tokens, norm1_g, w_in, na_rel_bias, na_meta_bias, mla_q_norm_g, mla_w_uq, mla_kv_norm_g,
           mla_w_ukv, w_na_branch, w_mla_branch, w_out, norm2_g, peer_w_q, peer_sub_keys, peer_u, peer_v,
           final_norm_g):
    b, s, d = x.shape
    depth = w_in.shape[0]
    p = dict(norm1_g=norm1_g, w_in=w_in, na_rel_bias=na_rel_bias, na_meta_bias=na_meta_bias,
             mla_q_norm_g=mla_q_norm_g, mla_w_uq=mla_w_uq, mla_kv_norm_g=mla_kv_norm_g, mla_w_ukv=mla_w_ukv,
             w_na_branch=w_na_branch, w_mla_branch=w_mla_branch, w_out=w_out, norm2_g=norm2_g,
             peer_w_q=peer_w_q, peer_sub_keys=peer_sub_keys, peer_u=peer_u, peer_v=peer_v)
    x_real = x.reshape(b * s, d)
    x_meta = jnp.broadcast_to(meta_tokens.astype(x.dtype)[None], (b, N_META, d)).reshape(b * N_META, d)
    gf = final_norm_g[None].astype(F32)
    tm = _row_tile(b * s, EXPERT_TOKEN_TILE)
    te = EXPERT_BLOCK
    for l in range(depth):
        h_real, h_meta = _layer(x_real, x_meta, p, l, batch=b)
        x_real = _peer(h_real, p, l, gf, tm=tm, te=te, final_norm=l == depth - 1)
        x_meta = _peer(h_meta, p, l, gf, tm=h_meta.shape[0], te=te, final_norm=False)
    return x_real.reshape(b, s, d)
```

```python
import functools

import jax
import jax.numpy as jnp
import numpy as np
from jax import lax
from jax.experimental import pallas as pl
from jax.experimental.pallas import tpu as pltpu

GRID_W = 64
N_META = 16
NA_HEADS = 16
NA_HEAD_DIM = 64
NA_WIN_H = 8
NA_WIN_W = 16
NA_WIDTH = NA_HEADS * NA_HEAD_DIM
MLA_HEADS = 16
MLA_RANK = 512
MLA_NOPE = 128
MLA_ROPE = 64
MLA_V = 128
MLA_QK_PAD = 256
ROPE_THETA = 10000.0
PEER_HEADS = 8
PEER_NKEYS = 128
PEER_DK = 256
PEER_TOPK = 16
NORM_EPS = 1e-6
NEG_INF = -1e30

V7X_VMEM_BYTES = 64 * 1024 * 1024
V7X_VMEM_HEADROOM = 6 * 1024 * 1024
V7X_LANES = 128
BF16_ROWS = 16

NA_QROWS = 4
NA_KROWS = NA_QROWS + NA_WIN_H
NA_PAIR_WIDTH = 2 * NA_HEAD_DIM
KR_WIDTH = 2 * MLA_ROPE

ROW_TILE = 1024
COL_TILE = 1024
MLA_UP_TILE = 256
MLA_Q_CHUNK = 512
ROUTE_TILE = 256
EXPERT_TOKEN_TILE = 512
EXPERT_BLOCK = 1024

F32 = jnp.float32
BF16 = jnp.bfloat16
_F32_MIN = float(np.finfo(np.float32).min)


def _vmem_limit(estimate_bytes):
    want = int(estimate_bytes * 1.25) + 4 * 1024 * 1024
    return min(max(want, 16 * 1024 * 1024), V7X_VMEM_BYTES - V7X_VMEM_HEADROOM)


def _params(semantics, vmem_estimate):
    return pltpu.CompilerParams(dimension_semantics=semantics, vmem_limit_bytes=_vmem_limit(vmem_estimate))


def _row_tile(rows, want):
    tile = min(rows, want)
    assert rows % tile == 0, (rows, tile)
    return tile


def _col_tile(d):
    return COL_TILE if d % COL_TILE == 0 else COL_TILE // 2


def _dot_nt(a, b):
    return lax.dot_general(a, b, (((1,), (1,)), ((), ())), preferred_element_type=F32)


def _rmsnorm(x, g):
    return x * lax.rsqrt(jnp.mean(x * x, axis=-1, keepdims=True) + NORM_EPS) * g


def _pad_meta_rows(a):
    return jnp.concatenate([a, jnp.zeros((V7X_LANES - N_META, a.shape[1]), a.dtype)], axis=0)


def _in_proj_kernel(x_ref, g_ref, w_ref, wkr_ref, ng_ref, o_ref, kr_ref, hn_ref, *, tn):
    j = pl.program_id(1)

    @pl.when(j == 0)
    def _():
        hn = _rmsnorm(x_ref[...], g_ref[...]).astype(BF16)
        hn_ref[...] = hn
        kr_ref[...] = jnp.dot(hn, wkr_ref[...], preferred_element_type=F32)

    acc = jnp.dot(hn_ref[...], w_ref[...], preferred_element_type=F32)
    col0 = j * tn
    plain_end = 3 * NA_WIDTH
    norm_end = plain_end + 2 * MLA_RANK

    @pl.when(col0 < plain_end)
    def _():
        scale = jnp.where(col0 < NA_WIDTH, NA_HEAD_DIM ** -0.5, 1.0).astype(F32)
        o_ref[...] = (acc * scale).astype(BF16)

    @pl.when(jnp.logical_and(col0 >= plain_end, col0 < norm_end))
    def _():
        for c in range(tn // MLA_RANK):
            sl = slice(c * MLA_RANK, (c + 1) * MLA_RANK)
            o_ref[:, sl] = _rmsnorm(acc[:, sl], ng_ref[:, sl]).astype(BF16)

    @pl.when(col0 >= norm_end)
    def _():
        o_ref[...] = (0.5 * jnp.tanh(0.5 * acc) + 0.5).astype(BF16)


def _in_proj(x2d, g1, w_main, w_kr, norm_g, *, tm):
    r, d = x2d.shape
    n = w_main.shape[1]
    tn = _col_tile(d)
    norm_blk0 = (3 * NA_WIDTH) // tn
    norm_nblk = (2 * MLA_RANK) // tn
    est = 2 * (tm * d * 4 + d * tn * 2 + tm * tn * 2 + tm * KR_WIDTH * 4 + d * KR_WIDTH * 2) + tm * d * 2 + tm * tn * 4
    return pl.pallas_call(
        functools.partial(_in_proj_kernel, tn=tn),
        grid=(r // tm, n // tn),
        in_specs=[
            pl.BlockSpec((tm, d), lambda i, j: (i, 0)),
            pl.BlockSpec((1, d), lambda i, j: (0, 0)),
            pl.BlockSpec((d, tn), lambda i, j: (0, j)),
            pl.BlockSpec((d, KR_WIDTH), lambda i, j: (0, 0)),
            pl.BlockSpec((1, tn), lambda i, j: (0, jnp.clip(j - norm_blk0, 0, norm_nblk - 1))),
        ],
        out_specs=[
            pl.BlockSpec((tm, tn), lambda i, j: (i, j)),
            pl.BlockSpec((tm, KR_WIDTH), lambda i, j: (i, 0)),
        ],
        out_shape=[jax.ShapeDtypeStruct((r, n), BF16), jax.ShapeDtypeStruct((r, KR_WIDTH), F32)],
        scratch_shapes=[pltpu.VMEM((tm, d), BF16)],
        compiler_params=_params(("parallel", "arbitrary"), est),
        name="in_proj",
    )(x2d, g1, w_main, w_kr, norm_g)


def _rope_half(t, cos, sin):
    return t * cos + pltpu.roll(t, 64, axis=1) * sin


def _mla_up_kernel(cq_ref, ckv_ref, kr_ref, cos_ref, sin_ref, wq_ref, wkv_ref, q_ref, k_ref, v_ref):
    cos = cos_ref[...]
    sin = sin_ref[...]
    scale = (MLA_NOPE + MLA_ROPE) ** -0.5
    yq = jnp.dot(cq_ref[...], wq_ref[...], preferred_element_type=F32)
    ykv = jnp.dot(ckv_ref[...], wkv_ref[...], preferred_element_type=F32)
    k_rope = _rope_half(kr_ref[...], cos, sin).astype(BF16)
    for h in range(MLA_HEADS):
        c0 = h * MLA_QK_PAD
        q_ref[h, :, 0:MLA_NOPE] = (yq[:, c0:c0 + MLA_NOPE] * scale).astype(BF16)
        q_ref[h, :, MLA_NOPE:] = (_rope_half(yq[:, c0 + MLA_NOPE:c0 + MLA_QK_PAD], cos, sin) * scale).astype(BF16)
        k_ref[h, :, 0:MLA_NOPE] = ykv[:, c0:c0 + MLA_NOPE].astype(BF16)
        k_ref[h, :, MLA_NOPE:] = k_rope
        v_ref[h] = ykv[:, c0 + MLA_NOPE:c0 + MLA_NOPE + MLA_V].astype(BF16)


def _mla_up(proj, kr, cos_t, sin_t, wq_all, wkv_all, *, tm):
    r = proj.shape[0]
    cq_blk = (3 * NA_WIDTH) // MLA_RANK
    n_pos_blk = cos_t.shape[0] // tm
    width = wq_all.shape[1]
    est = (2 * (2 * tm * MLA_RANK * 2 + 3 * tm * V7X_LANES * 4 + 2 * MLA_RANK * width * 2
                + MLA_HEADS * tm * (2 * MLA_QK_PAD + MLA_V) * 2) + 3 * tm * width * 4)
    return pl.pallas_call(
        _mla_up_kernel,
        grid=(r // tm,),
        in_specs=[
            pl.BlockSpec((tm, MLA_RANK), lambda i: (i, cq_blk)),
            pl.BlockSpec((tm, MLA_RANK), lambda i: (i, cq_blk + 1)),
            pl.BlockSpec((tm, KR_WIDTH), lambda i: (i, 0)),
            pl.BlockSpec((tm, V7X_LANES), lambda i: (i % n_pos_blk, 0)),
            pl.BlockSpec((tm, V7X_LANES), lambda i: (i % n_pos_blk, 0)),
            pl.BlockSpec((MLA_RANK, width), lambda i: (0, 0)),
            pl.BlockSpec((MLA_RANK, width), lambda i: (0, 0)),
        ],
        out_specs=[
            pl.BlockSpec((MLA_HEADS, tm, MLA_QK_PAD), lambda i: (0, i, 0)),
            pl.BlockSpec((MLA_HEADS, tm, MLA_QK_PAD), lambda i: (0, i, 0)),
            pl.BlockSpec((MLA_HEADS, tm, MLA_V), lambda i: (0, i, 0)),
        ],
        out_shape=[
            jax.ShapeDtypeStruct((MLA_HEADS, r, MLA_QK_PAD), BF16),
            jax.ShapeDtypeStruct((MLA_HEADS, r, MLA_QK_PAD), BF16),
            jax.ShapeDtypeStruct((MLA_HEADS, r, MLA_V), BF16),
        ],
        compiler_params=_params(("parallel",), est),
        name="mla_up",
    )(proj, proj, kr, cos_t, sin_t, wq_all, wkv_all)


def _mla_attn_kernel(q_ref, k_ref, v_ref, qm_ref, km_ref, vm_ref, o_ref, om_ref, *, tq):
    k = k_ref[0]
    v = v_ref[0]
    km = _pad_meta_rows(km_ref[0])
    vm = _pad_meta_rows(vm_ref[0])
    meta_lane = lax.broadcasted_iota(jnp.int32, (1, V7X_LANES), 1) < N_META

    def attend(q):
        s = _dot_nt(q, k)
        sm = jnp.where(meta_lane, _dot_nt(q, km), NEG_INF)
        m = jnp.maximum(jnp.max(s, axis=-1, keepdims=True), jnp.max(sm, axis=-1, keepdims=True))
        p = jnp.exp(s - m)
        pm = jnp.exp(sm - m)
        l = jnp.sum(p, axis=-1, keepdims=True) + jnp.sum(pm, axis=-1, keepdims=True)
        o = (jnp.dot(p.astype(BF16), v, preferred_element_type=F32)
             + jnp.dot(pm.astype(BF16), vm, preferred_element_type=F32))
        return o / l

    s_len = q_ref.shape[1]
    for c in range(s_len // tq):
        rows = slice(c * tq, (c + 1) * tq)
        o_ref[rows, :] = attend(q_ref[0, rows, :]).astype(BF16)
    om_ref[...] = attend(qm_ref[0]).astype(BF16)


def _mla_attn(q, k, v, qm, km, vm, *, batch):
    r = q.shape[1]
    s_len = r // batch
    tq = min(MLA_Q_CHUNK, s_len)
    est = 2 * (2 * s_len * 256 * 2 + 2 * s_len * 128 * 2) + 3 * tq * s_len * 4 + tq * s_len * 2
    return pl.pallas_call(
        functools.partial(_mla_attn_kernel, tq=tq),
        grid=(batch, MLA_HEADS),
        in_specs=[
            pl.BlockSpec((1, s_len, MLA_QK_PAD), lambda b, h: (h, b, 0)),
            pl.BlockSpec((1, s_len, MLA_QK_PAD), lambda b, h: (h, b, 0)),
            pl.BlockSpec((1, s_len, MLA_V), lambda b, h: (h, b, 0)),
            pl.BlockSpec((1, N_META, MLA_QK_PAD), lambda b, h: (h, b, 0)),
            pl.BlockSpec((1, N_META, MLA_QK_PAD), lambda b, h: (h, b, 0)),
            pl.BlockSpec((1, N_META, MLA_V), lambda b, h: (h, b, 0)),
        ],
        out_specs=[
            pl.BlockSpec((s_len, MLA_V), lambda b, h: (b, h)),
            pl.BlockSpec((N_META, MLA_V), lambda b, h: (b, h)),
        ],
        out_shape=[
            jax.ShapeDtypeStruct((r, MLA_HEADS * MLA_V), BF16),
            jax.ShapeDtypeStruct((batch * N_META, MLA_HEADS * MLA_V), BF16),
        ],
        compiler_params=_params(("parallel", "parallel"), est),
        name="mla_attn",
    )(q, k, v, qm, km, vm)


def _na_bias_tiles(rel_bias):
    qc = np.arange(GRID_W)
    kc = np.arange(GRID_W)
    cstart = np.clip(qc - NA_WIN_W // 2, 0, GRID_W - NA_WIN_W)
    col_ok = (kc[None, :] >= cstart[:, None]) & (kc[None, :] < cstart[:, None] + NA_WIN_W)
    col_off = np.clip(kc[None, :] - qc[:, None], -(NA_WIN_W - 1), NA_WIN_W - 1) + NA_WIN_W - 1
    n_co = 2 * NA_WIN_W - 1
    onehot = np.zeros((n_co, GRID_W * GRID_W), np.float32)
    onehot[col_off.reshape(-1), np.arange(GRID_W * GRID_W)] = 1.0
    h, n_ro = rel_bias.shape[0], rel_bias.shape[1]
    tiles = jnp.dot(rel_bias.astype(F32).reshape(h * n_ro, n_co), jnp.asarray(onehot),
                    precision=lax.Precision.HIGHEST).reshape(h, n_ro, GRID_W, GRID_W)
    return jnp.where(jnp.asarray(col_ok)[None, None], tiles, NEG_INF)


def _na_block_plan(rows):
    nqb = rows // NA_QROWS
    wh = NA_WIN_H
    plan = []
    for qb in (0, 1, nqb - 1):
        kstart = int(np.clip(NA_QROWS * qb - wh // 2, 0, rows - NA_KROWS))
        variant = []
        for i in range(NA_QROWS):
            r = NA_QROWS * qb + i
            rs = int(np.clip(r - wh // 2, 0, rows - wh))
            variant.append([kstart + j - r + wh - 1 if rs <= kstart + j < rs + wh else None
                            for j in range(NA_KROWS)])
        plan.append(variant)
    return plan


def _na_attn_kernel(q_ref, k_ref, v_ref, qm_ref, km_ref, vm_ref, tiles_ref, mb_ref, o_ref, om_ref, bias_ref,
                    *, rows):
    nqb = rows // NA_QROWS
    qblk = NA_QROWS * GRID_W
    kblk = NA_KROWS * GRID_W

    @pl.when(pl.program_id(1) == 0)
    def _():
        masked = jnp.full((GRID_W, GRID_W), NEG_INF, F32)
        for hh in range(2):
            for v, variant in enumerate(_na_block_plan(rows)):
                for i, row_plan in enumerate(variant):
                    for j0 in range(0, NA_KROWS, 2):
                        pair = [masked if ro is None else tiles_ref[0, hh, ro] for ro in row_plan[j0:j0 + 2]]
                        bias_ref[hh, v, i * GRID_W:(i + 1) * GRID_W, j0 * GRID_W:(j0 + 2) * GRID_W] = (
                            jnp.concatenate(pair, axis=1))

    km = _pad_meta_rows(km_ref[...])
    vm = _pad_meta_rows(vm_ref[...])
    lane = lax.broadcasted_iota(jnp.int32, (1, 2 * NA_HEAD_DIM), 1)
    head_lanes = [lane < NA_HEAD_DIM, lane >= NA_HEAD_DIM]

    def softmax_pv(s, sm, v_loc):
        m = jnp.maximum(jnp.max(s, axis=-1, keepdims=True), jnp.max(sm, axis=-1, keepdims=True))
        p = jnp.exp(s - m)
        pm = jnp.exp(sm - m)
        l = jnp.sum(p, axis=-1, keepdims=True) + jnp.sum(pm, axis=-1, keepdims=True)
        o = (jnp.dot(p.astype(BF16), v_loc, preferred_element_type=F32)
             + jnp.dot(pm.astype(BF16), vm, preferred_element_type=F32))
        return o / l

    def block(qb, carry):
        q0 = pl.multiple_of(qb * qblk, qblk)
        k0 = pl.multiple_of(jnp.clip(NA_QROWS * qb - NA_WIN_H // 2, 0, rows - NA_KROWS) * GRID_W, GRID_W)
        variant = jnp.where(qb == 0, 0, jnp.where(qb == nqb - 1, 2, 1))
        q = q_ref[pl.ds(q0, qblk), :]
        k_loc = k_ref[pl.ds(k0, kblk), :]
        v_loc = v_ref[pl.ds(k0, kblk), :]
        outs = []
        for hh in range(2):
            qh = jnp.where(head_lanes[hh], q, jnp.zeros_like(q))
            s = _dot_nt(qh, k_loc) + bias_ref[hh, variant]
            sm = _dot_nt(qh, km) + mb_ref[0, hh]
            outs.append(softmax_pv(s, sm, v_loc))
        o_ref[pl.ds(q0, qblk), :] = jnp.where(head_lanes[0], outs[0], outs[1]).astype(BF16)
        return carry

    lax.fori_loop(0, nqb, block, 0, unroll=2)

    qm = qm_ref[...]
    outs = []
    for hh in range(2):
        qh = jnp.where(head_lanes[hh], qm, jnp.zeros_like(qm))
        sm = _dot_nt(qh, km) + mb_ref[0, hh]
        m = jnp.max(sm, axis=-1, keepdims=True)
        pm = jnp.exp(sm - m)
        l = jnp.sum(pm, axis=-1, keepdims=True)
        outs.append(jnp.dot(pm.astype(BF16), vm, preferred_element_type=F32) / l)
    om_ref[...] = jnp.where(head_lanes[0], outs[0], outs[1]).astype(BF16)


def _na_attn(proj, proj_m, bias_tiles, meta_bias, *, batch):
    r = proj.shape[0]
    s_len = r // batch
    rows = s_len // GRID_W
    assert rows % NA_QROWS == 0 and rows >= NA_KROWS, rows
    npair = NA_HEADS // 2
    qblk = NA_QROWS * GRID_W
    kblk = NA_KROWS * GRID_W
    n_ro = bias_tiles.shape[1]
    tiles5 = bias_tiles.reshape(npair, 2, n_ro, GRID_W, GRID_W)
    mb4 = jnp.concatenate([meta_bias.astype(F32), jnp.full((NA_HEADS, V7X_LANES - N_META), NEG_INF, F32)],
                          axis=1).reshape(npair, 2, 1, V7X_LANES)
    est = 2 * (4 * s_len * NA_PAIR_WIDTH * 2) + 12 * qblk * kblk * 4
    return pl.pallas_call(
        functools.partial(_na_attn_kernel, rows=rows),
        grid=(npair, batch),
        in_specs=[
            pl.BlockSpec((s_len, NA_PAIR_WIDTH), lambda p, b: (b, p)),
            pl.BlockSpec((s_len, NA_PAIR_WIDTH), lambda p, b: (b, npair + p)),
            pl.BlockSpec((s_len, NA_PAIR_WIDTH), lambda p, b: (b, 2 * npair + p)),
            pl.BlockSpec((N_META, NA_PAIR_WIDTH), lambda p, b: (b, p)),
            pl.BlockSpec((N_META, NA_PAIR_WIDTH), lambda p, b: (b, npair + p)),
            pl.BlockSpec((N_META, NA_PAIR_WIDTH), lambda p, b: (b, 2 * npair + p)),
            pl.BlockSpec((1, 2, n_ro, GRID_W, GRID_W), lambda p, b: (p, 0, 0, 0, 0)),
            pl.BlockSpec((1, 2, 1, V7X_LANES), lambda p, b: (p, 0, 0, 0)),
        ],
        out_specs=[
            pl.BlockSpec((s_len, NA_PAIR_WIDTH), lambda p, b: (b, p)),
            pl.BlockSpec((N_META, NA_PAIR_WIDTH), lambda p, b: (b, p)),
        ],
        out_shape=[
            jax.ShapeDtypeStruct((r, NA_WIDTH), BF16),
            jax.ShapeDtypeStruct((batch * N_META, NA_WIDTH), BF16),
        ],
        scratch_shapes=[pltpu.VMEM((2, 3, qblk, kblk), F32)],
        compiler_params=_params(("parallel", "arbitrary"), est),
        name="na_attn",
    )(proj, proj, proj, proj_m, proj_m, proj_m, tiles5, mb4)


def _merge_kernel(ona_ref, omla_ref, wna_ref, wmla_ref, ga_ref, gb_ref, o_ref):
    a = jnp.dot(ona_ref[...], wna_ref[...], preferred_element_type=F32)
    b = jnp.dot(omla_ref[...], wmla_ref[...], preferred_element_type=F32)
    o_ref[...] = (ga_ref[...].astype(F32) * a + gb_ref[...].astype(F32) * b).astype(BF16)


def _merge(o_na, o_mla, w_na, w_mla, proj, *, tm):
    r = o_na.shape[0]
    d = w_na.shape[1]
    tn = _col_tile(d)
    ga_blk = (3 * NA_WIDTH + 2 * MLA_RANK) // tn
    gb_blk = ga_blk + d // tn
    est = 2 * (tm * (NA_WIDTH + MLA_HEADS * MLA_V) * 2 + (NA_WIDTH + MLA_HEADS * MLA_V) * tn * 2
               + 3 * tm * tn * 2) + 3 * tm * tn * 4
    return pl.pallas_call(
        _merge_kernel,
        grid=(r // tm, d // tn),
        in_specs=[
            pl.BlockSpec((tm, NA_WIDTH), lambda i, j: (i, 0)),
            pl.BlockSpec((tm, MLA_HEADS * MLA_V), lambda i, j: (i, 0)),
            pl.BlockSpec((NA_WIDTH, tn), lambda i, j: (0, j)),
            pl.BlockSpec((MLA_HEADS * MLA_V, tn), lambda i, j: (0, j)),
            pl.BlockSpec((tm, tn), lambda i, j: (i, ga_blk + j)),
            pl.BlockSpec((tm, tn), lambda i, j: (i, gb_blk + j)),
        ],
        out_specs=pl.BlockSpec((tm, tn), lambda i, j: (i, j)),
        out_shape=jax.ShapeDtypeStruct((r, d), BF16),
        compiler_params=_params(("parallel", "arbitrary"), est),
        name="merge",
    )(o_na, o_mla, w_na, w_mla, proj, proj)


def _out_proj_kernel(m_ref, w_ref, x_ref, o_ref):
    o_ref[...] = x_ref[...] + jnp.dot(m_ref[...], w_ref[...], preferred_element_type=F32)


def _out_proj(merged, w_out, x2d, *, tm):
    r, d = x2d.shape
    tn = _col_tile(d)
    est = 2 * (tm * d * 2 + d * tn * 2 + 2 * tm * tn * 4) + tm * tn * 4
    return pl.pallas_call(
        _out_proj_kernel,
        grid=(r // tm, d // tn),
        in_specs=[
            pl.BlockSpec((tm, d), lambda i, j: (i, 0)),
            pl.BlockSpec((d, tn), lambda i, j: (0, j)),
            pl.BlockSpec((tm, tn), lambda i, j: (i, j)),
        ],
        out_specs=pl.BlockSpec((tm, tn), lambda i, j: (i, j)),
        out_shape=jax.ShapeDtypeStruct((r, d), F32),
        compiler_params=_params(("parallel", "arbitrary"), est),
        name="out_proj",
    )(merged, w_out, x2d)


def _peer_q_kernel(h_ref, g_ref, w_ref, q_ref, hnt_ref, hn_ref):
    @pl.when(pl.program_id(1) == 0)
    def _():
        hn = _rmsnorm(h_ref[...], g_ref[...])
        hn_ref[...] = hn.astype(BF16)
        hnt_ref[...] = hn.T.astype(BF16)

    q_ref[...] = jnp.dot(hn_ref[...], w_ref[...], preferred_element_type=F32).astype(BF16)


def _peer_q(h2, g2, w_q, *, tm):
    r, d = h2.shape
    n = w_q.shape[1]
    tn = COL_TILE
    est = 2 * (tm * d * 4 + d * tn * 2 + tm * tn * 2 + tm * d * 2) + tm * d * 2 + tm * tn * 4 + 2 * tm * d * 4
    return pl.pallas_call(
        _peer_q_kernel,
        grid=(r // tm, n // tn),
        in_specs=[
            pl.BlockSpec((tm, d), lambda i, j: (i, 0)),
            pl.BlockSpec((1, d), lambda i, j: (0, 0)),
            pl.BlockSpec((d, tn), lambda i, j: (0, j)),
        ],
        out_specs=[
            pl.BlockSpec((tm, tn), lambda i, j: (i, j)),
            pl.BlockSpec((d, tm), lambda i, j: (0, i)),
        ],
        out_shape=[jax.ShapeDtypeStruct((r, n), BF16), jax.ShapeDtypeStruct((d, r), BF16)],
        scratch_shapes=[pltpu.VMEM((tm, d), BF16)],
        compiler_params=_params(("parallel", "arbitrary"), est),
        name="peer_q",
    )(h2, g2, w_q)


def _compare_exchange(a, b):
    if a is None:
        return b, None
    if b is None:
        return a, None
    return jnp.maximum(a, b), jnp.minimum(a, b)


def _bitonic_merge(xs):
    n = len(xs)
    step = n // 2
    while step >= 1:
        for i in range(n):
            if i & step == 0:
                xs[i], xs[i + step] = _compare_exchange(xs[i], xs[i + step])
        step //= 2
    return xs


def _bitonic_sort(xs):
    n = len(xs)
    xs = list(xs)
    size = 2
    while size <= n:
        step = size // 2
        while step >= 1:
            for i in range(n):
                if i & step == 0:
                    hi, lo = _compare_exchange(xs[i], xs[i + step])
                    xs[i], xs[i + step] = (hi, lo) if i & size == 0 else (lo, hi)
            step //= 2
        size *= 2
    return xs


def _merge_topk(xs, ys, k):
    xs = list(xs) + [None] * (k - len(xs))
    ys = list(ys) + [None] * (k - len(ys))
    return _bitonic_merge([_compare_exchange(xs[i], ys[k - 1 - i])[0] for i in range(k)])


def _pow2_ceil(n):
    return 1 << (n - 1).bit_length()


def _sorted_topk_slabs(s, k):
    slabs = _bitonic_sort([s[8 * v:8 * v + 8, :] for v in range(s.shape[0] // 8)])
    for shift in (4, 2, 1):
        slabs = _merge_topk(slabs, [pltpu.roll(x, shift, axis=0) for x in slabs], k)
    return slabs


def _extract_topk(x, k):
    n, t = x.shape
    row = lax.broadcasted_iota(jnp.int32, (n, t), 0).astype(F32)
    krow = lax.broadcasted_iota(jnp.int32, (k, t), 0)

    def body(i, carry):
        x, rank, vals = carry
        m = jnp.max(x, axis=0, keepdims=True)
        first = jnp.min(jnp.where(x == m, row, float(n)), axis=0, keepdims=True)
        hit = row == first
        rank = jnp.where(hit, jnp.asarray(i, F32), rank)
        x = jnp.where(hit, _F32_MIN, x)
        vals = jnp.where(krow == i, m, vals)
        return x, rank, vals

    _, rank, vals = lax.fori_loop(
        0, k, body, (x, jnp.full((n, t), float(k), F32), jnp.zeros((k, t), F32)))
    return vals, rank


def _route_head_exact(s1, s2):
    k = PEER_TOPK
    top1, rank1 = _extract_topk(s1, k)
    top2, rank2 = _extract_topk(s2, k)

    tm = s1.shape[1]
    pieces, spans = [], []
    off = 0
    for i in range(k):
        cnt = k // (i + 1)
        rows = -(-cnt // 8) * 8
        jrow = lax.broadcasted_iota(jnp.int32, (rows, tm), 0)
        pieces.append(jnp.where(jrow < cnt, top1[i:i + 1, :] + top2[0:rows, :], _F32_MIN))
        spans.append((off, rows))
        off += rows
    cand = jnp.concatenate(pieces, axis=0)
    ctop, crank = _extract_topk(cand, k)
    chosen = crank < float(k)
    cmax = ctop[0:1, :]
    z = jnp.sum(jnp.where(chosen, jnp.exp(cand - cmax), 0.0), axis=0, keepdims=True)

    y1 = jnp.ones(s1.shape, F32)
    for i, (off, rows) in enumerate(spans):
        n_sel = jnp.sum(jnp.where(chosen[off:off + rows, :], 1.0, 0.0), axis=0, keepdims=True)
        y1 = jnp.where(rank1 == float(i), 1.0 - n_sel, y1)
    return y1, jnp.exp(s1 - top1[0:1, :]) / z, -rank2, jnp.exp(s2 - top2[0:1, :])


def _route_tables_by_value(scores):
    k = PEER_TOPK
    tm = scores[0][0].shape[1]

    sublane = lax.broadcasted_iota(jnp.int32, (8, tm), 0)
    tops = []
    for side in range(2):
        packed = None
        for h in range(PEER_HEADS):
            slabs = _sorted_topk_slabs(scores[h][side], k)
            packed = slabs if packed is None else [jnp.where(sublane == h, s, p) for s, p in zip(slabs, packed)]
        tops.append(packed)
    top1, top2 = tops

    cells = {(i, j): top1[i] + top2[j] for i in range(k) for j in range(k // (i + 1))}
    grid_rows = [[cells[i, j] for j in range(k // (i + 1))] for i in range(k)]
    lists = [r for r in grid_rows[1:] if len(r) > 1] + [[r[0] for r in grid_rows if len(r) == 1]]
    while len(lists) > 1:
        lists.sort(key=len)
        size = min(k, _pow2_ceil(len(lists[0]) + len(lists[1])))
        merged = [m for m in _merge_topk(lists[0], lists[1], size) if m is not None]
        lists = [merged] + lists[2:]
    ctop = _merge_topk(grid_rows[0], lists[0], k)
    thr, cmax = ctop[k - 1], ctop[0]

    taken = {c: v >= thr for c, v in cells.items()}
    n_taken = sum(jnp.where(t, 1.0, 0.0) for t in taken.values())
    z = sum(jnp.where(taken[c], jnp.exp(v - cmax), 0.0) for c, v in cells.items())
    inv_z = 1.0 / z
    y1_rows = [1.0 - sum(jnp.where(taken[i, j], 1.0, 0.0) for j in range(k // (i + 1))) for i in range(k)]

    exact = jnp.where(n_taken == float(k), 1.0, 0.0)
    for top in (top1, top2):
        for i in range(k - 1):
            exact = jnp.where(top[i] > top[i + 1], exact, 0.0)

    tables = []
    for h in range(PEER_HEADS):
        s1, s2 = scores[h]
        row = slice(h, h + 1)
        y1 = jnp.ones(s1.shape, F32)
        for i in range(k):
            y1 = jnp.where(s1 == top1[i][row], y1_rows[i][row], y1)
        y2 = jnp.full(s2.shape, -float(k), F32)
        for j in reversed(range(k)):
            y2 = jnp.where(s2 >= top2[j][row], -float(j), y2)
        tables.append((y1, jnp.exp(s1 - top1[0][row]) * inv_z[row], y2, jnp.exp(s2 - top2[0][row])))
        for s, top in ((s1, top1), (s2, top2)):
            members = jnp.sum(jnp.where(s >= top[k - 1][row], 1.0, 0.0), axis=0, keepdims=True)
            exact = jnp.where(jnp.logical_and(sublane == h, members != float(k)), 0.0, exact)
    return tables, exact


def _peer_route_kernel(q_ref, keys_ref, y1_ref, e1_ref, y2_ref, e2_ref):
    half = PEER_DK // 2
    scores = []
    for h in range(PEER_HEADS):
        q = q_ref[:, h * PEER_DK:(h + 1) * PEER_DK]
        scores.append((_dot_nt(keys_ref[h, 0], q[:, :half]),
                       _dot_nt(keys_ref[h, 1], q[:, half:])))

    def store(h, tables):
        y1, e1, y2, e2 = tables
        y1_ref[h] = y1
        e1_ref[h] = e1
        y2_ref[h] = pltpu.bitcast(y2.astype(BF16), jnp.uint32)
        e2_ref[h] = pltpu.bitcast(e2.astype(BF16), jnp.uint32)

    tables, exact = _route_tables_by_value(scores)
    for h in range(PEER_HEADS):
        store(h, tables[h])

    for h in range(PEER_HEADS):
        @pl.when(jnp.min(exact[h:h + 1, :]) < 0.5)
        def _(h=h):
            store(h, _route_head_exact(*scores[h]))


def _peer_route(q2, sub_keys, *, tm):
    r = q2.shape[0]
    width = PEER_HEADS * PEER_DK
    tab_spec = pl.BlockSpec((PEER_HEADS, PEER_NKEYS, tm), lambda i: (0, 0, i))
    tab = jax.ShapeDtypeStruct((PEER_HEADS, PEER_NKEYS, r), F32)
    packed_spec = pl.BlockSpec((PEER_HEADS, PEER_NKEYS // 2, tm), lambda i: (0, 0, i))
    packed = jax.ShapeDtypeStruct((PEER_HEADS, PEER_NKEYS // 2, r), jnp.uint32)
    est = (2 * (tm * width * 2 + PEER_HEADS * PEER_NKEYS * PEER_DK * 2 + PEER_HEADS * PEER_NKEYS * tm * 16)
           + 40 * PEER_HEADS * PEER_NKEYS * tm * 4)
    return pl.pallas_call(
        _peer_route_kernel,
        grid=(r // tm,),
        in_specs=[
            pl.BlockSpec((tm, width), lambda i: (i, 0)),
            pl.BlockSpec((PEER_HEADS, 2, PEER_NKEYS, PEER_DK // 2), lambda i: (0, 0, 0, 0)),
        ],
        out_specs=[tab_spec, tab_spec, packed_spec, packed_spec],
        out_shape=[tab, tab, packed, packed],
        compiler_params=_params(("parallel",), est),
        name="peer_route",
    )(q2, sub_keys)


def _peer_dense_kernel(hnt_ref, u_ref, vt_ref, y1_ref, e1_ref, y2_ref, e2_ref, h_ref, g_ref, o_ref,
                       acc_ref, act_ref, p_ref, *, te, final_norm):
    j = pl.program_id(1)
    d, tm = acc_ref.shape

    @pl.when(j == 0)
    def _():
        acc_ref[...] = jnp.zeros_like(acc_ref)

    act = jax.nn.gelu(jnp.dot(u_ref[...], hnt_ref[...], preferred_element_type=F32).astype(BF16))
    act_ref[...] = pltpu.bitcast(act, jnp.uint32)
    a_blocks = te // PEER_NKEYS
    a0 = j * a_blocks
    zero = jnp.zeros((), BF16)
    for ai in range(a_blocks):
        y1_rows = [jnp.broadcast_to(y1_ref[h, pl.ds(a0 + ai, 1), :], (BF16_ROWS, tm)).astype(BF16)
                   for h in range(PEER_HEADS)]
        e1_rows = [jnp.broadcast_to(e1_ref[h, pl.ds(a0 + ai, 1), :], (BF16_ROWS, tm)).astype(BF16)
                   for h in range(PEER_HEADS)]
        for c in range(tm // V7X_LANES):
            cols = slice(c * V7X_LANES, (c + 1) * V7X_LANES)
            for g in range(PEER_NKEYS // BF16_ROWS):
                b_words = slice(g * BF16_ROWS // 2, (g + 1) * BF16_ROWS // 2)
                gate = None
                for h in range(PEER_HEADS):
                    sel = pltpu.bitcast(y2_ref[h, b_words, cols], BF16) >= y1_rows[h][:, cols]
                    term = jnp.where(sel, e1_rows[h][:, cols] * pltpu.bitcast(e2_ref[h, b_words, cols], BF16), zero)
                    gate = term if gate is None else gate + term
                e_words = slice((ai * PEER_NKEYS + g * BF16_ROWS) // 2, (ai * PEER_NKEYS + (g + 1) * BF16_ROWS) // 2)
                p_ref[e_words, cols] = pltpu.bitcast(gate * pltpu.bitcast(act_ref[e_words, cols], BF16), jnp.uint32)
    acc_ref[...] += jnp.dot(vt_ref[...], pltpu.bitcast(p_ref[...], BF16), preferred_element_type=F32)

    @pl.when(j == pl.num_programs(1) - 1)
    def _():
        out = h_ref[...] + acc_ref[...].T
        o_ref[...] = _rmsnorm(out, g_ref[...]) if final_norm else out


def _peer_dense(hn2t, u, vt, tabs, h2, gf, *, tm, te, final_norm):
    d, r = hn2t.shape
    e = u.shape[0]
    y1, e1, y2, e2 = tabs
    tab_spec = pl.BlockSpec((PEER_HEADS, PEER_NKEYS, tm), lambda i, j: (0, 0, i))
    packed_spec = pl.BlockSpec((PEER_HEADS, PEER_NKEYS // 2, tm), lambda i, j: (0, 0, i))
    est = (2 * (tm * d * 2 + 2 * te * d * 2 + PEER_HEADS * PEER_NKEYS * tm * 12 + tm * d * 4)
           + tm * d * 4 + d * tm * 4 + 6 * te * tm * 4)
    return pl.pallas_call(
        functools.partial(_peer_dense_kernel, te=te, final_norm=final_norm),
        grid=(r // tm, e // te),
        in_specs=[
            pl.BlockSpec((d, tm), lambda i, j: (0, i)),
            pl.BlockSpec((te, d), lambda i, j: (j, 0)),
            pl.BlockSpec((d, te), lambda i, j: (0, j)),
            tab_spec, tab_spec, packed_spec, packed_spec,
            pl.BlockSpec((tm, d), lambda i, j: (i, 0), pipeline_mode=pl.Buffered(1)),
            pl.BlockSpec((1, d), lambda i, j: (0, 0)),
        ],
        out_specs=pl.BlockSpec((tm, d), lambda i, j: (i, 0)),
        out_shape=jax.ShapeDtypeStruct((r, d), F32),
        scratch_shapes=[pltpu.VMEM((d, tm), F32), pltpu.VMEM((te // 2, tm), jnp.uint32),
                        pltpu.VMEM((te // 2, tm), jnp.uint32)],
        compiler_params=_params(("parallel", "arbitrary"), est),
        name="peer_dense",
    )(hn2t, u, vt, y1, e1, y2, e2, h2, gf)


def _rope_tables(pos):
    inv_freq = ROPE_THETA ** (-jnp.arange(0, MLA_ROPE, 2, dtype=F32) / MLA_ROPE)
    ang = pos.astype(F32)[:, None] * inv_freq[None, :]
    cos, sin = jnp.cos(ang), jnp.sin(ang)
    zeros = jnp.zeros((pos.shape[0], V7X_LANES - MLA_ROPE), F32)
    return (jnp.concatenate([cos, cos, zeros], axis=1), jnp.concatenate([-sin, sin, zeros], axis=1))


def _swap_halves(w):
    half = w.shape[-1] // 2
    return jnp.concatenate([w[..., half:], w[..., :half]], axis=-1)


def _layer(x_real, x_meta, p, l, *, batch):
    d = x_real.shape[1]
    s_len = x_real.shape[0] // batch
    tm_real = _row_tile(x_real.shape[0], ROW_TILE)
    tm_meta = x_meta.shape[0]

    w_in = p["w_in"][l]
    kr0 = 3 * NA_WIDTH + 2 * MLA_RANK
    w_main = jnp.concatenate([w_in[:, :kr0], w_in[:, kr0 + MLA_ROPE:]], axis=1).astype(BF16)
    w_kr = w_in[:, kr0:kr0 + MLA_ROPE]
    w_kr = jnp.concatenate([w_kr, _swap_halves(w_kr)], axis=1).astype(BF16)
    norm_g = jnp.concatenate([p["mla_q_norm_g"][l], p["mla_kv_norm_g"][l]])[None].astype(F32)
    g1 = p["norm1_g"][l][None].astype(F32)
    wq = p["mla_w_uq"][l].reshape(MLA_RANK, MLA_HEADS, MLA_NOPE + MLA_ROPE)
    wq = jnp.concatenate([wq, _swap_halves(wq[..., MLA_NOPE:])], axis=-1)
    wq = wq.reshape(MLA_RANK, MLA_HEADS * MLA_QK_PAD).astype(BF16)
    wkv = p["mla_w_ukv"][l].astype(BF16)
    cos_r, sin_r = _rope_tables(N_META + jnp.arange(s_len))
    cos_m, sin_m = _rope_tables(jnp.tile(jnp.arange(N_META), batch))
    bias_tiles = _na_bias_tiles(p["na_rel_bias"][l])

    proj_r, kr_r = _in_proj(x_real, g1, w_main, w_kr, norm_g, tm=tm_real)
    proj_m, kr_m = _in_proj(x_meta, g1, w_main, w_kr, norm_g, tm=tm_meta)
    q_r, k_r, v_r = _mla_up(proj_r, kr_r, cos_r, sin_r, wq, wkv, tm=_row_tile(s_len, MLA_UP_TILE))
    q_m, k_m, v_m = _mla_up(proj_m, kr_m, cos_m, sin_m, wq, wkv, tm=tm_meta)
    omla_r, omla_m = _mla_attn(q_r, k_r, v_r, q_m, k_m, v_m, batch=batch)
    ona_r, ona_m = _na_attn(proj_r, proj_m, bias_tiles, p["na_meta_bias"][l], batch=batch)

    w_na = p["w_na_branch"][l].astype(BF16)
    w_mla = p["w_mla_branch"][l].astype(BF16)
    w_out = p["w_out"][l].astype(BF16)
    h_real = _out_proj(_merge(ona_r, omla_r, w_na, w_mla, proj_r, tm=tm_real), w_out, x_real, tm=tm_real)
    h_meta = _out_proj(_merge(ona_m, omla_m, w_na, w_mla, proj_m, tm=tm_meta), w_out, x_meta, tm=tm_meta)
    return h_real, h_meta


def _peer(h2, p, l, gf, *, tm, te, final_norm):
    g2 = p["norm2_g"][l][None].astype(F32)
    w_q = p["peer_w_q"][l].astype(BF16)
    keys = p["peer_sub_keys"][l].astype(BF16)
    u = p["peer_u"][l].astype(BF16)
    vt = p["peer_v"][l].astype(BF16).T
    q2, hn2t = _peer_q(h2, g2, w_q, tm=_row_tile(h2.shape[0], ROW_TILE))
    tabs = _peer_route(q2, keys, tm=_row_tile(h2.shape[0], ROUTE_TILE))
    return _peer_dense(hn2t, u, vt, tabs, h2, gf, tm=tm, te=te, final_norm=final_norm)


def kernel(x, meta_tokens, norm1_g, w_in, na_rel_bias, na_meta_bias, mla_q_norm_g, mla_w_uq, mla_kv_norm_g,
           mla_w_ukv, w_na_branch, w_mla_branch, w_out, norm2_g, peer_w_q, peer_sub_keys, peer_u, peer_v,
           final_norm_g):
    b, s, d = x.shape
    depth = w_in.shape[0]
    p = dict(norm1_g=norm1_g, w_in=w_in, na_rel_bias=na_rel_bias, na_meta_bias=na_meta_bias,
             mla_q_norm_g=mla_q_norm_g, mla_w_uq=mla_w_uq, mla_kv_norm_g=mla_kv_norm_g, mla_w_ukv=mla_w_ukv,
             w_na_branch=w_na_branch, w_mla_branch=w_mla_branch, w_out=w_out, norm2_g=norm2_g,
             peer_w_q=peer_w_q, peer_sub_keys=peer_sub_keys, peer_u=peer_u, peer_v=peer_v)
    x_real = x.reshape(b * s, d)
    x_meta = jnp.broadcast_to(meta_tokens.astype(x.dtype)[None], (b, N_META, d)).reshape(b * N_META, d)
    gf = final_norm_g[None].astype(F32)
    tm = _row_tile(b * s, EXPERT_TOKEN_TILE)
    te = EXPERT_BLOCK
    for l in range(depth):
        h_real, h_meta = _layer(x_real, x_meta, p, l, batch=b)
        x_real = _peer(h_real, p, l, gf, tm=tm, te=te, final_norm=l == depth - 1)
        x_meta = _peer(h_meta, p, l, gf, tm=h_meta.shape[0], te=te, final_norm=False)
    return x_real.reshape(b, s, d)
```
